```python
import jax, jax.numpy as jnp
from jax import lax
import numpy as np

D_MODEL = 2048
BATCH = 4
SEQ = 4096
DEPTH = 1

GRID_W = 64
CTX_LEN = 256

POOL_WINDOWS = (2, 4, 8, 16)
POOL_GROUPS = 4
POOL_GROUP = 256
D_POOL = POOL_GROUPS * POOL_GROUP

HEAD_SIZE = 64
D_ATT = D_MODEL
N_HEADS = D_ATT // HEAD_SIZE
D_DECAY_LORA = 96
D_ICLR_LORA = 96
D_GATE_LORA = 256
GN_EPS = 64e-5
D_SHIFT = 3 * D_ATT + 2 * D_DECAY_LORA + 2 * D_ICLR_LORA + D_GATE_LORA
SHIFT_SPLITS = (D_ATT, 2 * D_ATT, 3 * D_ATT,
                3 * D_ATT + D_DECAY_LORA,
                3 * D_ATT + 2 * D_DECAY_LORA,
                3 * D_ATT + 2 * D_DECAY_LORA + D_ICLR_LORA,
                3 * D_ATT + 2 * D_DECAY_LORA + 2 * D_ICLR_LORA)

D_IN = D_POOL + D_SHIFT + 2 * D_MODEL
IN_SPLITS = (D_POOL, D_POOL + D_SHIFT)

N_EXPERTS = 64
TOP_K = 8
N_GROUPS = 8
TOPK_GROUPS = 4
D_EXPERT = 512
D_SHARED = 512
ROUTED_SCALE = 2.5
MOE_BLOCK = 128

NORM_EPS = 1e-6

kernel_name = "hybrid_pool_rwkv7_moe_prefix_dit_layer"


def rms_norm(x, g):
    xf = x.astype(jnp.float32)
    y = xf * lax.rsqrt(jnp.mean(xf * xf, axis=-1, keepdims=True) + NORM_EPS)
    return (y * g.astype(jnp.float32)).astype(x.dtype)


def grid_shift(u, rows):
    B, T, C = u.shape
    g = u.reshape(B, rows, GRID_W, C // 4, 4)
    left = jnp.pad(g[:, :, :-1, :, 0], ((0, 0), (0, 0), (1, 0), (0, 0)))
    right = jnp.pad(g[:, :, 1:, :, 1], ((0, 0), (0, 0), (0, 1), (0, 0)))
    up = jnp.pad(g[:, :-1, :, :, 2], ((0, 0), (1, 0), (0, 0), (0, 0)))
    down = jnp.pad(g[:, 1:, :, :, 3], ((0, 0), (0, 1), (0, 0), (0, 0)))
    return jnp.stack([left, right, up, down], axis=-1).reshape(B, T, C)


def seq_shift(u):
    B, T, C = u.shape
    g = u.reshape(B, T, C // 2, 2)
    prev = jnp.pad(g[:, :-1, :, 0], ((0, 0), (1, 0), (0, 0)))
    nxt = jnp.pad(g[:, 1:, :, 1], ((0, 0), (0, 1), (0, 0)))
    return jnp.stack([prev, nxt], axis=-1).reshape(B, T, C)


def multiscale_pool(u):
    L = u.shape[-2]
    uf = u.astype(jnp.float32)
    cs = jnp.cumsum(uf, axis=-2)
    cs = jnp.concatenate([jnp.zeros_like(cs[..., :1, :]), cs], axis=-2)
    t = jnp.arange(L)
    outs = []
    for gi, w in enumerate(POOL_WINDOWS):
        sl = slice(gi * POOL_GROUP, (gi + 1) * POOL_GROUP)
        cg = cs[..., sl]
        lo = jnp.clip(t - w // 2, 0, L)
        hi = jnp.clip(t + (w - w // 2), 0, L)
        win_sum = jnp.take(cg, hi, axis=-2) - jnp.take(cg, lo, axis=-2)
        mean = win_sum / (hi - lo).astype(jnp.float32)[:, None]
        outs.append(mean - uf[..., sl])
    return jnp.concatenate(outs, axis=-1).astype(u.dtype)


def pool_branch(u, pool_w, pool_scale, w_pool_out):
    d = multiscale_pool(u)
    d = d.reshape(d.shape[:-1] + (POOL_GROUPS, POOL_GROUP))
    y = jnp.einsum('...gc,gcd->...gd', d, pool_w)
    y = y.reshape(y.shape[:-2] + (D_POOL,)) * pool_scale
    return y @ w_pool_out


def rwkv_prepare(s, w0, w2, a0, a2, k_k, k_a):
    B, T, _ = s.shape
    f32 = jnp.float32
    r, k, v, wd_f, wd_b, ad_f, ad_b, gd = jnp.split(s, SHIFT_SPLITS, axis=-1)

    def heads(a):
        return a.astype(f32).reshape(B, T, N_HEADS, HEAD_SIZE)

    def decay(wd, w0_d, w2_d):
        w = -jax.nn.softplus(-(w0_d + jnp.tanh(wd) @ w2_d).astype(f32)) - 0.5
        return heads(jnp.exp(-jnp.exp(w)))

    def iclr(ad, a0_d, a2_d):
        return heads(jax.nn.sigmoid((a0_d + ad @ a2_d).astype(f32)))

    kh = heads(k)
    kk = kh * k_k.astype(f32).reshape(N_HEADS, HEAD_SIZE)
    kk = kk * lax.rsqrt(jnp.sum(kk * kk, axis=-1, keepdims=True) + 1e-12)
    ka = k_a.astype(f32).reshape(N_HEADS, HEAD_SIZE)
    a_f = iclr(ad_f, a0[0], a2[0])
    a_b = iclr(ad_b, a0[1], a2[1])
    k_f = kh * (1.0 + (a_f - 1.0) * ka)
    k_b = kh * (1.0 + (a_b - 1.0) * ka)
    return (heads(r), heads(v), kk,
            decay(wd_f, w0[0], w2[0]), decay(wd_b, w0[1], w2[1]),
            k_f, k_b, kk * a_f, kk * a_b, gd)


def wkv_scan(s0, r, w, k, v, kk, b, reverse, emit):
    def tm(a):
        return jnp.moveaxis(a, 1, 0)

    def step(S, inp):
        r_t, w_t, k_t, v_t, kk_t, b_t = inp
        S = (S * w_t[:, :, None, :]
             - jnp.einsum('bhij,bhj->bhi', S, kk_t)[..., None] * b_t[:, :, None, :]
             + v_t[..., None] * k_t[:, :, None, :])
        y = jnp.einsum('bhij,bhj->bhi', S, r_t) if emit else None
        return S, y

    S, ys = lax.scan(step, s0, (tm(r), tm(w), tm(k), tm(v), tm(kk), tm(b)), reverse=reverse)
    return S, (jnp.moveaxis(ys, 0, 1) if emit else None)


def rwkv_readout(y, r, v, k_f, k_b, gd, g2, r_k, lnx_w, lnx_b, w_rwkv_out):
    B, T = y.shape[:2]
    f32 = jnp.float32
    mean = jnp.mean(y, axis=-1, keepdims=True)
    var = jnp.mean(jnp.square(y - mean), axis=-1, keepdims=True)
    yn = ((y - mean) * lax.rsqrt(var + GN_EPS)).reshape(B, T, D_ATT)
    yn = yn * lnx_w.astype(f32) + lnx_b.astype(f32)
    bonus = jnp.sum(r * (k_f + k_b) * r_k.astype(f32), axis=-1, keepdims=True) * v
    g = (jax.nn.sigmoid(gd) @ g2).astype(f32)
    out = (yn + bonus.reshape(B, T, D_ATT)) * g
    return out.astype(gd.dtype) @ w_rwkv_out


def merge_branches(y_pool, y_rwkv, gates, w_out_l):
    g_pool, g_rwkv = jnp.split(gates, 2, axis=-1)
    return (jax.nn.sigmoid(g_pool) * y_pool + jax.nn.sigmoid(g_rwkv) * y_rwkv) @ w_out_l


def moe_ffn(h, router_w, router_bias, w_gate, w_up, w_down, sh_gate, sh_up, sh_down):
    shp = h.shape
    f32 = jnp.float32
    ht = h.reshape(-1, D_MODEL)
    n = ht.shape[0]
    scores = jax.nn.sigmoid((ht @ router_w).astype(f32))
    sel = scores + router_bias.astype(f32)
    grp = sel.reshape(n, N_GROUPS, N_EXPERTS // N_GROUPS)
    grp_score = jnp.sum(lax.top_k(grp, 2)[0], axis=-1)
    _, gidx = lax.top_k(grp_score, TOPK_GROUPS)
    gmask = jnp.sum(jax.nn.one_hot(gidx, N_GROUPS, dtype=f32), axis=-2)
    emask = jnp.repeat(gmask, N_EXPERTS // N_GROUPS, axis=-1)
    _, eidx = lax.top_k(jnp.where(emask > 0, sel, -jnp.inf), TOP_K)
    wsel = jnp.take_along_axis(scores, eidx, axis=-1)
    wsel = wsel / jnp.sum(wsel, axis=-1, keepdims=True) * ROUTED_SCALE

    nk = n * TOP_K
    cap = nk + N_EXPERTS * MOE_BLOCK
    n_blocks = cap // MOE_BLOCK
    e_flat = eidx.reshape(-1)
    tok_flat = jnp.repeat(jnp.arange(n, dtype=jnp.int32), TOP_K)
    oh = jax.nn.one_hot(e_flat, N_EXPERTS, dtype=jnp.int32)
    csum = jnp.cumsum(oh, axis=0)
    rank = jnp.take_along_axis(csum, e_flat[:, None], axis=-1)[:, 0] - 1
    counts = csum[-1]
    padded = (counts + MOE_BLOCK - 1) // MOE_BLOCK * MOE_BLOCK
    pstarts = jnp.cumsum(padded) - padded
    dest = pstarts[e_flat] + rank
    tok_buf = jnp.zeros((cap,), jnp.int32).at[dest].set(tok_flat)
    w_buf = jnp.zeros((cap,), f32).at[dest].set(wsel.reshape(-1))
    block_start = jnp.arange(n_blocks, dtype=jnp.int32) * MOE_BLOCK
    block_expert = jnp.minimum(
        jnp.sum(block_start[:, None] >= (pstarts + padded)[None, :], axis=-1), N_EXPERTS - 1)

    def expert_block(args):
        tok, wgt, e = args
        hb = ht[tok]
        act = jax.nn.silu(hb @ w_gate[e]) * (hb @ w_up[e])
        return (act @ w_down[e]) * wgt[:, None].astype(hb.dtype)

    y = lax.map(expert_block, (tok_buf.reshape(n_blocks, MOE_BLOCK),
                               w_buf.reshape(n_blocks, MOE_BLOCK), block_expert))
    routed = jnp.zeros_like(ht).at[tok_buf].add(y.reshape(cap, D_MODEL))
    shared = (jax.nn.silu(ht @ sh_gate) * (ht @ sh_up)) @ sh_down
    return (routed + shared).reshape(shp)


def setup_inputs(seed: int = 0) -> dict:
    key = jax.random.key(seed)
    keys = jax.random.split(key, 40)
    counter = iter(range(40))

    def nrm(shape, scale):
        return jax.random.normal(keys[next(counter)], shape, jnp.float32) * scale

    def uni(shape, lo, hi):
        return jax.random.uniform(keys[next(counter)], shape, jnp.float32, lo, hi)

    L, D = DEPTH, D_MODEL
    return {
        "x": nrm((BATCH, SEQ, D), 1.0),
        "c": nrm((BATCH, D), 1.0),
        "ctx": nrm((BATCH, CTX_LEN, D), 1.0),
        "c_ctx": nrm((D,), 1.0),
        "norm1_g": 1.0 + nrm((L, D), 0.05),
        "norm2_g": 1.0 + nrm((L, D), 0.05),
        "ada_w": nrm((L, D, 6 * D), 0.5 * D ** -0.5),
        "ada_b": nrm((L, 6 * D), 0.02),
        "w_in": nrm((L, D, D_IN), D ** -0.5),
        "shift_mu": uni((L, D_SHIFT), 0.0, 1.0),
        "pool_w": nrm((L, POOL_GROUPS, POOL_GROUP, POOL_GROUP), POOL_GROUP ** -0.5),
        "pool_scale": 1.0 + nrm((L, D_POOL), 0.1),
        "w_pool_out": nrm((L, D_POOL, D), D_POOL ** -0.5),
        "decay_w0": uni((L, 2, D_ATT), -6.0, 0.0),
        "decay_w2": nrm((L, 2, D_DECAY_LORA, D_ATT), 0.5 * D_DECAY_LORA ** -0.5),
        "iclr_a0": nrm((L, 2, D_ATT), 0.5),
        "iclr_a2": nrm((L, 2, D_ICLR_LORA, D_ATT), 0.5 * D_ICLR_LORA ** -0.5),
        "gate_g2": nrm((L, D_GATE_LORA, D_ATT), D_GATE_LORA ** -0.5),
        "k_k": 0.85 + nrm((L, D_ATT), 0.1),
        "k_a": 1.0 + nrm((L, D_ATT), 0.1),
        "r_k": nrm((L, N_HEADS, HEAD_SIZE), 0.1),
        "lnx_w": 1.0 + nrm((L, D_ATT), 0.05),
        "lnx_b": nrm((L, D_ATT), 0.02),
        "w_rwkv_out": nrm((L, D_ATT, D), D_ATT ** -0.5),
        "w_out": nrm((L, D, D), D ** -0.5),
        "router_w": nrm((L, D, N_EXPERTS), D ** -0.5),
        "router_bias": nrm((L, N_EXPERTS), 0.01),
        "exp_w_gate": nrm((L, N_EXPERTS, D, D_EXPERT), D ** -0.5),
        "exp_w_up": nrm((L, N_EXPERTS, D, D_EXPERT), D ** -0.5),
        "exp_w_down": nrm((L, N_EXPERTS, D_EXPERT, D), D_EXPERT ** -0.5),
        "shared_w_gate": nrm((L, D, D_SHARED), D ** -0.5),
        "shared_w_up": nrm((L, D, D_SHARED), D ** -0.5),
        "shared_w_down": nrm((L, D_SHARED, D), D_SHARED ** -0.5),
        "final_g": 1.0 + nrm((D,), 0.05),
    }


def reference(x, c, ctx, c_ctx, norm1_g, norm2_g, ada_w, ada_b, w_in, shift_mu, pool_w,
              pool_scale, w_pool_out, decay_w0, decay_w2, iclr_a0, iclr_a2, gate_g2, k_k, k_a,
              r_k, lnx_w, lnx_b, w_rwkv_out, w_out, router_w, router_bias, exp_w_gate, exp_w_up,
              exp_w_down, shared_w_gate, shared_w_up, shared_w_down, final_g):
    B, T, _ = x.shape
    rows = T // GRID_W
    s0 = jnp.zeros((B, N_HEADS, HEAD_SIZE, HEAD_SIZE), jnp.float32)
    xc = ctx
    for l in range(DEPTH):
        last = l == DEPTH - 1
        rw = (decay_w0[l], decay_w2[l], iclr_a0[l], iclr_a2[l], k_k[l], k_a[l])
        ro = (gate_g2[l], r_k[l], lnx_w[l], lnx_b[l], w_rwkv_out[l])
        po = (pool_w[l], pool_scale[l], w_pool_out[l])
        moe = (router_w[l], router_bias[l], exp_w_gate[l], exp_w_up[l], exp_w_down[l],
               shared_w_gate[l], shared_w_up[l], shared_w_down[l])

        mod = jax.nn.silu(c) @ ada_w[l] + ada_b[l]
        sh1, sc1, gt1, sh2, sc2, gt2 = [m[:, None, :] for m in jnp.split(mod, 6, axis=-1)]
        modc = jax.nn.silu(c_ctx) @ ada_w[l] + ada_b[l]
        sh1c, sc1c, gt1c, sh2c, sc2c, gt2c = jnp.split(modc, 6, axis=-1)

        hc = rms_norm(xc, norm1_g[l]) * (1.0 + sc1c) + sh1c
        if last:
            slab_c = hc @ w_in[l][:, D_POOL:D_POOL + D_SHIFT]
        else:
            uc, slab_c, gates_c = jnp.split(hc @ w_in[l], IN_SPLITS, axis=-1)
        slab_c = slab_c + (seq_shift(slab_c) - slab_c) * shift_mu[l]
        rc, vc, kkc, wfc, wbc, kfc, kbc, bfc, bbc, gdc = rwkv_prepare(slab_c, *rw)
        st_f, yc_f = wkv_scan(s0, rc, wfc, kfc, vc, kkc, bfc, False, not last)
        st_b, yc_b = wkv_scan(s0, rc, wbc, kbc, vc, kkc, bbc, True, not last)

        h = rms_norm(x, norm1_g[l]) * (1.0 + sc1) + sh1
        u, slab, gates = jnp.split(h @ w_in[l], IN_SPLITS, axis=-1)
        slab = slab + (grid_shift(slab, rows) - slab) * shift_mu[l]
        r, v, kk, wf, wb, kf, kb, bf, bb, gd = rwkv_prepare(slab, *rw)
        _, y_f = wkv_scan(st_f, r, wf, kf, v, kk, bf, False, True)
        _, y_b = wkv_scan(st_b, r, wb, kb, v, kk, bb, True, True)
        y_rwkv = rwkv_readout(y_f + y_b, r, v, kf, kb, gd, *ro)
        y_pool = pool_branch(u.reshape(B, rows, GRID_W, D_POOL), *po).reshape(B, T, D_MODEL)
        x = x + gt1 * merge_branches(y_pool, y_rwkv, gates, w_out[l])

        x = x + gt2 * moe_ffn(rms_norm(x, norm2_g[l]) * (1.0 + sc2) + sh2, *moe)

        if not last:
            yc_rwkv = rwkv_readout(yc_f + yc_b, rc, vc, kfc, kbc, gdc, *ro)
            yc_pool = pool_branch(uc, *po)
            xc = xc + gt1c * merge_branches(yc_pool, yc_rwkv, gates_c, w_out[l])
            xc = xc + gt2c * moe_ffn(rms_norm(xc, norm2_g[l]) * (1.0 + sc2c) + sh2c, *moe)
    return rms_norm(x, final_g)
```

```python
import functools

import jax
import jax.numpy as jnp
from jax import lax
from jax.experimental import pallas as pl
from jax.experimental.pallas import tpu as pltpu

F32 = jnp.float32
BF16 = jnp.bfloat16
I32 = jnp.int32
U32 = jnp.uint32

D_MODEL = 2048
GRID_W = 64
POOL_WINDOWS = (2, 4, 8, 16)
POOL_GROUP = 256
D_POOL = 1024
HEAD = 64
N_HEADS = 32
N_PAIRS = N_HEADS // 2
D_ATT = 2048
D_LORA = 96
D_GATE_LORA = 256
D_LORA_PAD = 768
GN_EPS = 64e-5
NORM_EPS = 1e-6
N_EXPERTS = 64
TOP_K = 8
N_GROUPS = 8
TOPK_GROUPS = 4
D_EXPERT = 512
ROUTED_SCALE = 2.5
EXP_M05 = 0.6065306597126334

LANES = 128
CHUNK = 64
EXPERT_BLOCK = 512
VMEM_LIMIT = 56 * 1024 * 1024


def _cparams(sem):
    return pltpu.CompilerParams(dimension_semantics=sem, vmem_limit_bytes=VMEM_LIMIT)


def _dot(a, b):
    return jnp.dot(a, b, preferred_element_type=F32)


def _dot_nt(a, b):
    return lax.dot_general(a, b, (((1,), (1,)), ((), ())), preferred_element_type=F32)


def _dot_tn(a, b):
    return lax.dot_general(a, b, (((0,), (0,)), ((), ())), preferred_element_type=F32)


def _split2(x):
    hi = x.astype(BF16)
    lo = (x - hi.astype(F32)).astype(BF16)
    return hi, lo


def _iota(shape, axis):
    return lax.broadcasted_iota(I32, shape, axis)


def _ada_kernel(c_ref, w_ref, b_ref, o_ref):
    c = c_ref[...]
    a = c * jax.nn.sigmoid(c)
    o_ref[...] = _dot(a.astype(BF16), w_ref[...].astype(BF16)) + b_ref[...]


def _ada_mod(cstack, ada_w, ada_b):
    m, d = cstack.shape
    n = ada_w.shape[1]
    tn = 1024
    return pl.pallas_call(
        _ada_kernel,
        grid=(n // tn,),
        in_specs=[pl.BlockSpec((m, d), lambda j: (0, 0)),
                  pl.BlockSpec((d, tn), lambda j: (0, j)),
                  pl.BlockSpec((1, tn), lambda j: (0, j))],
        out_specs=pl.BlockSpec((m, tn), lambda j: (0, j)),
        out_shape=jax.ShapeDtypeStruct((m, n), F32),
        compiler_params=_cparams(("parallel",)),
        name="ada_mod",
    )(cstack, ada_w, ada_b.reshape(1, n))


def _norm_mod_kernel(x_ref, g_ref, sc_ref, sh_ref, o_ref):
    x = x_ref[0]
    ms = jnp.mean(x * x, axis=-1, keepdims=True)
    y = x * lax.rsqrt(ms + NORM_EPS) * g_ref[...]
    o_ref[0] = (y * (1.0 + sc_ref[0]) + sh_ref[0]).astype(o_ref.dtype)


def _norm_mod(x, g, sc, sh, tt):
    b, t, d = x.shape
    return pl.pallas_call(
        _norm_mod_kernel,
        grid=(b, t // tt),
        in_specs=[pl.BlockSpec((1, tt, d), lambda i, j: (i, j, 0)),
                  pl.BlockSpec((1, d), lambda i, j: (0, 0)),
                  pl.BlockSpec((1, 1, d), lambda i, j: (i, 0, 0)),
                  pl.BlockSpec((1, 1, d), lambda i, j: (i, 0, 0))],
        out_specs=pl.BlockSpec((1, tt, d), lambda i, j: (i, j, 0)),
        out_shape=jax.ShapeDtypeStruct((b, t, d), BF16),
        compiler_params=_cparams(("parallel", "parallel")),
        name="norm_mod",
    )(x, g.reshape(1, d), sc, sh)


def _mm_kernel(a_ref, b_ref, o_ref):
    o_ref[...] = _dot(a_ref[...], b_ref[...]).astype(o_ref.dtype)


def _matmul(a, b, out_dtype, tm, tn):
    m, k = a.shape
    n = b.shape[1]
    tm = min(tm, m)
    return pl.pallas_call(
        _mm_kernel,
        grid=(m // tm, n // tn),
        in_specs=[pl.BlockSpec((tm, k), lambda i, j: (i, 0)),
                  pl.BlockSpec((k, tn), lambda i, j: (0, j))],
        out_specs=pl.BlockSpec((tm, tn), lambda i, j: (i, j)),
        out_shape=jax.ShapeDtypeStruct((m, n), out_dtype),
        compiler_params=_cparams(("parallel", "parallel")),
        name="matmul",
    )(a, b)


def _shift_grid(x, prev, nxt, first, last):
    tt, c = x.shape
    col = _iota((tt, c), 0) & (GRID_W - 1)
    m = _iota((tt, c), 1) & 3
    left = jnp.where(col == 0, 0.0, pltpu.roll(x, 1, 0))
    right = jnp.where(col == GRID_W - 1, 0.0, pltpu.roll(x, tt - 1, 0))
    prev = jnp.where(first, 0.0, prev)
    nxt = jnp.where(last, 0.0, nxt)
    if tt > GRID_W:
        up = jnp.concatenate([prev, x[:tt - GRID_W]], axis=0)
        down = jnp.concatenate([x[GRID_W:], nxt], axis=0)
    else:
        up, down = prev, nxt
    return jnp.where(m == 0, left, jnp.where(m == 1, right, jnp.where(m == 2, up, down)))


def _shift_seq(x, prev8, next8, first, last):
    t, c = x.shape
    row = _iota((t, c), 0)
    odd = (_iota((t, c), 1) & 1) == 1
    before = jnp.where(first, 0.0, prev8[7:8])
    after = jnp.where(last, 0.0, next8[0:1])
    prev = jnp.where(row == 0, before, pltpu.roll(x, 1, 0))
    nxt = jnp.where(row == t - 1, after, pltpu.roll(x, t - 1, 0))
    return jnp.where(odd, nxt, prev)


def _head_sum(x):
    w = 2 * LANES
    ones = (_iota((w, w), 0) >> 6 == _iota((w, w), 1) >> 6).astype(BF16)
    outs = []
    for c in range(x.shape[1] // w):
        hi, lo = _split2(x[:, c * w:(c + 1) * w])
        outs.append(_dot(hi, ones) + _dot(lo, ones))
    return jnp.concatenate(outs, axis=1)


def _prepare_kernel(*refs, grid_mode):
    (r_ref, rp_ref, rn_ref, k_ref, kp_ref, kn_ref, v_ref, vp_ref, vn_ref,
     l_ref, lp_ref, ln_ref) = refs[:12]
    rest = refs[12:]
    (mur_ref, muk_ref, muv_ref, mul_ref, w2f_ref, w2b_ref, a2f_ref, a2b_ref,
     w0f_ref, w0b_ref, a0f_ref, a0b_ref, kk_ref, ka_ref, rk_ref,
     or_ref, ov_ref, okk_ref, olwf_ref, olwb_ref, okf_ref, okb_ref, obf_ref, obb_ref,
     obon_ref, ogd_ref) = rest

    first = pl.program_id(1) == 0
    last = pl.program_id(1) == pl.num_programs(1) - 1
    shift = _shift_grid if grid_mode else _shift_seq

    def mix(x_ref, p_ref, n_ref, mu_ref):
        x = x_ref[0]
        return x + (shift(x, p_ref[0], n_ref[0], first, last) - x) * mu_ref[...]

    r = mix(r_ref, rp_ref, rn_ref, mur_ref)
    k = mix(k_ref, kp_ref, kn_ref, muk_ref)
    v = mix(v_ref, vp_ref, vn_ref, muv_ref)
    lo = mix(l_ref, lp_ref, ln_ref, mul_ref)

    th = jnp.tanh(lo[:, :2 * LANES]).astype(BF16)
    zf = w0f_ref[...] + _dot(th[:, :LANES], w2f_ref[...])
    zb = w0b_ref[...] + _dot(th[:, LANES:], w2b_ref[...])
    lwf = -EXP_M05 * jax.nn.sigmoid(zf)
    lwb = -EXP_M05 * jax.nn.sigmoid(zb)
    ad = lo[:, 2 * LANES:4 * LANES].astype(BF16)
    af = jax.nn.sigmoid(a0f_ref[...] + _dot(ad[:, :LANES], a2f_ref[...]))
    ab = jax.nn.sigmoid(a0b_ref[...] + _dot(ad[:, LANES:], a2b_ref[...]))
    ogd_ref[0] = lo[:, 4 * LANES:]

    kk = k * kk_ref[...]
    kk = kk * lax.rsqrt(_head_sum(kk * kk) + 1e-12)
    ka = ka_ref[...]
    kf = k * (1.0 + (af - 1.0) * ka)
    kb = k * (1.0 + (ab - 1.0) * ka)
    bonus = _head_sum(r * (kf + kb) * rk_ref[...]) * v

    outs = ((or_ref, r), (ov_ref, v), (okk_ref, kk), (olwf_ref, lwf), (olwb_ref, lwb),
            (okf_ref, kf), (okb_ref, kb), (obf_ref, kk * af), (obb_ref, kk * ab), (obon_ref, bonus))
    for o_ref, val in outs:
        for p in range(N_PAIRS):
            o_ref[0, p] = val[:, p * LANES:(p + 1) * LANES]


def _prepare(rkv, lora, pw, grid_mode):
    b, t, _ = rkv.shape
    d = D_ATT
    tt = 2 * GRID_W
    halo = GRID_W if grid_mode else 8
    hpt = tt // halo
    nhalo = t // halo
    grid = (b, t // tt)

    def tile_specs(w, c):
        return [pl.BlockSpec((1, tt, w), lambda i, j: (i, j, c)),
                pl.BlockSpec((1, halo, w), lambda i, j: (i, jnp.maximum(j * hpt - 1, 0), c)),
                pl.BlockSpec((1, halo, w), lambda i, j: (i, jnp.minimum((j + 1) * hpt, nhalo - 1), c))]

    in_specs = []
    args = []
    for c in range(3):
        in_specs += tile_specs(d, c)
        args += [rkv, rkv, rkv]
    in_specs += tile_specs(D_LORA_PAD, 0)
    args += [lora, lora, lora]

    def vec(c=0, w=d):
        return pl.BlockSpec((1, w), lambda i, j, c=c: (0, c))

    def full(shape):
        return pl.BlockSpec(shape, lambda i, j: (0,) * len(shape))

    in_specs += [vec(0), vec(1), vec(2), vec(0, D_LORA_PAD)]
    args += [pw["mu_rkv"], pw["mu_rkv"], pw["mu_rkv"], pw["mu_lora"]]
    in_specs += [full((LANES, d))] * 4
    args += [pw["w2f"], pw["w2b"], pw["a2f"], pw["a2b"]]
    in_specs += [vec()] * 7
    args += [pw["w0f"], pw["w0b"], pw["a0f"], pw["a0b"], pw["k_k"], pw["k_a"], pw["r_k"]]

    pair = jax.ShapeDtypeStruct((b, N_PAIRS, t, LANES), F32)
    pair_spec = pl.BlockSpec((1, N_PAIRS, tt, LANES), lambda i, j: (i, 0, j, 0))
    out_shape = [pair] * 10 + [jax.ShapeDtypeStruct((b, t, D_GATE_LORA), F32)]
    out_specs = [pair_spec] * 10 + [pl.BlockSpec((1, tt, D_GATE_LORA), lambda i, j: (i, j, 0))]
    return pl.pallas_call(
        functools.partial(_prepare_kernel, grid_mode=grid_mode),
        grid=grid, in_specs=in_specs, out_specs=out_specs, out_shape=out_shape,
        compiler_params=_cparams(("parallel", "parallel")),
        name="prepare_grid" if grid_mode else "prepare_seq",
    )(*args)


def _scan_kernel(*refs, reverse, pairs, tb, emit):
    r_ref, lw_ref, k_ref, v_ref, kk_ref, b_ref, s0_ref = refs[:7]
    if emit:
        y_ref, st_ref, s_scr = refs[7:]
    else:
        st_ref, s_scr = refs[7:]
    L = CHUNK
    n_chunks = tb // L

    @pl.when(pl.program_id(1) == 0)
    def _():
        s_scr[...] = s0_ref[...]

    ti = _iota((L, L), 0)
    si = _iota((L, L), 1)
    tri = ((si >= ti) if reverse else (si <= ti)).astype(BF16)
    t2 = _iota((2 * L, 2 * L), 0)
    s2 = _iota((2 * L, 2 * L), 1)
    same = (t2 >> 6) == (s2 >> 6)
    tl = t2 & (L - 1)
    sl = s2 & (L - 1)
    strict = same & ((sl > tl) if reverse else (sl < tl))
    incl = same & ((sl >= tl) if reverse else (sl <= tl))
    eye = (t2 == s2).astype(F32)
    head_a = _iota((L, LANES), 1) < HEAD

    def stack(x):
        return jnp.concatenate([jnp.where(head_a, x, 0.0), jnp.where(head_a, 0.0, x)], axis=0).astype(BF16)

    def chunk(ci, carry):
        cc = (n_chunks - 1 - ci) if reverse else ci
        rows = pl.ds(pl.multiple_of(cc * L, L), L)
        for p in range(pairs):
            lw = lw_ref[p, rows, :]
            hi = lw.astype(BF16)
            r1 = lw - hi.astype(F32)
            mid = r1.astype(BF16)
            lo = (r1 - mid.astype(F32)).astype(BF16)
            cum = _dot(tri, hi) + _dot(tri, mid) + _dot(tri, lo)
            w_inc = jnp.exp(cum)
            w_exc = jnp.exp(cum - lw)
            w_inv = jnp.exp(-cum)
            a_s = stack(w_exc * kk_ref[p, rows, :])
            b_s = stack(b_ref[p, rows, :] * w_inv)
            k_s = stack(k_ref[p, rows, :] * w_inv)
            v_s = stack(v_ref[p, rows, :])
            m1 = jnp.where(strict, _dot_nt(a_s, b_s), 0.0)
            m2 = jnp.where(strict, _dot_nt(a_s, k_s), 0.0)
            xp = -m1
            tinv = eye + xp
            for _ in range(5):
                xb = xp.astype(BF16)
                xp = _dot(xb, xb)
                tinv = tinv + _dot(tinv.astype(BF16), xp.astype(BF16))
            s = s_scr[p]
            sb = s.astype(BF16)
            g = _dot_nt(a_s, sb) + _dot(m2.astype(BF16), v_s)
            u_s = (-_dot(tinv.astype(BF16), g.astype(BF16))).astype(BF16)
            if emit:
                r_s = stack(w_inc * r_ref[p, rows, :])
                n1 = jnp.where(incl, _dot_nt(r_s, b_s), 0.0)
                n2 = jnp.where(incl, _dot_nt(r_s, k_s), 0.0)
                y2 = _dot_nt(r_s, sb) + _dot(n1.astype(BF16), u_s) + _dot(n2.astype(BF16), v_s)
                y_ref[p, rows, :] = y2[:L] + y2[L:]
            w_last = w_inc[0:1] if reverse else w_inc[L - 1:L]
            s_scr[p] = (s + _dot_tn(u_s, b_s) + _dot_tn(v_s, k_s)) * w_last
        return carry

    lax.fori_loop(0, n_chunks, chunk, 0)

    @pl.when(pl.program_id(1) == pl.num_programs(1) - 1)
    def _():
        st_ref[...] = s_scr[...]


def _scan(r, lw, k, v, kk, b, s0, reverse, emit):
    bp, t, _ = r.shape
    pairs = 4
    tb = min(t, 512)
    nb = t // tb

    def tmap(g, c):
        return (g, (nb - 1 - c) if reverse else c, 0)

    data = pl.BlockSpec((pairs, tb, LANES), tmap)
    state = pl.BlockSpec((pairs, LANES, LANES), lambda g, c: (g, 0, 0))
    out_shape = [jax.ShapeDtypeStruct((bp, LANES, LANES), F32)]
    out_specs = [state]
    if emit:
        out_shape = [jax.ShapeDtypeStruct((bp, t, LANES), F32)] + out_shape
        out_specs = [data] + out_specs
    res = pl.pallas_call(
        functools.partial(_scan_kernel, reverse=reverse, pairs=pairs, tb=tb, emit=emit),
        grid=(bp // pairs, nb),
        in_specs=[data] * 6 + [state],
        out_specs=out_specs, out_shape=out_shape,
        scratch_shapes=[pltpu.VMEM((pairs, LANES, LANES), F32)],
        compiler_params=_cparams(("parallel", "arbitrary")),
        name="scan_" + ("bwd" if reverse else "fwd") + ("_emit" if emit else "_state"),
    )(r, lw, k, v, kk, b, s0)
    return (res[0], res[1]) if emit else (None, res[0])


def _readout_kernel(yf_ref, yb_ref, bon_ref, gd_ref, lnw_ref, lnb_ref, g2_ref, w_ref, o_ref):
    y = jnp.concatenate([yf_ref[0, p] + yb_ref[0, p] for p in range(N_PAIRS)], axis=1)
    bonus = jnp.concatenate([bon_ref[0, p] for p in range(N_PAIRS)], axis=1)
    mean = _head_sum(y) * (1.0 / HEAD)
    dlt = y - mean
    var = _head_sum(dlt * dlt) * (1.0 / HEAD)
    yn = dlt * lax.rsqrt(var + GN_EPS) * lnw_ref[...] + lnb_ref[...]
    gate = _dot(jax.nn.sigmoid(gd_ref[0]).astype(BF16), g2_ref[...])
    out = ((yn + bonus) * gate).astype(BF16)
    o_ref[0] = _dot(out, w_ref[...]).astype(o_ref.dtype)


def _readout(yf, yb, bonus, gd, lnw, lnb, g2, w_out):
    b, _, t, _ = yf.shape
    tt = 256
    d = D_ATT
    pair_spec = pl.BlockSpec((1, N_PAIRS, tt, LANES), lambda i, j: (i, 0, j, 0))
    return pl.pallas_call(
        _readout_kernel,
        grid=(b, t // tt),
        in_specs=[pair_spec, pair_spec, pair_spec,
                  pl.BlockSpec((1, tt, D_GATE_LORA), lambda i, j: (i, j, 0)),
                  pl.BlockSpec((1, d), lambda i, j: (0, 0)),
                  pl.BlockSpec((1, d), lambda i, j: (0, 0)),
                  pl.BlockSpec((D_GATE_LORA, d), lambda i, j: (0, 0)),
                  pl.BlockSpec((d, D_MODEL), lambda i, j: (0, 0))],
        out_specs=pl.BlockSpec((1, tt, D_MODEL), lambda i, j: (i, j, 0)),
        out_shape=jax.ShapeDtypeStruct((b, t, D_MODEL), BF16),
        compiler_params=_cparams(("parallel", "parallel")),
        name="readout",
    )(yf, yb, bonus, gd, lnw, lnb, g2, w_out)


def _pool_kernel(u_ref, pw_ref, ps_ref, wo_ref, o_ref):
    u = u_ref[0]
    tt = u.shape[0]
    t2 = _iota((tt, tt), 0)
    s2 = _iota((tt, tt), 1)
    same = (t2 >> 6) == (s2 >> 6)
    tc = t2 & (GRID_W - 1)
    sc = s2 & (GRID_W - 1)
    col = _iota((tt, POOL_GROUP), 0) & (GRID_W - 1)
    ys = []
    for gi, w in enumerate(POOL_WINDOWS):
        ug = u[:, gi * POOL_GROUP:(gi + 1) * POOL_GROUP]
        win = (same & (sc >= tc - w // 2) & (sc < tc + (w - w // 2))).astype(BF16)
        hi, lo = _split2(ug)
        wsum = _dot(win, hi) + _dot(win, lo)
        cnt = (jnp.minimum(col + (w - w // 2), GRID_W) - jnp.maximum(col - w // 2, 0)).astype(F32)
        dlt = wsum / cnt - ug
        ys.append(_dot(dlt.astype(BF16), pw_ref[gi]))
    y1 = jnp.concatenate(ys, axis=1) * ps_ref[...]
    o_ref[0] = _dot(y1.astype(BF16), wo_ref[...]).astype(o_ref.dtype)


def _pool_branch(u, pool_w, pool_scale, w_pool_out):
    b, t, _ = u.shape
    tt = 256
    return pl.pallas_call(
        _pool_kernel,
        grid=(b, t // tt),
        in_specs=[pl.BlockSpec((1, tt, D_POOL), lambda i, j: (i, j, 0)),
                  pl.BlockSpec((4, POOL_GROUP, POOL_GROUP), lambda i, j: (0, 0, 0)),
                  pl.BlockSpec((1, D_POOL), lambda i, j: (0, 0)),
                  pl.BlockSpec((D_POOL, D_MODEL), lambda i, j: (0, 0))],
        out_specs=pl.BlockSpec((1, tt, D_MODEL), lambda i, j: (i, j, 0)),
        out_shape=jax.ShapeDtypeStruct((b, t, D_MODEL), BF16),
        compiler_params=_cparams(("parallel", "parallel")),
        name="pool_branch",
    )(u, pool_w, pool_scale, w_pool_out)


def _merge_kernel(yp_ref, yr_ref, gp_ref, gr_ref, x_ref, gt_ref, sc_ref, sh_ref, g_ref, w_ref, rw_ref,
                  x1_ref, h_ref, lg_ref):
    m = (jax.nn.sigmoid(gp_ref[0].astype(F32)) * yp_ref[0].astype(F32)
         + jax.nn.sigmoid(gr_ref[0].astype(F32)) * yr_ref[0].astype(F32))
    x1 = x_ref[0] + gt_ref[0] * _dot(m.astype(BF16), w_ref[...])
    x1_ref[0] = x1
    ms = jnp.mean(x1 * x1, axis=-1, keepdims=True)
    h = x1 * lax.rsqrt(ms + NORM_EPS) * g_ref[...]
    h = h * (1.0 + sc_ref[0]) + sh_ref[0]
    h_ref[0] = h
    hh, hl = _split2(h)
    rh, rl = _split2(rw_ref[...])
    lg_ref[0] = _dot(hh, rh) + _dot(hl, rh) + _dot(hh, rl)


def _merge(y_pool, y_rwkv, gates, x, gt1, sc2, sh2, g2, w_out, router_w_pad):
    b, t, d = x.shape
    tt = 256
    tile = pl.BlockSpec((1, tt, d), lambda i, j: (i, j, 0))
    mod = pl.BlockSpec((1, 1, d), lambda i, j: (i, 0, 0))
    return pl.pallas_call(
        _merge_kernel,
        grid=(b, t // tt),
        in_specs=[tile, tile,
                  pl.BlockSpec((1, tt, d), lambda i, j: (i, j, 0)),
                  pl.BlockSpec((1, tt, d), lambda i, j: (i, j, 1)),
                  tile, mod, mod, mod,
                  pl.BlockSpec((1, d), lambda i, j: (0, 0)),
                  pl.BlockSpec((d, d), lambda i, j: (0, 0)),
                  pl.BlockSpec((d, LANES), lambda i, j: (0, 0))],
        out_specs=[tile, tile,
                   pl.BlockSpec((1, tt, LANES), lambda i, j: (i, j, 0))],
        out_shape=[jax.ShapeDtypeStruct((b, t, d), F32),
                   jax.ShapeDtypeStruct((b, t, d), F32),
                   jax.ShapeDtypeStruct((b, t, LANES), F32)],
        compiler_params=_cparams(("parallel", "parallel")),
        name="merge",
    )(y_pool, y_rwkv, gates, gates, x, gt1, sc2, sh2, g2.reshape(1, d), w_out, router_w_pad)


def _router_kernel(lg_ref, bias_ref, e_ref, rk_ref, w_ref, cnt_ref, carry):
    tt = lg_ref.shape[0]
    shape = (tt, LANES)
    lane = _iota(shape, 1)
    valid = lane < N_EXPERTS
    grp = (lane & (N_EXPERTS - 1)) >> 3
    neg = jnp.float32(-jnp.inf)

    @pl.when(pl.program_id(0) == 0)
    def _():
        carry[...] = jnp.zeros_like(carry)

    scores = jax.nn.sigmoid(lg_ref[...])
    sel = scores + bias_ref[...]
    sel = jnp.where(valid, sel, pltpu.roll(sel, N_EXPERTS, 1))

    def group_reduce(x, op):
        for sh in (1, 2, 4):
            up = pltpu.roll(x, sh, 1)
            dn = pltpu.roll(x, LANES - sh, 1)
            x = op(x, jnp.where((lane & sh) != 0, up, dn))
        return x

    m1 = group_reduce(sel, jnp.maximum)
    first = group_reduce(jnp.where(sel == m1, lane, LANES), jnp.minimum)
    m2 = group_reduce(jnp.where(lane == first, neg, sel), jnp.maximum)
    gs = m1 + m2
    beaten = jnp.zeros(shape, I32)
    for k in range(1, N_GROUPS):
        other = pltpu.roll(gs, 8 * k, 1)
        og = (grp - k) & (N_GROUPS - 1)
        beaten = beaten + ((other > gs) | ((other == gs) & (og < grp))).astype(I32)
    cur = jnp.where((beaten < TOPK_GROUPS) & valid, sel, neg)

    picked = jnp.zeros(shape, jnp.bool_)
    e_acc = jnp.zeros(shape, I32)
    w_acc = jnp.zeros(shape, F32)
    idxs = []
    for k in range(TOP_K):
        m = jnp.max(cur, axis=1, keepdims=True)
        idx = jnp.min(jnp.where(cur == m, lane, LANES), axis=1, keepdims=True)
        oh = lane == idx
        sc = jnp.sum(jnp.where(oh, scores, 0.0), axis=1, keepdims=True)
        e_acc = jnp.where(lane == k, idx, e_acc)
        w_acc = jnp.where(lane == k, sc, w_acc)
        picked = picked | oh
        cur = jnp.where(oh, neg, cur)
        idxs.append(idx)
    wsum = jnp.sum(w_acc, axis=1, keepdims=True)
    w_ref[...] = w_acc / wsum * ROUTED_SCALE
    e_ref[...] = e_acc

    lower = (_iota((tt, tt), 1) < _iota((tt, tt), 0)).astype(BF16)
    pk = picked.astype(BF16)
    before = _dot(lower, pk) + carry[...]
    r_acc = jnp.zeros(shape, F32)
    for k in range(TOP_K):
        rk = jnp.sum(jnp.where(lane == idxs[k], before, 0.0), axis=1, keepdims=True)
        r_acc = jnp.where(lane == k, rk, r_acc)
    rk_ref[...] = r_acc.astype(I32)
    carry[...] = carry[...] + jnp.sum(picked.astype(F32), axis=0, keepdims=True)
    cnt_ref[...] = carry[...]


def _router(logits, bias_pad):
    n = logits.shape[0]
    tt = 256
    tile = pl.BlockSpec((tt, LANES), lambda i: (i, 0))
    row = pl.BlockSpec((1, LANES), lambda i: (0, 0))
    return pl.pallas_call(
        _router_kernel,
        grid=(n // tt,),
        in_specs=[tile, row],
        out_specs=[tile, tile, tile, row],
        out_shape=[jax.ShapeDtypeStruct((n, LANES), I32), jax.ShapeDtypeStruct((n, LANES), I32),
                   jax.ShapeDtypeStruct((n, LANES), F32), jax.ShapeDtypeStruct((1, LANES), F32)],
        scratch_shapes=[pltpu.VMEM((1, LANES), F32)],
        compiler_params=_cparams(("arbitrary",)),
        name="router",
    )(logits, bias_pad)


def _dispatch_kernel(e_ref, rk_ref, ps_ref, h_ref, xs_in_ref, xs_ref, sem, *, tt):
    del xs_in_ref
    base = pl.program_id(0) * tt

    def row_copy(src_row, dst_row):
        return pltpu.make_async_copy(h_ref.at[pl.ds(src_row, 1)], xs_ref.at[pl.ds(dst_row, 1)], sem)

    def start(j, c):
        dst = ps_ref[e_ref[j]] + rk_ref[j]
        row_copy(base + (j >> 3), dst).start()
        return c

    lax.fori_loop(0, tt * TOP_K, start, 0)

    def wait(j, c):
        row_copy(0, 0).wait()
        return c

    lax.fori_loop(0, tt * TOP_K, wait, 0)


def _dispatch(e_flat, rk_flat, pstarts, h_rows, xs_init):
    n = h_rows.shape[0]
    tt = min(512, n)
    smem_blk = pl.BlockSpec((tt * TOP_K,), lambda i: (i,), memory_space=pltpu.SMEM)
    return pl.pallas_call(
        functools.partial(_dispatch_kernel, tt=tt),
        grid=(n // tt,),
        in_specs=[smem_blk, smem_blk,
                  pl.BlockSpec(memory_space=pltpu.SMEM),
                  pl.BlockSpec(memory_space=pl.ANY),
                  pl.BlockSpec(memory_space=pl.ANY)],
        out_specs=pl.BlockSpec(memory_space=pl.ANY),
        out_shape=jax.ShapeDtypeStruct(xs_init.shape, xs_init.dtype),
        scratch_shapes=[pltpu.SemaphoreType.DMA(())],
        input_output_aliases={4: 0},
        compiler_params=_cparams(("arbitrary",)),
        name="dispatch",
    )(e_flat, rk_flat, pstarts, h_rows, xs_init)


def _expert_kernel(be_ref, nu_ref, x_ref, wg_ref, wu_ref, wd_ref, o_ref):
    del be_ref

    @pl.when(pl.program_id(0) < nu_ref[0])
    def _():
        xb = x_ref[...].astype(BF16)
        g = _dot(xb, wg_ref[0])
        u = _dot(xb, wu_ref[0])
        act = (g * jax.nn.sigmoid(g) * u).astype(BF16)
        o_ref[...] = _dot(act, wd_ref[0])

    @pl.when(pl.program_id(0) >= nu_ref[0])
    def _():
        o_ref[...] = jnp.zeros_like(o_ref)


def _experts(block_expert, n_used, xs, wg, wu, wd):
    cap, d = xs.shape
    bm = EXPERT_BLOCK
    de = wg.shape[2]
    grid_spec = pltpu.PrefetchScalarGridSpec(
        num_scalar_prefetch=2,
        grid=(cap // bm,),
        in_specs=[pl.BlockSpec((bm, d), lambda i, be, nu: (i, 0)),
                  pl.BlockSpec((1, D_MODEL, de), lambda i, be, nu: (be[i], 0, 0)),
                  pl.BlockSpec((1, D_MODEL, de), lambda i, be, nu: (be[i], 0, 0)),
                  pl.BlockSpec((1, de, D_MODEL), lambda i, be, nu: (be[i], 0, 0))],
        out_specs=pl.BlockSpec((bm, d), lambda i, be, nu: (i, 0)),
    )
    return pl.pallas_call(
        _expert_kernel,
        grid_spec=grid_spec,
        out_shape=jax.ShapeDtypeStruct((cap, d), F32),
        compiler_params=_cparams(("arbitrary",)),
        name="experts",
    )(block_expert, n_used, xs, wg, wu, wd)


def _combine_kernel(e_ref, rk_ref, ps_ref, ys_ref, w_ref, sh_ref, x_ref, gt_ref, g_ref, o_ref, buf, sem, *, tt):
    def row_copy(src_row, k, t):
        return pltpu.make_async_copy(ys_ref.at[pl.ds(src_row, 1)], buf.at[k, pl.ds(t, 1)], sem)

    def start(j, c):
        src = ps_ref[e_ref[j]] + rk_ref[j]
        row_copy(src, j & (TOP_K - 1), j >> 3).start()
        return c

    lax.fori_loop(0, tt * TOP_K, start, 0)

    def wait(j, c):
        row_copy(0, 0, 0).wait()
        return c

    lax.fori_loop(0, tt * TOP_K, wait, 0)

    w = w_ref[0]
    moe = sh_ref[0]
    for k in range(TOP_K):
        moe = moe + w[:, k:k + 1] * buf[k]
    x2 = x_ref[0] + gt_ref[0] * moe
    ms = jnp.mean(x2 * x2, axis=-1, keepdims=True)
    o_ref[0] = x2 * lax.rsqrt(ms + NORM_EPS) * g_ref[...]


def _combine(e_flat, rk_flat, pstarts, ys, wsel, shared, x1, gt2, final_g):
    b, t, d = x1.shape
    tt = 128
    nt = t // tt
    smem_blk = pl.BlockSpec((tt * TOP_K,), lambda i, j: (i * nt + j,), memory_space=pltpu.SMEM)
    return pl.pallas_call(
        functools.partial(_combine_kernel, tt=tt),
        grid=(b, nt),
        in_specs=[smem_blk, smem_blk,
                  pl.BlockSpec(memory_space=pltpu.SMEM),
                  pl.BlockSpec(memory_space=pl.ANY),
                  pl.BlockSpec((1, tt, LANES), lambda i, j: (i, j, 0)),
                  pl.BlockSpec((1, tt, d), lambda i, j: (i, j, 0)),
                  pl.BlockSpec((1, tt, d), lambda i, j: (i, j, 0)),
                  pl.BlockSpec((1, 1, d), lambda i, j: (i, 0, 0)),
                  pl.BlockSpec((1, d), lambda i, j: (0, 0))],
        out_specs=pl.BlockSpec((1, tt, d), lambda i, j: (i, j, 0)),
        out_shape=jax.ShapeDtypeStruct((b, t, d), F32),
        scratch_shapes=[pltpu.VMEM((TOP_K, tt, d), F32), pltpu.SemaphoreType.DMA(())],
        compiler_params=_cparams(("arbitrary", "arbitrary")),
        name="combine",
    )(e_flat, rk_flat, pstarts, ys, wsel, shared, x1, gt2, final_g.reshape(1, d))


def _pad_rows(w, rows):
    return jnp.zeros((rows, w.shape[1]), w.dtype).at[:w.shape[0]].set(w)


def _lora_pad_cols(w):
    out = jnp.zeros(w.shape[:-1] + (D_LORA_PAD,), w.dtype)
    for i in range(4):
        out = out.at[..., i * LANES:i * LANES + D_LORA].set(w[..., i * D_LORA:(i + 1) * D_LORA])
    return out.at[..., 4 * LANES:].set(w[..., 4 * D_LORA:])


def kernel(x, c, ctx, c_ctx, norm1_g, norm2_g, ada_w, ada_b, w_in, shift_mu, pool_w, pool_scale, w_pool_out, decay_w0, decay_w2, iclr_a0, iclr_a2, gate_g2, k_k, k_a, r_k, lnx_w, lnx_b, w_rwkv_out, w_out, router_w, router_bias, exp_w_gate, exp_w_up, exp_w_down, shared_w_gate, shared_w_up, shared_w_down, final_g):
    B, T, D = x.shape
    TC = ctx.shape[1]
    n = B * T
    l = 0

    w_in_l = w_in[l]
    w_u = w_in_l[:, :D_POOL].astype(BF16)
    w_rkv = w_in_l[:, D_POOL:D_POOL + 3 * D_ATT].astype(BF16)
    w_lora = _lora_pad_cols(w_in_l[:, D_POOL + 3 * D_ATT:D_POOL + 3 * D_ATT + 4 * D_LORA + D_GATE_LORA]).astype(BF16)
    w_gates = w_in_l[:, D_POOL + 3 * D_ATT + 4 * D_LORA + D_GATE_LORA:].astype(BF16)
    mu = shift_mu[l]
    row = lambda a: a.reshape(1, -1)
    pw = {
        "mu_rkv": row(mu[:3 * D_ATT]),
        "mu_lora": row(_lora_pad_cols(mu[3 * D_ATT:])),
        "w2f": _pad_rows(decay_w2[l, 0], LANES).astype(BF16),
        "w2b": _pad_rows(decay_w2[l, 1], LANES).astype(BF16),
        "a2f": _pad_rows(iclr_a2[l, 0], LANES).astype(BF16),
        "a2b": _pad_rows(iclr_a2[l, 1], LANES).astype(BF16),
        "w0f": row(decay_w0[l, 0]), "w0b": row(decay_w0[l, 1]),
        "a0f": row(iclr_a0[l, 0]), "a0b": row(iclr_a0[l, 1]),
        "k_k": row(k_k[l]), "k_a": row(k_a[l]), "r_k": row(r_k[l]),
    }

    cstack = jnp.zeros((8, D), F32).at[:B].set(c).at[B].set(c_ctx)
    mod = _ada_mod(cstack, ada_w[l], ada_b[l])
    sh1, sc1, gt1, sh2, sc2, gt2 = [mod[:B, i * D:(i + 1) * D].reshape(B, 1, D) for i in range(6)]
    sh1c = jnp.broadcast_to(mod[B, 0:D].reshape(1, 1, D), (B, 1, D))
    sc1c = jnp.broadcast_to(mod[B, D:2 * D].reshape(1, 1, D), (B, 1, D))

    hc = _norm_mod(ctx, norm1_g[l], sc1c, sh1c, TC).reshape(B * TC, D)
    rkv_c = _matmul(hc, w_rkv, F32, 512, 512).reshape(B, TC, 3 * D_ATT)
    lora_c = _matmul(hc, w_lora, F32, 512, D_LORA_PAD).reshape(B, TC, D_LORA_PAD)
    pc = _prepare(rkv_c, lora_c, pw, grid_mode=False)
    rc, vc, kkc, lwfc, lwbc, kfc, kbc, bfc, bbc = [a.reshape(B * N_PAIRS, TC, LANES) for a in pc[:9]]
    s0 = jnp.zeros((B * N_PAIRS, LANES, LANES), F32)
    _, st_f = _scan(rc, lwfc, kfc, vc, kkc, bfc, s0, reverse=False, emit=False)
    _, st_b = _scan(rc, lwbc, kbc, vc, kkc, bbc, s0, reverse=True, emit=False)

    h = _norm_mod(x, norm1_g[l], sc1, sh1, 512).reshape(n, D)
    u = _matmul(h, w_u, F32, 512, 512).reshape(B, T, D_POOL)
    rkv = _matmul(h, w_rkv, F32, 512, 512).reshape(B, T, 3 * D_ATT)
    lora = _matmul(h, w_lora, F32, 512, D_LORA_PAD).reshape(B, T, D_LORA_PAD)
    gates = _matmul(h, w_gates, BF16, 512, 512).reshape(B, T, 2 * D)
    pp = _prepare(rkv, lora, pw, grid_mode=True)
    r_, v_, kk_, lwf, lwb, kf, kb, bf_, bb_ = [a.reshape(B * N_PAIRS, T, LANES) for a in pp[:9]]
    bonus, gd = pp[9], pp[10]
    y_f, _ = _scan(r_, lwf, kf, v_, kk_, bf_, st_f, reverse=False, emit=True)
    y_b, _ = _scan(r_, lwb, kb, v_, kk_, bb_, st_b, reverse=True, emit=True)
    y_rwkv = _readout(y_f.reshape(B, N_PAIRS, T, LANES), y_b.reshape(B, N_PAIRS, T, LANES), bonus, gd,
                      row(lnx_w[l]), row(lnx_b[l]), gate_g2[l].astype(BF16), w_rwkv_out[l].astype(BF16))
    y_pool = _pool_branch(u, pool_w[l].astype(BF16), row(pool_scale[l]), w_pool_out[l].astype(BF16))

    router_w_pad = jnp.zeros((D, LANES), F32).at[:, :N_EXPERTS].set(router_w[l])
    x1, h2, logits = _merge(y_pool, y_rwkv, gates, x, gt1, sc2, sh2, norm2_g[l],
                                w_out[l].astype(BF16), router_w_pad)

    bias_pad = jnp.zeros((1, LANES), F32).at[0, :N_EXPERTS].set(router_bias[l])
    e_idx, e_rank, wsel, counts = _router(logits.reshape(n, LANES), bias_pad)
    e_flat = e_idx[:, :TOP_K].reshape(-1)
    rk_flat = e_rank[:, :TOP_K].reshape(-1)
    bm = EXPERT_BLOCK
    cnt = counts[0, :N_EXPERTS].astype(I32)
    padded = (cnt + bm - 1) // bm * bm
    pend = jnp.cumsum(padded)
    pstarts = (pend - padded).astype(I32)
    cap = n * TOP_K + N_EXPERTS * bm
    n_blocks = cap // bm
    block_start = jnp.arange(n_blocks, dtype=I32) * bm
    block_expert = jnp.minimum(jnp.sum(block_start[:, None] >= pend[None, :], axis=-1), N_EXPERTS - 1).astype(I32)
    n_used = (pend[-1] // bm).astype(I32).reshape(1)

    h2 = h2.reshape(n, D)
    xs = _dispatch(e_flat, rk_flat, pstarts, h2, jnp.zeros((cap, D), F32))
    ys = _experts(block_expert, n_used, xs, exp_w_gate[l].astype(BF16), exp_w_up[l].astype(BF16),
                  exp_w_down[l].astype(BF16))
    shared = _experts(jnp.zeros((n // bm,), I32), jnp.full((1,), n // bm, I32), h2,
                      shared_w_gate[l].astype(BF16)[None], shared_w_up[l].astype(BF16)[None],
                      shared_w_down[l].astype(BF16)[None])
    return _combine(e_flat, rk_flat, pstarts, ys, wsel.reshape(B, T, LANES),
                    shared.reshape(B, T, D), x1, gt2, final_g)
```

```python
import functools

import jax
import jax.numpy as jnp
from jax import lax
from jax.experimental import pallas as pl
from jax.experimental.pallas import tpu as pltpu

F32 = jnp.float32
BF16 = jnp.bfloat16
I32 = jnp.int32
U32 = jnp.uint32

D_MODEL = 2048
GRID_W = 64
POOL_WINDOWS = (2, 4, 8, 16)
POOL_GROUP = 256
D_POOL = 1024
HEAD = 64
N_HEADS = 32
N_PAIRS = N_HEADS // 2
D_ATT = 2048
D_LORA = 96
D_GATE_LORA = 256
D_LORA_PAD = 768
GN_EPS = 64e-5
NORM_EPS = 1e-6
N_EXPERTS = 64
TOP_K = 8
N_GROUPS = 8
TOPK_GROUPS = 4
D_EXPERT = 512
ROUTED_SCALE = 2.5
EXP_M05 = 0.6065306597126334

LANES = 128
CHUNK = 64
EXPERT_BLOCK = 512
VMEM_LIMIT = 56 * 1024 * 1024


def _cparams(sem):
    return pltpu.CompilerParams(dimension_semantics=sem, vmem_limit_bytes=VMEM_LIMIT)


def _dot(a, b):
    return jnp.dot(a, b, preferred_element_type=F32)


def _dot_nt(a, b):
    return lax.dot_general(a, b, (((1,), (1,)), ((), ())), preferred_element_type=F32)


def _bmm(a, b):
    return lax.dot_general(a, b, (((2,), (1,)), ((0,), (0,))), preferred_element_type=F32)


def _bmm_nt(a, b):
    return lax.dot_general(a, b, (((2,), (2,)), ((0,), (0,))), preferred_element_type=F32)


def _bmm_tn(a, b):
    return lax.dot_general(a, b, (((1,), (1,)), ((0,), (0,))), preferred_element_type=F32)


def _split2(x):
    hi = x.astype(BF16)
    lo = (x - hi.astype(F32)).astype(BF16)
    return hi, lo


def _iota(shape, axis):
    return lax.broadcasted_iota(I32, shape, axis)


def _ada_kernel(c_ref, w_ref, b_ref, o_ref):
    c = c_ref[...]
    a = c * jax.nn.sigmoid(c)
    o_ref[...] = _dot(a.astype(BF16), w_ref[...].astype(BF16)) + b_ref[...]


def _ada_mod(cstack, ada_w, ada_b):
    m, d = cstack.shape
    n = ada_w.shape[1]
    tn = 1024
    return pl.pallas_call(
        _ada_kernel,
        grid=(n // tn,),
        in_specs=[pl.BlockSpec((m, d), lambda j: (0, 0)),
                  pl.BlockSpec((d, tn), lambda j: (0, j)),
                  pl.BlockSpec((1, tn), lambda j: (0, j))],
        out_specs=pl.BlockSpec((m, tn), lambda j: (0, j)),
        out_shape=jax.ShapeDtypeStruct((m, n), F32),
        compiler_params=_cparams(("parallel",)),
        name="ada_mod",
    )(cstack, ada_w, ada_b.reshape(1, n))


def _norm_mod_kernel(x_ref, g_ref, sc_ref, sh_ref, o_ref):
    x = x_ref[0]
    ms = jnp.mean(x * x, axis=-1, keepdims=True)
    y = x * lax.rsqrt(ms + NORM_EPS) * g_ref[...]
    o_ref[0] = (y * (1.0 + sc_ref[0]) + sh_ref[0]).astype(o_ref.dtype)


def _norm_mod(x, g, sc, sh, tt):
    b, t, d = x.shape
    return pl.pallas_call(
        _norm_mod_kernel,
        grid=(b, t // tt),
        in_specs=[pl.BlockSpec((1, tt, d), lambda i, j: (i, j, 0)),
                  pl.BlockSpec((1, d), lambda i, j: (0, 0)),
                  pl.BlockSpec((1, 1, d), lambda i, j: (i, 0, 0)),
                  pl.BlockSpec((1, 1, d), lambda i, j: (i, 0, 0))],
        out_specs=pl.BlockSpec((1, tt, d), lambda i, j: (i, j, 0)),
        out_shape=jax.ShapeDtypeStruct((b, t, d), BF16),
        compiler_params=_cparams(("parallel", "parallel")),
        name="norm_mod",
    )(x, g.reshape(1, d), sc, sh)


def _mm_kernel(a_ref, b_ref, o_ref):
    o_ref[...] = _dot(a_ref[...], b_ref[...]).astype(o_ref.dtype)


def _matmul(a, b, out_dtype, tm, tn):
    m, k = a.shape
    n = b.shape[1]
    tm = min(tm, m)
    return pl.pallas_call(
        _mm_kernel,
        grid=(m // tm, n // tn),
        in_specs=[pl.BlockSpec((tm, k), lambda i, j: (i, 0)),
                  pl.BlockSpec((k, tn), lambda i, j: (0, j))],
        out_specs=pl.BlockSpec((tm, tn), lambda i, j: (i, j)),
        out_shape=jax.ShapeDtypeStruct((m, n), out_dtype),
        compiler_params=_cparams(("parallel", "parallel")),
        name="matmul",
    )(a, b)


def _shift_grid(x, prev, nxt, first, last):
    tt, c = x.shape
    col = _iota((tt, c), 0) & (GRID_W - 1)
    m = _iota((tt, c), 1) & 3
    left = jnp.where(col == 0, 0.0, pltpu.roll(x, 1, 0))
    right = jnp.where(col == GRID_W - 1, 0.0, pltpu.roll(x, tt - 1, 0))
    prev = jnp.where(first, 0.0, prev)
    nxt = jnp.where(last, 0.0, nxt)
    if tt > GRID_W:
        up = jnp.concatenate([prev, x[:tt - GRID_W]], axis=0)
        down = jnp.concatenate([x[GRID_W:], nxt], axis=0)
    else:
        up, down = prev, nxt
    return jnp.where(m == 0, left, jnp.where(m == 1, right, jnp.where(m == 2, up, down)))


def _shift_seq(x, prev8, next8, first, last):
    t, c = x.shape
    row = _iota((t, c), 0)
    odd = (_iota((t, c), 1) & 1) == 1
    before = jnp.where(first, 0.0, prev8[7:8])
    after = jnp.where(last, 0.0, next8[0:1])
    prev = jnp.where(row == 0, before, pltpu.roll(x, 1, 0))
    nxt = jnp.where(row == t - 1, after, pltpu.roll(x, t - 1, 0))
    return jnp.where(odd, nxt, prev)


def _head_sum(x):
    w = 2 * LANES
    ones = (_iota((w, w), 0) >> 6 == _iota((w, w), 1) >> 6).astype(BF16)
    outs = []
    for c in range(x.shape[1] // w):
        hi, lo = _split2(x[:, c * w:(c + 1) * w])
        outs.append(_dot(hi, ones) + _dot(lo, ones))
    return jnp.concatenate(outs, axis=1)


def _prepare_kernel(*refs, grid_mode):
    (r_ref, rp_ref, rn_ref, k_ref, kp_ref, kn_ref, v_ref, vp_ref, vn_ref,
     l_ref, lp_ref, ln_ref) = refs[:12]
    rest = refs[12:]
    (mur_ref, muk_ref, muv_ref, mul_ref, w2f_ref, w2b_ref, a2f_ref, a2b_ref,
     w0f_ref, w0b_ref, a0f_ref, a0b_ref, kk_ref, ka_ref, rk_ref,
     or_ref, ov_ref, okk_ref, olwf_ref, olwb_ref, okf_ref, okb_ref, obf_ref, obb_ref,
     obon_ref, ogd_ref) = rest

    first = pl.program_id(1) == 0
    last = pl.program_id(1) == pl.num_programs(1) - 1
    shift = _shift_grid if grid_mode else _shift_seq

    def mix(x_ref, p_ref, n_ref, mu_ref):
        x = x_ref[0]
        return x + (shift(x, p_ref[0], n_ref[0], first, last) - x) * mu_ref[...]

    r = mix(r_ref, rp_ref, rn_ref, mur_ref)
    k = mix(k_ref, kp_ref, kn_ref, muk_ref)
    v = mix(v_ref, vp_ref, vn_ref, muv_ref)
    lo = mix(l_ref, lp_ref, ln_ref, mul_ref)

    th = jnp.tanh(lo[:, :2 * LANES]).astype(BF16)
    zf = w0f_ref[...] + _dot(th[:, :LANES], w2f_ref[...])
    zb = w0b_ref[...] + _dot(th[:, LANES:], w2b_ref[...])
    lwf = -EXP_M05 * jax.nn.sigmoid(zf)
    lwb = -EXP_M05 * jax.nn.sigmoid(zb)
    ad = lo[:, 2 * LANES:4 * LANES].astype(BF16)
    af = jax.nn.sigmoid(a0f_ref[...] + _dot(ad[:, :LANES], a2f_ref[...]))
    ab = jax.nn.sigmoid(a0b_ref[...] + _dot(ad[:, LANES:], a2b_ref[...]))
    ogd_ref[0] = lo[:, 4 * LANES:]

    kk = k * kk_ref[...]
    kk = kk * lax.rsqrt(_head_sum(kk * kk) + 1e-12)
    ka = ka_ref[...]
    kf = k * (1.0 + (af - 1.0) * ka)
    kb = k * (1.0 + (ab - 1.0) * ka)
    bonus = _head_sum(r * (kf + kb) * rk_ref[...]) * v

    outs = ((or_ref, r), (ov_ref, v), (okk_ref, kk), (olwf_ref, lwf), (olwb_ref, lwb),
            (okf_ref, kf), (okb_ref, kb), (obf_ref, kk * af), (obb_ref, kk * ab), (obon_ref, bonus))
    for o_ref, val in outs:
        for p in range(N_PAIRS):
            o_ref[0, p] = val[:, p * LANES:(p + 1) * LANES]


def _prepare(rkv, lora, pw, grid_mode):
    b, t, _ = rkv.shape
    d = D_ATT
    tt = 2 * GRID_W
    halo = GRID_W if grid_mode else 8
    hpt = tt // halo
    nhalo = t // halo
    grid = (b, t // tt)

    def tile_specs(w, c):
        return [pl.BlockSpec((1, tt, w), lambda i, j: (i, j, c)),
                pl.BlockSpec((1, halo, w), lambda i, j: (i, jnp.maximum(j * hpt - 1, 0), c)),
                pl.BlockSpec((1, halo, w), lambda i, j: (i, jnp.minimum((j + 1) * hpt, nhalo - 1), c))]

    in_specs = []
    args = []
    for c in range(3):
        in_specs += tile_specs(d, c)
        args += [rkv, rkv, rkv]
    in_specs += tile_specs(D_LORA_PAD, 0)
    args += [lora, lora, lora]

    def vec(c=0, w=d):
        return pl.BlockSpec((1, w), lambda i, j, c=c: (0, c))

    def full(shape):
        return pl.BlockSpec(shape, lambda i, j: (0,) * len(shape))

    in_specs += [vec(0), vec(1), vec(2), vec(0, D_LORA_PAD)]
    args += [pw["mu_rkv"], pw["mu_rkv"], pw["mu_rkv"], pw["mu_lora"]]
    in_specs += [full((LANES, d))] * 4
    args += [pw["w2f"], pw["w2b"], pw["a2f"], pw["a2b"]]
    in_specs += [vec()] * 7
    args += [pw["w0f"], pw["w0b"], pw["a0f"], pw["a0b"], pw["k_k"], pw["k_a"], pw["r_k"]]

    pair = jax.ShapeDtypeStruct((b, N_PAIRS, t, LANES), F32)
    pair_spec = pl.BlockSpec((1, N_PAIRS, tt, LANES), lambda i, j: (i, 0, j, 0))
    out_shape = [pair] * 10 + [jax.ShapeDtypeStruct((b, t, D_GATE_LORA), F32)]
    out_specs = [pair_spec] * 10 + [pl.BlockSpec((1, tt, D_GATE_LORA), lambda i, j: (i, j, 0))]
    return pl.pallas_call(
        functools.partial(_prepare_kernel, grid_mode=grid_mode),
        grid=grid, in_specs=in_specs, out_specs=out_specs, out_shape=out_shape,
        compiler_params=_cparams(("parallel", "parallel")),
        name="prepare_grid" if grid_mode else "prepare_seq",
    )(*args)


def _scan_kernel(*refs, reverse, pairs, tb, emit):
    r_ref, lw_ref, k_ref, v_ref, kk_ref, b_ref, s0_ref = refs[:7]
    if emit:
        y_ref, st_ref, s_scr = refs[7:]
    else:
        st_ref, s_scr = refs[7:]
    L = CHUNK
    n_chunks = tb // L

    @pl.when(pl.program_id(1) == 0)
    def _():
        s_scr[...] = s0_ref[...]

    ti = _iota((L, L), 0)
    si = _iota((L, L), 1)
    tri = ((si >= ti) if reverse else (si <= ti)).astype(BF16)
    t2 = _iota((2 * L, 2 * L), 0)
    s2 = _iota((2 * L, 2 * L), 1)
    same = (t2 >> 6) == (s2 >> 6)
    tl = t2 & (L - 1)
    sl = s2 & (L - 1)
    strict = same & ((sl > tl) if reverse else (sl < tl))
    incl = same & ((sl >= tl) if reverse else (sl <= tl))
    eye = (t2 == s2).astype(F32)
    head_a = _iota((pairs, L, LANES), 2) < HEAD
    tri_b = jnp.broadcast_to(tri, (pairs, L, L))

    def stack(x):
        return jnp.concatenate([jnp.where(head_a, x, 0.0), jnp.where(head_a, 0.0, x)], axis=1).astype(BF16)

    def chunk(ci, carry):
        cc = (n_chunks - 1 - ci) if reverse else ci
        rows = pl.ds(pl.multiple_of(cc * L, L), L)
        lw = lw_ref[:, rows, :]
        hi = lw.astype(BF16)
        r1 = lw - hi.astype(F32)
        mid = r1.astype(BF16)
        lo = (r1 - mid.astype(F32)).astype(BF16)
        cum = _bmm(tri_b, hi) + _bmm(tri_b, mid) + _bmm(tri_b, lo)
        w_inc = jnp.exp(cum)
        w_exc = jnp.exp(cum - lw)
        w_inv = jnp.exp(-cum)
        a_s = stack(w_exc * kk_ref[:, rows, :])
        b_s = stack(b_ref[:, rows, :] * w_inv)
        k_s = stack(k_ref[:, rows, :] * w_inv)
        v_s = stack(v_ref[:, rows, :])
        m1 = jnp.where(strict, _bmm_nt(a_s, b_s), 0.0)
        m2 = jnp.where(strict, _bmm_nt(a_s, k_s), 0.0)
        xp = -m1
        tinv = eye + xp
        for _ in range(5):
            xb = xp.astype(BF16)
            xp = _bmm(xb, xb)
            tinv = tinv + _bmm(tinv.astype(BF16), xp.astype(BF16))
        s = s_scr[...]
        sb = s.astype(BF16)
        g = _bmm_nt(a_s, sb) + _bmm(m2.astype(BF16), v_s)
        u_s = (-_bmm(tinv.astype(BF16), g.astype(BF16))).astype(BF16)
        if emit:
            r_s = stack(w_inc * r_ref[:, rows, :])
            n1 = jnp.where(incl, _bmm_nt(r_s, b_s), 0.0)
            n2 = jnp.where(incl, _bmm_nt(r_s, k_s), 0.0)
            y2 = _bmm_nt(r_s, sb) + _bmm(n1.astype(BF16), u_s) + _bmm(n2.astype(BF16), v_s)
            y_ref[:, rows, :] = y2[:, :L] + y2[:, L:]
        w_last = w_inc[:, 0:1] if reverse else w_inc[:, L - 1:L]
        s_scr[...] = (s + _bmm_tn(u_s, b_s) + _bmm_tn(v_s, k_s)) * w_last
        return carry

    lax.fori_loop(0, n_chunks, chunk, 0)

    @pl.when(pl.program_id(1) == pl.num_programs(1) - 1)
    def _():
        st_ref[...] = s_scr[...]


def _scan(r, lw, k, v, kk, b, s0, reverse, emit):
    bp, t, _ = r.shape
    pairs = 8
    tb = min(t, 256)
    nb = t // tb

    def tmap(g, c):
        return (g, (nb - 1 - c) if reverse else c, 0)

    data = pl.BlockSpec((pairs, tb, LANES), tmap)
    state = pl.BlockSpec((pairs, LANES, LANES), lambda g, c: (g, 0, 0))
    out_shape = [jax.ShapeDtypeStruct((bp, LANES, LANES), F32)]
    out_specs = [state]
    if emit:
        out_shape = [jax.ShapeDtypeStruct((bp, t, LANES), F32)] + out_shape
        out_specs = [data] + out_specs
    res = pl.pallas_call(
        functools.partial(_scan_kernel, reverse=reverse, pairs=pairs, tb=tb, emit=emit),
        grid=(bp // pairs, nb),
        in_specs=[data] * 6 + [state],
        out_specs=out_specs, out_shape=out_shape,
        scratch_shapes=[pltpu.VMEM((pairs, LANES, LANES), F32)],
        compiler_params=_cparams(("parallel", "arbitrary")),
        name="scan_" + ("bwd" if reverse else "fwd") + ("_emit" if emit else "_state"),
    )(r, lw, k, v, kk, b, s0)
    return (res[0], res[1]) if emit else (None, res[0])


def _readout_kernel(yf_ref, yb_ref, bon_ref, gd_ref, lnw_ref, lnb_ref, g2_ref, w_ref, o_ref):
    y = jnp.concatenate([yf_ref[0, p] + yb_ref[0, p] for p in range(N_PAIRS)], axis=1)
    bonus = jnp.concatenate([bon_ref[0, p] for p in range(N_PAIRS)], axis=1)
    mean = _head_sum(y) * (1.0 / HEAD)
    dlt = y - mean
    var = _head_sum(dlt * dlt) * (1.0 / HEAD)
    yn = dlt * lax.rsqrt(var + GN_EPS) * lnw_ref[...] + lnb_ref[...]
    gate = _dot(jax.nn.sigmoid(gd_ref[0]).astype(BF16), g2_ref[...])
    out = ((yn + bonus) * gate).astype(BF16)
    o_ref[0] = _dot(out, w_ref[...]).astype(o_ref.dtype)


def _readout(yf, yb, bonus, gd, lnw, lnb, g2, w_out):
    b, _, t, _ = yf.shape
    tt = 256
    d = D_ATT
    pair_spec = pl.BlockSpec((1, N_PAIRS, tt, LANES), lambda i, j: (i, 0, j, 0))
    return pl.pallas_call(
        _readout_kernel,
        grid=(b, t // tt),
        in_specs=[pair_spec, pair_spec, pair_spec,
                  pl.BlockSpec((1, tt, D_GATE_LORA), lambda i, j: (i, j, 0)),
                  pl.BlockSpec((1, d), lambda i, j: (0, 0)),
                  pl.BlockSpec((1, d), lambda i, j: (0, 0)),
                  pl.BlockSpec((D_GATE_LORA, d), lambda i, j: (0, 0)),
                  pl.BlockSpec((d, D_MODEL), lambda i, j: (0, 0))],
        out_specs=pl.BlockSpec((1, tt, D_MODEL), lambda i, j: (i, j, 0)),
        out_shape=jax.ShapeDtypeStruct((b, t, D_MODEL), BF16),
        compiler_params=_cparams(("parallel", "parallel")),
        name="readout",
    )(yf, yb, bonus, gd, lnw, lnb, g2, w_out)


def _pool_kernel(u_ref, pw_ref, ps_ref, wo_ref, o_ref):
    u = u_ref[0]
    tt = u.shape[0]
    t2 = _iota((tt, tt), 0)
    s2 = _iota((tt, tt), 1)
    same = (t2 >> 6) == (s2 >> 6)
    tc = t2 & (GRID_W - 1)
    sc = s2 & (GRID_W - 1)
    col = _iota((tt, POOL_GROUP), 0) & (GRID_W - 1)
    ys = []
    for gi, w in enumerate(POOL_WINDOWS):
        ug = u[:, gi * POOL_GROUP:(gi + 1) * POOL_GROUP]
        win = (same & (sc >= tc - w // 2) & (sc < tc + (w - w // 2))).astype(BF16)
        hi, lo = _split2(ug)
        wsum = _dot(win, hi) + _dot(win, lo)
        cnt = (jnp.minimum(col + (w - w // 2), GRID_W) - jnp.maximum(col - w // 2, 0)).astype(F32)
        dlt = wsum / cnt - ug
        ys.append(_dot(dlt.astype(BF16), pw_ref[gi]))
    y1 = jnp.concatenate(ys, axis=1) * ps_ref[...]
    o_ref[0] = _dot(y1.astype(BF16), wo_ref[...]).astype(o_ref.dtype)


def _pool_branch(u, pool_w, pool_scale, w_pool_out):
    b, t, _ = u.shape
    tt = 256
    return pl.pallas_call(
        _pool_kernel,
        grid=(b, t // tt),
        in_specs=[pl.BlockSpec((1, tt, D_POOL), lambda i, j: (i, j, 0)),
                  pl.BlockSpec((4, POOL_GROUP, POOL_GROUP), lambda i, j: (0, 0, 0)),
                  pl.BlockSpec((1, D_POOL), lambda i, j: (0, 0)),
                  pl.BlockSpec((D_POOL, D_MODEL), lambda i, j: (0, 0))],
        out_specs=pl.BlockSpec((1, tt, D_MODEL), lambda i, j: (i, j, 0)),
        out_shape=jax.ShapeDtypeStruct((b, t, D_MODEL), BF16),
        compiler_params=_cparams(("parallel", "parallel")),
        name="pool_branch",
    )(u, pool_w, pool_scale, w_pool_out)


def _merge_kernel(yp_ref, yr_ref, gp_ref, gr_ref, x_ref, gt_ref, sc_ref, sh_ref, g_ref, w_ref, rw_ref,
                  x1_ref, h_ref, lg_ref):
    m = (jax.nn.sigmoid(gp_ref[0].astype(F32)) * yp_ref[0].astype(F32)
         + jax.nn.sigmoid(gr_ref[0].astype(F32)) * yr_ref[0].astype(F32))
    x1 = x_ref[0] + gt_ref[0] * _dot(m.astype(BF16), w_ref[...])
    x1_ref[0] = x1
    ms = jnp.mean(x1 * x1, axis=-1, keepdims=True)
    h = x1 * lax.rsqrt(ms + NORM_EPS) * g_ref[...]
    h = h * (1.0 + sc_ref[0]) + sh_ref[0]
    h_ref[0] = h
    hh, hl = _split2(h)
    rh, rl = _split2(rw_ref[...])
    lg_ref[0] = _dot(hh, rh) + _dot(hl, rh) + _dot(hh, rl)


def _merge(y_pool, y_rwkv, gates, x, gt1, sc2, sh2, g2, w_out, router_w_pad):
    b, t, d = x.shape
    tt = 256
    tile = pl.BlockSpec((1, tt, d), lambda i, j: (i, j, 0))
    mod = pl.BlockSpec((1, 1, d), lambda i, j: (i, 0, 0))
    return pl.pallas_call(
        _merge_kernel,
        grid=(b, t // tt),
        in_specs=[tile, tile,
                  pl.BlockSpec((1, tt, d), lambda i, j: (i, j, 0)),
                  pl.BlockSpec((1, tt, d), lambda i, j: (i, j, 1)),
                  tile, mod, mod, mod,
                  pl.BlockSpec((1, d), lambda i, j: (0, 0)),
                  pl.BlockSpec((d, d), lambda i, j: (0, 0)),
                  pl.BlockSpec((d, LANES), lambda i, j: (0, 0))],
        out_specs=[tile, tile,
                   pl.BlockSpec((1, tt, LANES), lambda i, j: (i, j, 0))],
        out_shape=[jax.ShapeDtypeStruct((b, t, d), F32),
                   jax.ShapeDtypeStruct((b, t, d), F32),
                   jax.ShapeDtypeStruct((b, t, LANES), F32)],
        compiler_params=_cparams(("parallel", "parallel")),
        name="merge",
    )(y_pool, y_rwkv, gates, gates, x, gt1, sc2, sh2, g2.reshape(1, d), w_out, router_w_pad)


def _router_kernel(lg_ref, bias_ref, e_ref, rk_ref, w_ref, cnt_ref, carry):
    tt = lg_ref.shape[0]
    shape = (tt, LANES)
    lane = _iota(shape, 1)
    valid = lane < N_EXPERTS
    grp = (lane & (N_EXPERTS - 1)) >> 3
    neg = jnp.float32(-jnp.inf)

    @pl.when(pl.program_id(0) == 0)
    def _():
        carry[...] = jnp.zeros_like(carry)

    scores = jax.nn.sigmoid(lg_ref[...])
    sel = scores + bias_ref[...]
    sel = jnp.where(valid, sel, pltpu.roll(sel, N_EXPERTS, 1))

    def group_reduce(x, op):
        for sh in (1, 2, 4):
            up = pltpu.roll(x, sh, 1)
            dn = pltpu.roll(x, LANES - sh, 1)
            x = op(x, jnp.where((lane & sh) != 0, up, dn))
        return x

    m1 = group_reduce(sel, jnp.maximum)
    first = group_reduce(jnp.where(sel == m1, lane, LANES), jnp.minimum)
    m2 = group_reduce(jnp.where(lane == first, neg, sel), jnp.maximum)
    gs = m1 + m2
    beaten = jnp.zeros(shape, I32)
    for k in range(1, N_GROUPS):
        other = pltpu.roll(gs, 8 * k, 1)
        og = (grp - k) & (N_GROUPS - 1)
        beaten = beaten + ((other > gs) | ((other == gs) & (og < grp))).astype(I32)
    cur = jnp.where((beaten < TOPK_GROUPS) & valid, sel, neg)

    picked = jnp.zeros(shape, jnp.bool_)
    e_acc = jnp.zeros(shape, I32)
    w_acc = jnp.zeros(shape, F32)
    idxs = []
    for k in range(TOP_K):
        m = jnp.max(cur, axis=1, keepdims=True)
        idx = jnp.min(jnp.where(cur == m, lane, LANES), axis=1, keepdims=True)
        oh = lane == idx
        sc = jnp.sum(jnp.where(oh, scores, 0.0), axis=1, keepdims=True)
        e_acc = jnp.where(lane == k, idx, e_acc)
        w_acc = jnp.where(lane == k, sc, w_acc)
        picked = picked | oh
        cur = jnp.where(oh, neg, cur)
        idxs.append(idx)
    wsum = jnp.sum(w_acc, axis=1, keepdims=True)
    w_ref[...] = w_acc / wsum * ROUTED_SCALE
    e_ref[...] = e_acc

    lower = (_iota((tt, tt), 1) < _iota((tt, tt), 0)).astype(BF16)
    pk = picked.astype(BF16)
    before = _dot(lower, pk) + carry[...]
    r_acc = jnp.zeros(shape, F32)
    for k in range(TOP_K):
        rk = jnp.sum(jnp.where(lane == idxs[k], before, 0.0), axis=1, keepdims=True)
        r_acc = jnp.where(lane == k, rk, r_acc)
    rk_ref[...] = r_acc.astype(I32)
    carry[...] = carry[...] + jnp.sum(picked.astype(F32), axis=0, keepdims=True)
    cnt_ref[...] = carry[...]


def _router(logits, bias_pad):
    n = logits.shape[0]
    tt = 256
    tile = pl.BlockSpec((tt, LANES), lambda i: (i, 0))
    row = pl.BlockSpec((1, LANES), lambda i: (0, 0))
    return pl.pallas_call(
        _router_kernel,
        grid=(n // tt,),
        in_specs=[tile, row],
        out_specs=[tile, tile, tile, row],
        out_shape=[jax.ShapeDtypeStruct((n, LANES), I32), jax.ShapeDtypeStruct((n, LANES), I32),
                   jax.ShapeDtypeStruct((n, LANES), F32), jax.ShapeDtypeStruct((1, LANES), F32)],
        scratch_shapes=[pltpu.VMEM((1, LANES), F32)],
        compiler_params=_cparams(("arbitrary",)),
        name="router",
    )(logits, bias_pad)


def _dispatch_kernel(e_ref, rk_ref, ps_ref, h_ref, xs_in_ref, xs_ref, sem, *, tt):
    del xs_in_ref

    def row_copy(src_row, dst_row):
        return pltpu.make_async_copy(h_ref.at[pl.ds(src_row, 1)], xs_ref.at[pl.ds(dst_row, 1)], sem)

    def start(t, c):
        for k in range(TOP_K):
            j = t * TOP_K + k
            row_copy(t, ps_ref[e_ref[j]] + rk_ref[j]).start()
        return c

    lax.fori_loop(0, tt, start, 0)

    def wait(t, c):
        for k in range(TOP_K):
            row_copy(0, 0).wait()
        return c

    lax.fori_loop(0, tt, wait, 0)


def _dispatch(e_flat, rk_flat, pstarts, h_rows, xs_init):
    n = h_rows.shape[0]
    tt = min(512, n)
    smem_blk = pl.BlockSpec((tt * TOP_K,), lambda i: (i,), memory_space=pltpu.SMEM)
    return pl.pallas_call(
        functools.partial(_dispatch_kernel, tt=tt),
        grid=(n // tt,),
        in_specs=[smem_blk, smem_blk,
                  pl.BlockSpec(memory_space=pltpu.SMEM),
                  pl.BlockSpec((tt, h_rows.shape[1]), lambda i: (i, 0)),
                  pl.BlockSpec(memory_space=pl.ANY)],
        out_specs=pl.BlockSpec(memory_space=pl.ANY),
        out_shape=jax.ShapeDtypeStruct(xs_init.shape, xs_init.dtype),
        scratch_shapes=[pltpu.SemaphoreType.DMA(())],
        input_output_aliases={4: 0},
        compiler_params=_cparams(("arbitrary",)),
        name="dispatch",
    )(e_flat, rk_flat, pstarts, h_rows, xs_init)


def _expert_kernel(be_ref, nu_ref, x_ref, wg_ref, wu_ref, wd_ref, o_ref):
    del be_ref

    @pl.when(pl.program_id(0) < nu_ref[0])
    def _():
        xb = x_ref[...].astype(BF16)
        g = _dot(xb, wg_ref[0])
        u = _dot(xb, wu_ref[0])
        act = (g * jax.nn.sigmoid(g) * u).astype(BF16)
        o_ref[...] = _dot(act, wd_ref[0])

    @pl.when(pl.program_id(0) >= nu_ref[0])
    def _():
        o_ref[...] = jnp.zeros_like(o_ref)


def _experts(block_expert, n_used, xs, wg, wu, wd):
    cap, d = xs.shape
    bm = EXPERT_BLOCK
    de = wg.shape[2]
    grid_spec = pltpu.PrefetchScalarGridSpec(
        num_scalar_prefetch=2,
        grid=(cap // bm,),
        in_specs=[pl.BlockSpec((bm, d), lambda i, be, nu: (i, 0)),
                  pl.BlockSpec((1, D_MODEL, de), lambda i, be, nu: (be[i], 0, 0)),
                  pl.BlockSpec((1, D_MODEL, de), lambda i, be, nu: (be[i], 0, 0)),
                  pl.BlockSpec((1, de, D_MODEL), lambda i, be, nu: (be[i], 0, 0))],
        out_specs=pl.BlockSpec((bm, d), lambda i, be, nu: (i, 0)),
    )
    return pl.pallas_call(
        _expert_kernel,
        grid_spec=grid_spec,
        out_shape=jax.ShapeDtypeStruct((cap, d), F32),
        compiler_params=_cparams(("arbitrary",)),
        name="experts",
    )(block_expert, n_used, xs, wg, wu, wd)


def _combine_kernel(e_ref, rk_ref, ps_ref, ys_ref, w_ref, sh_ref, x_ref, gt_ref, g_ref, o_ref, buf, sem, *, tt):
    def row_copy(src_row, k, t):
        return pltpu.make_async_copy(ys_ref.at[pl.ds(src_row, 1)], buf.at[k, pl.ds(t, 1)], sem)

    def start(t, c):
        for k in range(TOP_K):
            j = t * TOP_K + k
            row_copy(ps_ref[e_ref[j]] + rk_ref[j], k, t).start()
        return c

    lax.fori_loop(0, tt, start, 0)

    def wait(t, c):
        for k in range(TOP_K):
            row_copy(0, 0, 0).wait()
        return c

    lax.fori_loop(0, tt, wait, 0)

    w = w_ref[0]
    moe = sh_ref[0]
    for k in range(TOP_K):
        moe = moe + w[:, k:k + 1] * buf[k]
    x2 = x_ref[0] + gt_ref[0] * moe
    ms = jnp.mean(x2 * x2, axis=-1, keepdims=True)
    o_ref[0] = x2 * lax.rsqrt(ms + NORM_EPS) * g_ref[...]


def _combine(e_flat, rk_flat, pstarts, ys, wsel, shared, x1, gt2, final_g):
    b, t, d = x1.shape
    tt = 128
    nt = t // tt
    smem_blk = pl.BlockSpec((tt * TOP_K,), lambda i, j: (i * nt + j,), memory_space=pltpu.SMEM)
    return pl.pallas_call(
        functools.partial(_combine_kernel, tt=tt),
        grid=(b, nt),
        in_specs=[smem_blk, smem_blk,
                  pl.BlockSpec(memory_space=pltpu.SMEM),
                  pl.BlockSpec(memory_space=pl.ANY),
                  pl.BlockSpec((1, tt, LANES), lambda i, j: (i, j, 0)),
                  pl.BlockSpec((1, tt, d), lambda i, j: (i, j, 0)),
                  pl.BlockSpec((1, tt, d), lambda i, j: (i, j, 0)),
                  pl.BlockSpec((1, 1, d), lambda i, j: (i, 0, 0)),
                  pl.BlockSpec((1, d), lambda i, j: (0, 0))],
        out_specs=pl.BlockSpec((1, tt, d), lambda i, j: (i, j, 0)),
        out_shape=jax.ShapeDtypeStruct((b, t, d), F32),
        scratch_shapes=[pltpu.VMEM((TOP_K, tt, d), F32), pltpu.SemaphoreType.DMA(())],
        compiler_params=_cparams(("arbitrary", "arbitrary")),
        name="combine",
    )(e_flat, rk_flat, pstarts, ys, wsel, shared, x1, gt2, final_g.reshape(1, d))


def _pad_rows(w, rows):
    return jnp.zeros((rows, w.shape[1]), w.dtype).at[:w.shape[0]].set(w)


def _lora_pad_cols(w):
    out = jnp.zeros(w.shape[:-1] + (D_LORA_PAD,), w.dtype)
    for i in range(4):
        out = out.at[..., i * LANES:i * LANES + D_LORA].set(w[..., i * D_LORA:(i + 1) * D_LORA])
    return out.at[..., 4 * LANES:].set(w[..., 4 * D_LORA:])


def kernel(x, c, ctx, c_ctx, norm1_g, norm2_g, ada_w, ada_b, w_in, shift_mu, pool_w, pool_scale, w_pool_out, decay_w0, decay_w2, iclr_a0, iclr_a2, gate_g2, k_k, k_a, r_k, lnx_w, lnx_b, w_rwkv_out, w_out, router_w, router_bias, exp_w_gate, exp_w_up, exp_w_down, shared_w_gate, shared_w_up, shared_w_down, final_g):
    B, T, D = x.shape
    TC = ctx.shape[1]
    n = B * T
    l = 0

    w_in_l = w_in[l]
    w_u = w_in_l[:, :D_POOL].astype(BF16)
    w_rkv = w_in_l[:, D_POOL:D_POOL + 3 * D_ATT].astype(BF16)
    w_lora = _lora_pad_cols(w_in_l[:, D_POOL + 3 * D_ATT:D_POOL + 3 * D_ATT + 4 * D_LORA + D_GATE_LORA]).astype(BF16)
    w_gates = w_in_l[:, D_POOL + 3 * D_ATT + 4 * D_LORA + D_GATE_LORA:].astype(BF16)
    mu = shift_mu[l]
    row = lambda a: a.reshape(1, -1)
    pw = {
        "mu_rkv": row(mu[:3 * D_ATT]),
        "mu_lora": row(_lora_pad_cols(mu[3 * D_ATT:])),
        "w2f": _pad_rows(decay_w2[l, 0], LANES).astype(BF16),
        "w2b": _pad_rows(decay_w2[l, 1], LANES).astype(BF16),
        "a2f": _pad_rows(iclr_a2[l, 0], LANES).astype(BF16),
        "a2b": _pad_rows(iclr_a2[l, 1], LANES).astype(BF16),
        "w0f": row(decay_w0[l, 0]), "w0b": row(decay_w0[l, 1]),
        "a0f": row(iclr_a0[l, 0]), "a0b": row(iclr_a0[l, 1]),
        "k_k": row(k_k[l]), "k_a": row(k_a[l]), "r_k": row(r_k[l]),
    }

    cstack = jnp.zeros((8, D), F32).at[:B].set(c).at[B].set(c_ctx)
    mod = _ada_mod(cstack, ada_w[l], ada_b[l])
    sh1, sc1, gt1, sh2, sc2, gt2 = [mod[:B, i * D:(i + 1) * D].reshape(B, 1, D) for i in range(6)]
    sh1c = jnp.broadcast_to(mod[B, 0:D].reshape(1, 1, D), (B, 1, D))
    sc1c = jnp.broadcast_to(mod[B, D:2 * D].reshape(1, 1, D), (B, 1, D))

    hc = _norm_mod(ctx, norm1_g[l], sc1c, sh1c, TC).reshape(B * TC, D)
    rkv_c = _matmul(hc, w_rkv, F32, 512, 512).reshape(B, TC, 3 * D_ATT)
    lora_c = _matmul(hc, w_lora, F32, 512, D_LORA_PAD).reshape(B, TC, D_LORA_PAD)
    pc = _prepare(rkv_c, lora_c, pw, grid_mode=False)
    rc, vc, kkc, lwfc, lwbc, kfc, kbc, bfc, bbc = [a.reshape(B * N_PAIRS, TC, LANES) for a in pc[:9]]
    s0 = jnp.zeros((B * N_PAIRS, LANES, LANES), F32)
    _, st_f = _scan(rc, lwfc, kfc, vc, kkc, bfc, s0, reverse=False, emit=False)
    _, st_b = _scan(rc, lwbc, kbc, vc, kkc, bbc, s0, reverse=True, emit=False)

    h = _norm_mod(x, norm1_g[l], sc1, sh1, 512).reshape(n, D)
    u = _matmul(h, w_u, F32, 512, 512).reshape(B, T, D_POOL)
    rkv = _matmul(h, w_rkv, F32, 512, 512).reshape(B, T, 3 * D_ATT)
    lora = _matmul(h, w_lora, F32, 512, D_LORA_PAD).reshape(B, T, D_LORA_PAD)
    gates = _matmul(h, w_gates, BF16, 512, 512).reshape(B, T, 2 * D)
    pp = _prepare(rkv, lora, pw, grid_mode=True)
    r_, v_, kk_, lwf, lwb, kf, kb, bf_, bb_ = [a.reshape(B * N_PAIRS, T, LANES) for a in pp[:9]]
    bonus, gd = pp[9], pp[10]
    y_f, _ = _scan(r_, lwf, kf, v_, kk_, bf_, st_f, reverse=False, emit=True)
    y_b, _ = _scan(r_, lwb, kb, v_, kk_, bb_, st_b, reverse=True, emit=True)
    y_rwkv = _readout(y_f.reshape(B, N_PAIRS, T, LANES), y_b.reshape(B, N_PAIRS, T, LANES), bonus, gd,
                      row(lnx_w[l]), row(lnx_b[l]), gate_g2[l].astype(BF16), w_rwkv_out[l].astype(BF16))
    y_pool = _pool_branch(u, pool_w[l].astype(BF16), row(pool_scale[l]), w_pool_out[l].astype(BF16))

    router_w_pad = jnp.zeros((D, LANES), F32).at[:, :N_EXPERTS].set(router_w[l])
    x1, h2, logits = _merge(y_pool, y_rwkv, gates, x, gt1, sc2, sh2, norm2_g[l],
                                w_out[l].astype(BF16), router_w_pad)

    bias_pad = jnp.zeros((1, LANES), F32).at[0, :N_EXPERTS].set(router_bias[l])
    e_idx, e_rank, wsel, counts = _router(logits.reshape(n, LANES), bias_pad)
    e_flat = e_idx[:, :TOP_K].reshape(-1)
    rk_flat = e_rank[:, :TOP_K].reshape(-1)
    bm = EXPERT_BLOCK
    cnt = counts[0, :N_EXPERTS].astype(I32)
    padded = (cnt + bm - 1) // bm * bm
    pend = jnp.cumsum(padded)
    pstarts = (pend - padded).astype(I32)
    cap = n * TOP_K + N_EXPERTS * bm
    n_blocks = cap // bm
    block_start = jnp.arange(n_blocks, dtype=I32) * bm
    block_expert = jnp.minimum(jnp.sum(block_start[:, None] >= pend[None, :], axis=-1), N_EXPERTS - 1).astype(I32)
    n_used = (pend[-1] // bm).astype(I32).reshape(1)

    h2 = h2.reshape(n, D)
    xs = _dispatch(e_flat, rk_flat, pstarts, h2, jnp.zeros((cap, D), F32))
    ys = _experts(block_expert, n_used, xs, exp_w_gate[l].astype(BF16), exp_w_up[l].astype(BF16),
                  exp_w_down[l].astype(BF16))
    shared = _experts(jnp.zeros((n // bm,), I32), jnp.full((1,), n // bm, I32), h2,
                      shared_w_gate[l].astype(BF16)[None], shared_w_up[l].astype(BF16)[None],
                      shared_w_down[l].astype(BF16)[None])
    return _combine(e_flat, rk_flat, pstarts, ys, wsel.reshape(B, T, LANES),
                    shared.reshape(B, T, D), x1, gt2, final_g)
```

```python
import functools

import jax
import jax.numpy as jnp
from jax import lax
from jax.experimental import pallas as pl
from jax.experimental.pallas import tpu as pltpu

F32 = jnp.float32
BF16 = jnp.bfloat16
I32 = jnp.int32
U32 = jnp.uint32

D_MODEL = 2048
GRID_W = 64
POOL_WINDOWS = (2, 4, 8, 16)
POOL_GROUP = 256
D_POOL = 1024
HEAD = 64
N_HEADS = 32
N_PAIRS = N_HEADS // 2
D_ATT = 2048
D_LORA = 96
D_GATE_LORA = 256
D_LORA_PAD = 768
GN_EPS = 64e-5
NORM_EPS = 1e-6
N_EXPERTS = 64
TOP_K = 8
N_GROUPS = 8
TOPK_GROUPS = 4
D_EXPERT = 512
ROUTED_SCALE = 2.5
EXP_M05 = 0.6065306597126334

LANES = 128
CHUNK = 64
EXPERT_BLOCK = 512
VMEM_LIMIT = 56 * 1024 * 1024


def _cparams(sem):
    return pltpu.CompilerParams(dimension_semantics=sem, vmem_limit_bytes=VMEM_LIMIT)


def _dot(a, b):
    return jnp.dot(a, b, preferred_element_type=F32)


def _dot_nt(a, b):
    return lax.dot_general(a, b, (((1,), (1,)), ((), ())), preferred_element_type=F32)


def _bmm(a, b):
    return lax.dot_general(a, b, (((2,), (1,)), ((0,), (0,))), preferred_element_type=F32)


def _bmm_nt(a, b):
    return lax.dot_general(a, b, (((2,), (2,)), ((0,), (0,))), preferred_element_type=F32)


def _bmm_tn(a, b):
    return lax.dot_general(a, b, (((1,), (1,)), ((0,), (0,))), preferred_element_type=F32)


def _split2(x):
    hi = x.astype(BF16)
    lo = (x - hi.astype(F32)).astype(BF16)
    return hi, lo


def _iota(shape, axis):
    return lax.broadcasted_iota(I32, shape, axis)


def _ada_kernel(c_ref, w_ref, b_ref, o_ref):
    c = c_ref[...]
    a = c * jax.nn.sigmoid(c)
    o_ref[...] = _dot(a.astype(BF16), w_ref[...].astype(BF16)) + b_ref[...]


def _ada_mod(cstack, ada_w, ada_b):
    m, d = cstack.shape
    n = ada_w.shape[1]
    tn = 1024
    return pl.pallas_call(
        _ada_kernel,
        grid=(n // tn,),
        in_specs=[pl.BlockSpec((m, d), lambda j: (0, 0)),
                  pl.BlockSpec((d, tn), lambda j: (0, j)),
                  pl.BlockSpec((1, tn), lambda j: (0, j))],
        out_specs=pl.BlockSpec((m, tn), lambda j: (0, j)),
        out_shape=jax.ShapeDtypeStruct((m, n), F32),
        compiler_params=_cparams(("parallel",)),
        name="ada_mod",
    )(cstack, ada_w, ada_b.reshape(1, n))


def _norm_mod_kernel(x_ref, g_ref, sc_ref, sh_ref, o_ref):
    x = x_ref[0]
    ms = jnp.mean(x * x, axis=-1, keepdims=True)
    y = x * lax.rsqrt(ms + NORM_EPS) * g_ref[...]
    o_ref[0] = (y * (1.0 + sc_ref[0]) + sh_ref[0]).astype(o_ref.dtype)


def _norm_mod(x, g, sc, sh, tt):
    b, t, d = x.shape
    return pl.pallas_call(
        _norm_mod_kernel,
        grid=(b, t // tt),
        in_specs=[pl.BlockSpec((1, tt, d), lambda i, j: (i, j, 0)),
                  pl.BlockSpec((1, d), lambda i, j: (0, 0)),
                  pl.BlockSpec((1, 1, d), lambda i, j: (i, 0, 0)),
                  pl.BlockSpec((1, 1, d), lambda i, j: (i, 0, 0))],
        out_specs=pl.BlockSpec((1, tt, d), lambda i, j: (i, j, 0)),
        out_shape=jax.ShapeDtypeStruct((b, t, d), BF16),
        compiler_params=_cparams(("parallel", "parallel")),
        name="norm_mod",
    )(x, g.reshape(1, d), sc, sh)


def _mm_kernel(a_ref, b_ref, o_ref):
    o_ref[...] = _dot(a_ref[...], b_ref[...]).astype(o_ref.dtype)


def _matmul(a, b, out_dtype, tm, tn):
    m, k = a.shape
    n = b.shape[1]
    tm = min(tm, m)
    return pl.pallas_call(
        _mm_kernel,
        grid=(m // tm, n // tn),
        in_specs=[pl.BlockSpec((tm, k), lambda i, j: (i, 0)),
                  pl.BlockSpec((k, tn), lambda i, j: (0, j))],
        out_specs=pl.BlockSpec((tm, tn), lambda i, j: (i, j)),
        out_shape=jax.ShapeDtypeStruct((m, n), out_dtype),
        compiler_params=_cparams(("parallel", "parallel")),
        name="matmul",
    )(a, b)


def _shift_grid(x, prev, nxt, first, last):
    tt, c = x.shape
    col = _iota((tt, c), 0) & (GRID_W - 1)
    m = _iota((tt, c), 1) & 3
    left = jnp.where(col == 0, 0.0, pltpu.roll(x, 1, 0))
    right = jnp.where(col == GRID_W - 1, 0.0, pltpu.roll(x, tt - 1, 0))
    prev = jnp.where(first, 0.0, prev)
    nxt = jnp.where(last, 0.0, nxt)
    if tt > GRID_W:
        up = jnp.concatenate([prev, x[:tt - GRID_W]], axis=0)
        down = jnp.concatenate([x[GRID_W:], nxt], axis=0)
    else:
        up, down = prev, nxt
    return jnp.where(m == 0, left, jnp.where(m == 1, right, jnp.where(m == 2, up, down)))


def _shift_seq(x, prev8, next8, first, last):
    t, c = x.shape
    row = _iota((t, c), 0)
    odd = (_iota((t, c), 1) & 1) == 1
    before = jnp.where(first, 0.0, prev8[7:8])
    after = jnp.where(last, 0.0, next8[0:1])
    prev = jnp.where(row == 0, before, pltpu.roll(x, 1, 0))
    nxt = jnp.where(row == t - 1, after, pltpu.roll(x, t - 1, 0))
    return jnp.where(odd, nxt, prev)


def _head_sum(x):
    w = 2 * LANES
    ones = (_iota((w, w), 0) >> 6 == _iota((w, w), 1) >> 6).astype(BF16)
    outs = []
    for c in range(x.shape[1] // w):
        hi, lo = _split2(x[:, c * w:(c + 1) * w])
        outs.append(_dot(hi, ones) + _dot(lo, ones))
    return jnp.concatenate(outs, axis=1)


def _prepare_kernel(*refs, grid_mode):
    (r_ref, rp_ref, rn_ref, k_ref, kp_ref, kn_ref, v_ref, vp_ref, vn_ref,
     l_ref, lp_ref, ln_ref) = refs[:12]
    rest = refs[12:]
    (mur_ref, muk_ref, muv_ref, mul_ref, w2f_ref, w2b_ref, a2f_ref, a2b_ref,
     w0f_ref, w0b_ref, a0f_ref, a0b_ref, kk_ref, ka_ref, rk_ref,
     oaf_ref, obf_ref, okf_ref, orf_ref, owf_ref, oab_ref, obb_ref, okb_ref, orb_ref, owb_ref,
     ov_ref, obon_ref, ogd_ref) = rest

    first = pl.program_id(1) == 0
    last = pl.program_id(1) == pl.num_programs(1) - 1
    shift = _shift_grid if grid_mode else _shift_seq

    def mix(x_ref, p_ref, n_ref, mu_ref):
        x = x_ref[0]
        return x + (shift(x, p_ref[0], n_ref[0], first, last) - x) * mu_ref[...]

    r = mix(r_ref, rp_ref, rn_ref, mur_ref)
    k = mix(k_ref, kp_ref, kn_ref, muk_ref)
    v = mix(v_ref, vp_ref, vn_ref, muv_ref)
    lo = mix(l_ref, lp_ref, ln_ref, mul_ref)

    th = jnp.tanh(lo[:, :2 * LANES]).astype(BF16)
    zf = w0f_ref[...] + _dot(th[:, :LANES], w2f_ref[...])
    zb = w0b_ref[...] + _dot(th[:, LANES:], w2b_ref[...])
    lwf = -EXP_M05 * jax.nn.sigmoid(zf)
    lwb = -EXP_M05 * jax.nn.sigmoid(zb)
    ad = lo[:, 2 * LANES:4 * LANES].astype(BF16)
    af = jax.nn.sigmoid(a0f_ref[...] + _dot(ad[:, :LANES], a2f_ref[...]))
    ab = jax.nn.sigmoid(a0b_ref[...] + _dot(ad[:, LANES:], a2b_ref[...]))
    ogd_ref[0] = lo[:, 4 * LANES:]

    kk = k * kk_ref[...]
    kk = kk * lax.rsqrt(_head_sum(kk * kk) + 1e-12)
    ka = ka_ref[...]
    kf = k * (1.0 + (af - 1.0) * ka)
    kb = k * (1.0 + (ab - 1.0) * ka)
    bonus = _head_sum(r * (kf + kb) * rk_ref[...]) * v

    tt = r.shape[0]
    t2 = _iota((tt, tt), 0)
    s2 = _iota((tt, tt), 1)
    same = (t2 >> 6) == (s2 >> 6)

    def scan_operands(lw, b, kd, reverse):
        tri = (same & ((s2 >= t2) if reverse else (s2 <= t2))).astype(BF16)
        hi = lw.astype(BF16)
        r1 = lw - hi.astype(F32)
        mid = r1.astype(BF16)
        lo3 = (r1 - mid.astype(F32)).astype(BF16)
        cum = _dot(tri, hi) + _dot(tri, mid) + _dot(tri, lo3)
        w_inc = jnp.exp(cum)
        w_inv = jnp.exp(-cum)
        ends = [w_inc[c * CHUNK:c * CHUNK + 1] if reverse else w_inc[(c + 1) * CHUNK - 1:(c + 1) * CHUNK]
                for c in range(tt // CHUNK)]
        w_last = jnp.concatenate(ends + [jnp.zeros((8 - len(ends), lw.shape[1]), F32)], axis=0)
        return jnp.exp(cum - lw) * kk, b * w_inv, kd * w_inv, w_inc * r, w_last

    al_f, be_f, ka_f, rh_f, wl_f = scan_operands(lwf, kk * af, kf, False)
    al_b, be_b, ka_b, rh_b, wl_b = scan_operands(lwb, kk * ab, kb, True)
    outs = ((oaf_ref, al_f), (obf_ref, be_f), (okf_ref, ka_f), (orf_ref, rh_f), (owf_ref, wl_f),
            (oab_ref, al_b), (obb_ref, be_b), (okb_ref, ka_b), (orb_ref, rh_b), (owb_ref, wl_b),
            (ov_ref, v), (obon_ref, bonus))
    for o_ref, val in outs:
        for p in range(N_PAIRS):
            o_ref[0, p] = val[:, p * LANES:(p + 1) * LANES].astype(o_ref.dtype)


def _prepare(rkv, lora, pw, grid_mode):
    b, t, _ = rkv.shape
    d = D_ATT
    tt = 2 * GRID_W
    halo = GRID_W if grid_mode else 8
    hpt = tt // halo
    nhalo = t // halo
    grid = (b, t // tt)

    def tile_specs(w, c):
        return [pl.BlockSpec((1, tt, w), lambda i, j: (i, j, c)),
                pl.BlockSpec((1, halo, w), lambda i, j: (i, jnp.maximum(j * hpt - 1, 0), c)),
                pl.BlockSpec((1, halo, w), lambda i, j: (i, jnp.minimum((j + 1) * hpt, nhalo - 1), c))]

    in_specs = []
    args = []
    for c in range(3):
        in_specs += tile_specs(d, c)
        args += [rkv, rkv, rkv]
    in_specs += tile_specs(D_LORA_PAD, 0)
    args += [lora, lora, lora]

    def vec(c=0, w=d):
        return pl.BlockSpec((1, w), lambda i, j, c=c: (0, c))

    def full(shape):
        return pl.BlockSpec(shape, lambda i, j: (0,) * len(shape))

    in_specs += [vec(0), vec(1), vec(2), vec(0, D_LORA_PAD)]
    args += [pw["mu_rkv"], pw["mu_rkv"], pw["mu_rkv"], pw["mu_lora"]]
    in_specs += [full((LANES, d))] * 4
    args += [pw["w2f"], pw["w2b"], pw["a2f"], pw["a2b"]]
    in_specs += [vec()] * 7
    args += [pw["w0f"], pw["w0b"], pw["a0f"], pw["a0b"], pw["k_k"], pw["k_a"], pw["r_k"]]

    pair_spec = pl.BlockSpec((1, N_PAIRS, tt, LANES), lambda i, j: (i, 0, j, 0))
    wl_spec = pl.BlockSpec((1, N_PAIRS, 8, LANES), lambda i, j: (i, 0, j, 0))

    def pair(dtype):
        return jax.ShapeDtypeStruct((b, N_PAIRS, t, LANES), dtype)

    wl = jax.ShapeDtypeStruct((b, N_PAIRS, t // tt * 8, LANES), F32)
    direction = [pair(BF16)] * 4 + [wl]
    out_shape = direction * 2 + [pair(BF16), pair(F32), jax.ShapeDtypeStruct((b, t, D_GATE_LORA), F32)]
    out_specs = ([pair_spec] * 4 + [wl_spec]) * 2 + [pair_spec, pair_spec,
                                                    pl.BlockSpec((1, tt, D_GATE_LORA), lambda i, j: (i, j, 0))]
    return pl.pallas_call(
        functools.partial(_prepare_kernel, grid_mode=grid_mode),
        grid=grid, in_specs=in_specs, out_specs=out_specs, out_shape=out_shape,
        compiler_params=_cparams(("parallel", "parallel")),
        name="prepare_grid" if grid_mode else "prepare_seq",
    )(*args)


def _scan_kernel(*refs, reverse, pairs, tb, emit):
    al_ref, be_ref, ka_ref, rh_ref, wl_ref, v_ref, s0_ref = refs[:7]
    if emit:
        y_ref, st_ref, s_scr = refs[7:]
    else:
        st_ref, s_scr = refs[7:]
    L = CHUNK
    n_chunks = tb // L

    @pl.when(pl.program_id(1) == 0)
    def _():
        s_scr[...] = s0_ref[...]

    t2 = _iota((2 * L, 2 * L), 0)
    s2 = _iota((2 * L, 2 * L), 1)
    same = (t2 >> 6) == (s2 >> 6)
    tl = t2 & (L - 1)
    sl = s2 & (L - 1)
    strict = same & ((sl > tl) if reverse else (sl < tl))
    incl = same & ((sl >= tl) if reverse else (sl <= tl))
    eye = (t2 == s2).astype(F32)
    head_a = _iota((pairs, L, LANES), 2) < HEAD
    zero = jnp.zeros((), BF16)

    def stack(ref, rows):
        x = ref[:, rows, :]
        return jnp.concatenate([jnp.where(head_a, x, zero), jnp.where(head_a, zero, x)], axis=1)

    def chunk(ci, carry):
        cc = (n_chunks - 1 - ci) if reverse else ci
        rows = pl.ds(pl.multiple_of(cc * L, L), L)
        a_s = stack(al_ref, rows)
        b_s = stack(be_ref, rows)
        k_s = stack(ka_ref, rows)
        v_s = stack(v_ref, rows)
        lhs = jnp.concatenate([a_s, stack(rh_ref, rows)], axis=1) if emit else a_s
        xb = _bmm_nt(lhs, b_s)
        xk = _bmm_nt(lhs, k_s)
        m1 = jnp.where(strict, xb[:, :2 * L], 0.0)
        m2 = jnp.where(strict, xk[:, :2 * L], 0.0)
        xp = -m1
        tinv = eye + xp
        for _ in range(5):
            xq = xp.astype(BF16)
            xp = _bmm(xq, xq)
            tinv = tinv + _bmm(tinv.astype(BF16), xp.astype(BF16))
        s = s_scr[...]
        xs = _bmm_nt(lhs, s.astype(BF16))
        if emit:
            n1 = jnp.where(incl, xb[:, 2 * L:], 0.0)
            n2 = jnp.where(incl, xk[:, 2 * L:], 0.0)
            xv = _bmm(jnp.concatenate([m2, n2], axis=1).astype(BF16), v_s)
        else:
            xv = _bmm(m2.astype(BF16), v_s)
        g = xs[:, :2 * L] + xv[:, :2 * L]
        u_s = (-_bmm(tinv.astype(BF16), g.astype(BF16))).astype(BF16)
        if emit:
            y2 = xs[:, 2 * L:] + xv[:, 2 * L:] + _bmm(n1.astype(BF16), u_s)
            y_ref[:, rows, :] = y2[:, :L] + y2[:, L:]
        w_last = wl_ref[:, pl.ds((cc >> 1) * 8 + (cc & 1), 1), :]
        upd = _bmm_tn(jnp.concatenate([u_s, v_s], axis=1), jnp.concatenate([b_s, k_s], axis=1))
        s_scr[...] = (s + upd) * w_last
        return carry

    lax.fori_loop(0, n_chunks, chunk, 0)

    @pl.when(pl.program_id(1) == pl.num_programs(1) - 1)
    def _():
        st_ref[...] = s_scr[...]


def _scan(al, be, ka, rh, wl, v, s0, reverse, emit, pairs=16, tb=256):
    bp, t, _ = al.shape
    tb = min(t, tb)
    nb = t // tb

    def tmap(g, c):
        return (g, (nb - 1 - c) if reverse else c, 0)

    data = pl.BlockSpec((pairs, tb, LANES), tmap)
    wl_spec = pl.BlockSpec((pairs, tb // (2 * CHUNK) * 8, LANES), tmap)
    state = pl.BlockSpec((pairs, LANES, LANES), lambda g, c: (g, 0, 0))
    out_shape = [jax.ShapeDtypeStruct((bp, LANES, LANES), F32)]
    out_specs = [state]
    if emit:
        out_shape = [jax.ShapeDtypeStruct((bp, t, LANES), F32)] + out_shape
        out_specs = [data] + out_specs
    res = pl.pallas_call(
        functools.partial(_scan_kernel, reverse=reverse, pairs=pairs, tb=tb, emit=emit),
        grid=(bp // pairs, nb),
        in_specs=[data] * 4 + [wl_spec, data, state],
        out_specs=out_specs, out_shape=out_shape,
        scratch_shapes=[pltpu.VMEM((pairs, LANES, LANES), F32)],
        compiler_params=_cparams(("parallel", "arbitrary")),
        name="scan_" + ("bwd" if reverse else "fwd") + ("_emit" if emit else "_state"),
    )(al, be, ka, rh, wl, v, s0)
    return (res[0], res[1]) if emit else (None, res[0])


def _readout_kernel(yf_ref, yb_ref, bon_ref, gd_ref, lnw_ref, lnb_ref, g2_ref, w_ref, o_ref):
    y = jnp.concatenate([yf_ref[0, p] + yb_ref[0, p] for p in range(N_PAIRS)], axis=1)
    bonus = jnp.concatenate([bon_ref[0, p] for p in range(N_PAIRS)], axis=1)
    mean = _head_sum(y) * (1.0 / HEAD)
    dlt = y - mean
    var = _head_sum(dlt * dlt) * (1.0 / HEAD)
    yn = dlt * lax.rsqrt(var + GN_EPS) * lnw_ref[...] + lnb_ref[...]
    gate = _dot(jax.nn.sigmoid(gd_ref[0]).astype(BF16), g2_ref[...])
    out = ((yn + bonus) * gate).astype(BF16)
    o_ref[0] = _dot(out, w_ref[...]).astype(o_ref.dtype)


def _readout(yf, yb, bonus, gd, lnw, lnb, g2, w_out):
    b, _, t, _ = yf.shape
    tt = 256
    d = D_ATT
    pair_spec = pl.BlockSpec((1, N_PAIRS, tt, LANES), lambda i, j: (i, 0, j, 0))
    return pl.pallas_call(
        _readout_kernel,
        grid=(b, t // tt),
        in_specs=[pair_spec, pair_spec, pair_spec,
                  pl.BlockSpec((1, tt, D_GATE_LORA), lambda i, j: (i, j, 0)),
                  pl.BlockSpec((1, d), lambda i, j: (0, 0)),
                  pl.BlockSpec((1, d), lambda i, j: (0, 0)),
                  pl.BlockSpec((D_GATE_LORA, d), lambda i, j: (0, 0)),
                  pl.BlockSpec((d, D_MODEL), lambda i, j: (0, 0))],
        out_specs=pl.BlockSpec((1, tt, D_MODEL), lambda i, j: (i, j, 0)),
        out_shape=jax.ShapeDtypeStruct((b, t, D_MODEL), BF16),
        compiler_params=_cparams(("parallel", "parallel")),
        name="readout",
    )(yf, yb, bonus, gd, lnw, lnb, g2, w_out)


def _pool_kernel(u_ref, pw_ref, ps_ref, wo_ref, o_ref):
    u = u_ref[0]
    tt = u.shape[0]
    t2 = _iota((tt, tt), 0)
    s2 = _iota((tt, tt), 1)
    same = (t2 >> 6) == (s2 >> 6)
    tc = t2 & (GRID_W - 1)
    sc = s2 & (GRID_W - 1)
    col = _iota((tt, POOL_GROUP), 0) & (GRID_W - 1)
    ys = []
    for gi, w in enumerate(POOL_WINDOWS):
        ug = u[:, gi * POOL_GROUP:(gi + 1) * POOL_GROUP]
        win = (same & (sc >= tc - w // 2) & (sc < tc + (w - w // 2))).astype(BF16)
        hi, lo = _split2(ug)
        wsum = _dot(win, hi) + _dot(win, lo)
        cnt = (jnp.minimum(col + (w - w // 2), GRID_W) - jnp.maximum(col - w // 2, 0)).astype(F32)
        dlt = wsum / cnt - ug
        ys.append(_dot(dlt.astype(BF16), pw_ref[gi]))
    y1 = jnp.concatenate(ys, axis=1) * ps_ref[...]
    o_ref[0] = _dot(y1.astype(BF16), wo_ref[...]).astype(o_ref.dtype)


def _pool_branch(u, pool_w, pool_scale, w_pool_out):
    b, t, _ = u.shape
    tt = 256
    return pl.pallas_call(
        _pool_kernel,
        grid=(b, t // tt),
        in_specs=[pl.BlockSpec((1, tt, D_POOL), lambda i, j: (i, j, 0)),
                  pl.BlockSpec((4, POOL_GROUP, POOL_GROUP), lambda i, j: (0, 0, 0)),
                  pl.BlockSpec((1, D_POOL), lambda i, j: (0, 0)),
                  pl.BlockSpec((D_POOL, D_MODEL), lambda i, j: (0, 0))],
        out_specs=pl.BlockSpec((1, tt, D_MODEL), lambda i, j: (i, j, 0)),
        out_shape=jax.ShapeDtypeStruct((b, t, D_MODEL), BF16),
        compiler_params=_cparams(("parallel", "parallel")),
        name="pool_branch",
    )(u, pool_w, pool_scale, w_pool_out)


def _merge_kernel(yp_ref, yr_ref, gp_ref, gr_ref, x_ref, gt_ref, sc_ref, sh_ref, g_ref, w_ref, rw_ref,
                  x1_ref, h_ref, lg_ref):
    m = (jax.nn.sigmoid(gp_ref[0].astype(F32)) * yp_ref[0].astype(F32)
         + jax.nn.sigmoid(gr_ref[0].astype(F32)) * yr_ref[0].astype(F32))
    x1 = x_ref[0] + gt_ref[0] * _dot(m.astype(BF16), w_ref[...])
    x1_ref[0] = x1
    ms = jnp.mean(x1 * x1, axis=-1, keepdims=True)
    h = x1 * lax.rsqrt(ms + NORM_EPS) * g_ref[...]
    h = h * (1.0 + sc_ref[0]) + sh_ref[0]
    h_ref[0] = h
    hh, hl = _split2(h)
    rh, rl = _split2(rw_ref[...])
    lg_ref[0] = _dot(hh, rh) + _dot(hl, rh) + _dot(hh, rl)


def _merge(y_pool, y_rwkv, gates, x, gt1, sc2, sh2, g2, w_out, router_w_pad):
    b, t, d = x.shape
    tt = 256
    tile = pl.BlockSpec((1, tt, d), lambda i, j: (i, j, 0))
    mod = pl.BlockSpec((1, 1, d), lambda i, j: (i, 0, 0))
    return pl.pallas_call(
        _merge_kernel,
        grid=(b, t // tt),
        in_specs=[tile, tile,
                  pl.BlockSpec((1, tt, d), lambda i, j: (i, j, 0)),
                  pl.BlockSpec((1, tt, d), lambda i, j: (i, j, 1)),
                  tile, mod, mod, mod,
                  pl.BlockSpec((1, d), lambda i, j: (0, 0)),
                  pl.BlockSpec((d, d), lambda i, j: (0, 0)),
                  pl.BlockSpec((d, LANES), lambda i, j: (0, 0))],
        out_specs=[tile, tile,
                   pl.BlockSpec((1, tt, LANES), lambda i, j: (i, j, 0))],
        out_shape=[jax.ShapeDtypeStruct((b, t, d), F32),
                   jax.ShapeDtypeStruct((b, t, d), F32),
                   jax.ShapeDtypeStruct((b, t, LANES), F32)],
        compiler_params=_cparams(("parallel", "parallel")),
        name="merge",
    )(y_pool, y_rwkv, gates, gates, x, gt1, sc2, sh2, g2.reshape(1, d), w_out, router_w_pad)


def _router_kernel(lg_ref, bias_ref, e_ref, rk_ref, w_ref, cnt_ref, carry):
    tt = lg_ref.shape[0]
    shape = (tt, LANES)
    lane = _iota(shape, 1)
    valid = lane < N_EXPERTS
    grp = (lane & (N_EXPERTS - 1)) >> 3
    neg = jnp.float32(-jnp.inf)

    @pl.when(pl.program_id(0) == 0)
    def _():
        carry[...] = jnp.zeros_like(carry)

    scores = jax.nn.sigmoid(lg_ref[...])
    sel = scores + bias_ref[...]
    sel = jnp.where(valid, sel, pltpu.roll(sel, N_EXPERTS, 1))

    def group_reduce(x, op):
        for sh in (1, 2, 4):
            up = pltpu.roll(x, sh, 1)
            dn = pltpu.roll(x, LANES - sh, 1)
            x = op(x, jnp.where((lane & sh) != 0, up, dn))
        return x

    m1 = group_reduce(sel, jnp.maximum)
    first = group_reduce(jnp.where(sel == m1, lane, LANES), jnp.minimum)
    m2 = group_reduce(jnp.where(lane == first, neg, sel), jnp.maximum)
    gs = m1 + m2
    beaten = jnp.zeros(shape, I32)
    for k in range(1, N_GROUPS):
        other = pltpu.roll(gs, 8 * k, 1)
        og = (grp - k) & (N_GROUPS - 1)
        beaten = beaten + ((other > gs) | ((other == gs) & (og < grp))).astype(I32)
    cur = jnp.where((beaten < TOPK_GROUPS) & valid, sel, neg)

    picked = jnp.zeros(shape, jnp.bool_)
    e_acc = jnp.zeros(shape, I32)
    w_acc = jnp.zeros(shape, F32)
    idxs = []
    for k in range(TOP_K):
        m = jnp.max(cur, axis=1, keepdims=True)
        idx = jnp.min(jnp.where(cur == m, lane, LANES), axis=1, keepdims=True)
        oh = lane == idx
        sc = jnp.sum(jnp.where(oh, scores, 0.0), axis=1, keepdims=True)
        e_acc = jnp.where(lane == k, idx, e_acc)
        w_acc = jnp.where(lane == k, sc, w_acc)
        picked = picked | oh
        cur = jnp.where(oh, neg, cur)
        idxs.append(idx)
    wsum = jnp.sum(w_acc, axis=1, keepdims=True)
    w_ref[...] = w_acc / wsum * ROUTED_SCALE
    e_ref[...] = e_acc

    lower = (_iota((tt, tt), 1) < _iota((tt, tt), 0)).astype(BF16)
    pk = picked.astype(BF16)
    before = _dot(lower, pk) + carry[...]
    r_acc = jnp.zeros(shape, F32)
    for k in range(TOP_K):
        rk = jnp.sum(jnp.where(lane == idxs[k], before, 0.0), axis=1, keepdims=True)
        r_acc = jnp.where(lane == k, rk, r_acc)
    rk_ref[...] = r_acc.astype(I32)
    carry[...] = carry[...] + jnp.sum(picked.astype(F32), axis=0, keepdims=True)
    cnt_ref[...] = carry[...]


def _router(logits, bias_pad):
    n = logits.shape[0]
    tt = 256
    tile = pl.BlockSpec((tt, LANES), lambda i: (i, 0))
    row = pl.BlockSpec((1, LANES), lambda i: (0, 0))
    return pl.pallas_call(
        _router_kernel,
        grid=(n // tt,),
        in_specs=[tile, row],
        out_specs=[tile, tile, tile, row],
        out_shape=[jax.ShapeDtypeStruct((n, LANES), I32), jax.ShapeDtypeStruct((n, LANES), I32),
                   jax.ShapeDtypeStruct((n, LANES), F32), jax.ShapeDtypeStruct((1, LANES), F32)],
        scratch_shapes=[pltpu.VMEM((1, LANES), F32)],
        compiler_params=_cparams(("arbitrary",)),
        name="router",
    )(logits, bias_pad)


def _dispatch_kernel(e_ref, rk_ref, ps_ref, cnt_ref, h_ref, xs_ref, zbuf, sem, zsem, *, tt):
    bm = EXPERT_BLOCK

    @pl.when(pl.program_id(0) == 0)
    def _():
        zbuf[...] = jnp.zeros_like(zbuf)

        def tail_copy(e):
            last = ps_ref[e] + ((cnt_ref[e] + bm - 1) & -bm) - bm
            return pltpu.make_async_copy(zbuf, xs_ref.at[pl.ds(pl.multiple_of(last, bm), bm)], zsem)

        def tail_start(e, c):
            @pl.when((cnt_ref[e] & (bm - 1)) != 0)
            def _():
                tail_copy(e).start()
            return c

        def tail_wait(e, c):
            @pl.when((cnt_ref[e] & (bm - 1)) != 0)
            def _():
                tail_copy(e).wait()
            return c

        lax.fori_loop(0, N_EXPERTS, tail_start, 0)
        lax.fori_loop(0, N_EXPERTS, tail_wait, 0)

    def row_copy(src_row, dst_row):
        return pltpu.make_async_copy(h_ref.at[pl.ds(src_row, 1)], xs_ref.at[pl.ds(dst_row, 1)], sem)

    def start(t, c):
        for k in range(TOP_K):
            j = t * TOP_K + k
            row_copy(t, ps_ref[e_ref[j]] + rk_ref[j]).start(priority=k % 2)
        return c

    lax.fori_loop(0, tt, start, 0)

    def wait(t, c):
        for k in range(TOP_K):
            row_copy(0, 0).wait()
        return c

    lax.fori_loop(0, tt, wait, 0)


def _dispatch(e_flat, rk_flat, pstarts, counts, h_rows, cap):
    n, d = h_rows.shape
    tt = min(512, n)
    smem_blk = pl.BlockSpec((tt * TOP_K,), lambda i: (i,), memory_space=pltpu.SMEM)
    return pl.pallas_call(
        functools.partial(_dispatch_kernel, tt=tt),
        grid=(n // tt,),
        in_specs=[smem_blk, smem_blk,
                  pl.BlockSpec(memory_space=pltpu.SMEM),
                  pl.BlockSpec(memory_space=pltpu.SMEM),
                  pl.BlockSpec((tt, d), lambda i: (i, 0))],
        out_specs=pl.BlockSpec(memory_space=pl.ANY),
        out_shape=jax.ShapeDtypeStruct((cap, d), h_rows.dtype),
        scratch_shapes=[pltpu.VMEM((EXPERT_BLOCK, d), h_rows.dtype),
                        pltpu.SemaphoreType.DMA(()), pltpu.SemaphoreType.DMA(())],
        compiler_params=_cparams(("arbitrary",)),
        name="dispatch",
    )(e_flat, rk_flat, pstarts, counts, h_rows)


def _expert_kernel(be_ref, nu_ref, x_ref, wg_ref, wu_ref, wd_ref, o_ref, wg_bf, wu_bf, wd_bf):
    i = pl.program_id(0)
    used = i < nu_ref[0]
    new_expert = (i == 0) | (be_ref[i] != be_ref[jnp.maximum(i - 1, 0)])

    @pl.when(used & new_expert)
    def _():
        wg_bf[...] = wg_ref[0].astype(BF16)
        wu_bf[...] = wu_ref[0].astype(BF16)
        wd_bf[...] = wd_ref[0].astype(BF16)

    @pl.when(used)
    def _():
        xb = x_ref[...].astype(BF16)
        g = _dot(xb, wg_bf[...])
        u = _dot(xb, wu_bf[...])
        act = (g * jax.nn.sigmoid(g) * u).astype(BF16)
        o_ref[...] = _dot(act, wd_bf[...])

    @pl.when(jnp.logical_not(used))
    def _():
        o_ref[...] = jnp.zeros_like(o_ref)


def _experts(block_expert, n_used, xs, wg, wu, wd):
    cap, d = xs.shape
    bm = EXPERT_BLOCK
    de = wg.shape[2]
    grid_spec = pltpu.PrefetchScalarGridSpec(
        num_scalar_prefetch=2,
        grid=(cap // bm,),
        in_specs=[pl.BlockSpec((bm, d), lambda i, be, nu: (jnp.minimum(i, nu[0] - 1), 0)),
                  pl.BlockSpec((1, D_MODEL, de), lambda i, be, nu: (be[i], 0, 0)),
                  pl.BlockSpec((1, D_MODEL, de), lambda i, be, nu: (be[i], 0, 0)),
                  pl.BlockSpec((1, de, D_MODEL), lambda i, be, nu: (be[i], 0, 0))],
        out_specs=pl.BlockSpec((bm, d), lambda i, be, nu: (i, 0)),
        scratch_shapes=[pltpu.VMEM((D_MODEL, de), BF16), pltpu.VMEM((D_MODEL, de), BF16),
                        pltpu.VMEM((de, D_MODEL), BF16)],
    )
    return pl.pallas_call(
        _expert_kernel,
        grid_spec=grid_spec,
        out_shape=jax.ShapeDtypeStruct((cap, d), F32),
        compiler_params=_cparams(("arbitrary",)),
        name="experts",
    )(block_expert, n_used, xs, wg, wu, wd)


def _combine_kernel(e_ref, rk_ref, ps_ref, ys_ref, w_ref, sh_ref, x_ref, gt_ref, g_ref, o_ref, buf, sem, *, tt):
    def row_copy(src_row, k, t):
        return pltpu.make_async_copy(ys_ref.at[pl.ds(src_row, 1)], buf.at[k, pl.ds(t, 1)], sem)

    def start(t, c):
        for k in range(TOP_K):
            j = t * TOP_K + k
            row_copy(ps_ref[e_ref[j]] + rk_ref[j], k, t).start(priority=k % 2)
        return c

    lax.fori_loop(0, tt, start, 0)

    def wait(t, c):
        for k in range(TOP_K):
            row_copy(0, 0, 0).wait()
        return c

    lax.fori_loop(0, tt, wait, 0)

    w = w_ref[0]
    moe = sh_ref[0]
    for k in range(TOP_K):
        moe = moe + w[:, k:k + 1] * buf[k]
    x2 = x_ref[0] + gt_ref[0] * moe
    ms = jnp.mean(x2 * x2, axis=-1, keepdims=True)
    o_ref[0] = x2 * lax.rsqrt(ms + NORM_EPS) * g_ref[...]


def _combine(e_flat, rk_flat, pstarts, ys, wsel, shared, x1, gt2, final_g):
    b, t, d = x1.shape
    tt = 128
    nt = t // tt
    smem_blk = pl.BlockSpec((tt * TOP_K,), lambda i, j: (i * nt + j,), memory_space=pltpu.SMEM)
    return pl.pallas_call(
        functools.partial(_combine_kernel, tt=tt),
        grid=(b, nt),
        in_specs=[smem_blk, smem_blk,
                  pl.BlockSpec(memory_space=pltpu.SMEM),
                  pl.BlockSpec(memory_space=pl.ANY),
                  pl.BlockSpec((1, tt, LANES), lambda i, j: (i, j, 0)),
                  pl.BlockSpec((1, tt, d), lambda i, j: (i, j, 0)),
                  pl.BlockSpec((1, tt, d), lambda i, j: (i, j, 0)),
                  pl.BlockSpec((1, 1, d), lambda i, j: (i, 0, 0)),
                  pl.BlockSpec((1, d), lambda i, j: (0, 0))],
        out_specs=pl.BlockSpec((1, tt, d), lambda i, j: (i, j, 0)),
        out_shape=jax.ShapeDtypeStruct((b, t, d), F32),
        scratch_shapes=[pltpu.VMEM((TOP_K, tt, d), F32), pltpu.SemaphoreType.DMA(())],
        compiler_params=_cparams(("arbitrary", "arbitrary")),
        name="combine",
    )(e_flat, rk_flat, pstarts, ys, wsel, shared, x1, gt2, final_g.reshape(1, d))


def _pad_rows(w, rows):
    return jnp.zeros((rows, w.shape[1]), w.dtype).at[:w.shape[0]].set(w)


def _lora_pad_cols(w):
    out = jnp.zeros(w.shape[:-1] + (D_LORA_PAD,), w.dtype)
    for i in range(4):
        out = out.at[..., i * LANES:i * LANES + D_LORA].set(w[..., i * D_LORA:(i + 1) * D_LORA])
    return out.at[..., 4 * LANES:].set(w[..., 4 * D_LORA:])


def kernel(x, c, ctx, c_ctx, norm1_g, norm2_g, ada_w, ada_b, w_in, shift_mu, pool_w, pool_scale, w_pool_out, decay_w0, decay_w2, iclr_a0, iclr_a2, gate_g2, k_k, k_a, r_k, lnx_w, lnx_b, w_rwkv_out, w_out, router_w, router_bias, exp_w_gate, exp_w_up, exp_w_down, shared_w_gate, shared_w_up, shared_w_down, final_g):
    B, T, D = x.shape
    TC = ctx.shape[1]
    n = B * T
    l = 0

    w_in_l = w_in[l]
    w_u = w_in_l[:, :D_POOL].astype(BF16)
    w_rkv = w_in_l[:, D_POOL:D_POOL + 3 * D_ATT].astype(BF16)
    w_lora = _lora_pad_cols(w_in_l[:, D_POOL + 3 * D_ATT:D_POOL + 3 * D_ATT + 4 * D_LORA + D_GATE_LORA]).astype(BF16)
    w_gates = w_in_l[:, D_POOL + 3 * D_ATT + 4 * D_LORA + D_GATE_LORA:].astype(BF16)
    mu = shift_mu[l]
    row = lambda a: a.reshape(1, -1)
    pw = {
        "mu_rkv": row(mu[:3 * D_ATT]),
        "mu_lora": row(_lora_pad_cols(mu[3 * D_ATT:])),
        "w2f": _pad_rows(decay_w2[l, 0], LANES).astype(BF16),
        "w2b": _pad_rows(decay_w2[l, 1], LANES).astype(BF16),
        "a2f": _pad_rows(iclr_a2[l, 0], LANES).astype(BF16),
        "a2b": _pad_rows(iclr_a2[l, 1], LANES).astype(BF16),
        "w0f": row(decay_w0[l, 0]), "w0b": row(decay_w0[l, 1]),
        "a0f": row(iclr_a0[l, 0]), "a0b": row(iclr_a0[l, 1]),
        "k_k": row(k_k[l]), "k_a": row(k_a[l]), "r_k": row(r_k[l]),
    }

    cstack = jnp.zeros((8, D), F32).at[:B].set(c).at[B].set(c_ctx)
    mod = _ada_mod(cstack, ada_w[l], ada_b[l])
    sh1, sc1, gt1, sh2, sc2, gt2 = [mod[:B, i * D:(i + 1) * D].reshape(B, 1, D) for i in range(6)]
    sh1c = jnp.broadcast_to(mod[B, 0:D].reshape(1, 1, D), (B, 1, D))
    sc1c = jnp.broadcast_to(mod[B, D:2 * D].reshape(1, 1, D), (B, 1, D))

    hc = _norm_mod(ctx, norm1_g[l], sc1c, sh1c, TC).reshape(B * TC, D)
    rkv_c = _matmul(hc, w_rkv, F32, 512, 512).reshape(B, TC, 3 * D_ATT)
    lora_c = _matmul(hc, w_lora, F32, 512, D_LORA_PAD).reshape(B, TC, D_LORA_PAD)
    pc = _prepare(rkv_c, lora_c, pw, grid_mode=False)
    flat = lambda a: a.reshape(B * N_PAIRS, a.shape[2], LANES)
    s0 = jnp.zeros((B * N_PAIRS, LANES, LANES), F32)
    vc = flat(pc[10])
    _, st_f = _scan(*[flat(a) for a in pc[0:5]], vc, s0, reverse=False, emit=False)
    _, st_b = _scan(*[flat(a) for a in pc[5:10]], vc, s0, reverse=True, emit=False)

    h = _norm_mod(x, norm1_g[l], sc1, sh1, 512).reshape(n, D)
    u = _matmul(h, w_u, F32, 2048, 512).reshape(B, T, D_POOL)
    rkv = _matmul(h, w_rkv, F32, 2048, 512).reshape(B, T, 3 * D_ATT)
    lora = _matmul(h, w_lora, F32, 2048, D_LORA_PAD).reshape(B, T, D_LORA_PAD)
    gates = _matmul(h, w_gates, BF16, 2048, 512).reshape(B, T, 2 * D)
    pp = _prepare(rkv, lora, pw, grid_mode=True)
    v_, bonus, gd = flat(pp[10]), pp[11], pp[12]
    y_f, _ = _scan(*[flat(a) for a in pp[0:5]], v_, st_f, reverse=False, emit=True)
    y_b, _ = _scan(*[flat(a) for a in pp[5:10]], v_, st_b, reverse=True, emit=True)
    y_rwkv = _readout(y_f.reshape(B, N_PAIRS, T, LANES), y_b.reshape(B, N_PAIRS, T, LANES), bonus, gd,
                      row(lnx_w[l]), row(lnx_b[l]), gate_g2[l].astype(BF16), w_rwkv_out[l].astype(BF16))
    y_pool = _pool_branch(u, pool_w[l].astype(BF16), row(pool_scale[l]), w_pool_out[l].astype(BF16))

    router_w_pad = jnp.zeros((D, LANES), F32).at[:, :N_EXPERTS].set(router_w[l])
    x1, h2, logits = _merge(y_pool, y_rwkv, gates, x, gt1, sc2, sh2, norm2_g[l],
                                w_out[l].astype(BF16), router_w_pad)

    bias_pad = jnp.zeros((1, LANES), F32).at[0, :N_EXPERTS].set(router_bias[l])
    e_idx, e_rank, wsel, counts = _router(logits.reshape(n, LANES), bias_pad)
    e_flat = e_idx[:, :TOP_K].reshape(-1)
    rk_flat = e_rank[:, :TOP_K].reshape(-1)
    bm = EXPERT_BLOCK
    cnt = counts[0, :N_EXPERTS].astype(I32)
    padded = (cnt + bm - 1) // bm * bm
    pend = jnp.cumsum(padded)
    pstarts = (pend - padded).astype(I32)
    cap = n * TOP_K + N_EXPERTS * bm
    n_blocks = cap // bm
    block_start = jnp.arange(n_blocks, dtype=I32) * bm
    block_expert = jnp.minimum(jnp.sum(block_start[:, None] >= pend[None, :], axis=-1), N_EXPERTS - 1).astype(I32)
    n_used = (pend[-1] // bm).astype(I32).reshape(1)

    h2 = h2.reshape(n, D)
    xs = _dispatch(e_flat, rk_flat, pstarts, cnt, h2, cap)
    ys = _experts(block_expert, n_used, xs, exp_w_gate[l], exp_w_up[l], exp_w_down[l])
    shared = _experts(jnp.zeros((n // bm,), I32), jnp.full((1,), n // bm, I32), h2,
                      shared_w_gate[l][None], shared_w_up[l][None], shared_w_down[l][None])
    return _combine(e_flat, rk_flat, pstarts, ys, wsel.reshape(B, T, LANES),
                    shared.reshape(B, T, D), x1, gt2, final_g)
```

```python
import functools

import jax
import jax.numpy as jnp
from jax import lax
from jax.experimental import pallas as pl
from jax.experimental.pallas import tpu as pltpu

F32 = jnp.float32
BF16 = jnp.bfloat16
I32 = jnp.int32
U32 = jnp.uint32

D_MODEL = 2048
GRID_W = 64
POOL_WINDOWS = (2, 4, 8, 16)
POOL_GROUP = 256
D_POOL = 1024
HEAD = 64
N_HEADS = 32
N_PAIRS = N_HEADS // 2
D_ATT = 2048
D_LORA = 96
D_GATE_LORA = 256
D_LORA_PAD = 768
GN_EPS = 64e-5
NORM_EPS = 1e-6
N_EXPERTS = 64
TOP_K = 8
N_GROUPS = 8
TOPK_GROUPS = 4
D_EXPERT = 512
ROUTED_SCALE = 2.5
EXP_M05 = 0.6065306597126334

LANES = 128
CHUNK = 64
EXPERT_BLOCK = 512
VMEM_LIMIT = 56 * 1024 * 1024


def _cparams(sem):
    return pltpu.CompilerParams(dimension_semantics=sem, vmem_limit_bytes=VMEM_LIMIT)


def _dot(a, b):
    return jnp.dot(a, b, preferred_element_type=F32)


def _dot_nt(a, b):
    return lax.dot_general(a, b, (((1,), (1,)), ((), ())), preferred_element_type=F32)


def _bmm(a, b):
    return lax.dot_general(a, b, (((2,), (1,)), ((0,), (0,))), preferred_element_type=F32)


def _bmm_nt(a, b):
    return lax.dot_general(a, b, (((2,), (2,)), ((0,), (0,))), preferred_element_type=F32)


def _bmm_tn(a, b):
    return lax.dot_general(a, b, (((1,), (1,)), ((0,), (0,))), preferred_element_type=F32)


def _split2(x):
    hi = x.astype(BF16)
    lo = (x - hi.astype(F32)).astype(BF16)
    return hi, lo


def _iota(shape, axis):
    return lax.broadcasted_iota(I32, shape, axis)


def _ada_kernel(c_ref, w_ref, b_ref, o_ref):
    c = c_ref[...]
    a = c * jax.nn.sigmoid(c)
    o_ref[...] = _dot(a.astype(BF16), w_ref[...].astype(BF16)) + b_ref[...]


def _ada_mod(cstack, ada_w, ada_b):
    m, d = cstack.shape
    n = ada_w.shape[1]
    tn = 1024
    return pl.pallas_call(
        _ada_kernel,
        grid=(n // tn,),
        in_specs=[pl.BlockSpec((m, d), lambda j: (0, 0)),
                  pl.BlockSpec((d, tn), lambda j: (0, j)),
                  pl.BlockSpec((1, tn), lambda j: (0, j))],
        out_specs=pl.BlockSpec((m, tn), lambda j: (0, j)),
        out_shape=jax.ShapeDtypeStruct((m, n), F32),
        compiler_params=_cparams(("parallel",)),
        name="ada_mod",
    )(cstack, ada_w, ada_b.reshape(1, n))


def _norm_mod_kernel(x_ref, g_ref, sc_ref, sh_ref, o_ref):
    x = x_ref[0]
    ms = jnp.mean(x * x, axis=-1, keepdims=True)
    y = x * lax.rsqrt(ms + NORM_EPS) * g_ref[...]
    o_ref[0] = (y * (1.0 + sc_ref[0]) + sh_ref[0]).astype(o_ref.dtype)


def _norm_mod(x, g, sc, sh, tt):
    b, t, d = x.shape
    return pl.pallas_call(
        _norm_mod_kernel,
        grid=(b, t // tt),
        in_specs=[pl.BlockSpec((1, tt, d), lambda i, j: (i, j, 0)),
                  pl.BlockSpec((1, d), lambda i, j: (0, 0)),
                  pl.BlockSpec((1, 1, d), lambda i, j: (i, 0, 0)),
                  pl.BlockSpec((1, 1, d), lambda i, j: (i, 0, 0))],
        out_specs=pl.BlockSpec((1, tt, d), lambda i, j: (i, j, 0)),
        out_shape=jax.ShapeDtypeStruct((b, t, d), BF16),
        compiler_params=_cparams(("parallel", "parallel")),
        name="norm_mod",
    )(x, g.reshape(1, d), sc, sh)


def _mm_kernel(a_ref, b_ref, o_ref):
    o_ref[...] = _dot(a_ref[...], b_ref[...]).astype(o_ref.dtype)


def _matmul(a, b, out_dtype, tm, tn):
    m, k = a.shape
    n = b.shape[1]
    tm = min(tm, m)
    return pl.pallas_call(
        _mm_kernel,
        grid=(m // tm, n // tn),
        in_specs=[pl.BlockSpec((tm, k), lambda i, j: (i, 0)),
                  pl.BlockSpec((k, tn), lambda i, j: (0, j))],
        out_specs=pl.BlockSpec((tm, tn), lambda i, j: (i, j)),
        out_shape=jax.ShapeDtypeStruct((m, n), out_dtype),
        compiler_params=_cparams(("parallel", "parallel")),
        name="matmul",
    )(a, b)


def _shift_grid(x, prev, nxt, first, last):
    tt, c = x.shape
    col = _iota((tt, c), 0) & (GRID_W - 1)
    m = _iota((tt, c), 1) & 3
    left = jnp.where(col == 0, 0.0, pltpu.roll(x, 1, 0))
    right = jnp.where(col == GRID_W - 1, 0.0, pltpu.roll(x, tt - 1, 0))
    prev = jnp.where(first, 0.0, prev)
    nxt = jnp.where(last, 0.0, nxt)
    if tt > GRID_W:
        up = jnp.concatenate([prev, x[:tt - GRID_W]], axis=0)
        down = jnp.concatenate([x[GRID_W:], nxt], axis=0)
    else:
        up, down = prev, nxt
    return jnp.where(m == 0, left, jnp.where(m == 1, right, jnp.where(m == 2, up, down)))


def _shift_seq(x, prev8, next8, first, last):
    t, c = x.shape
    row = _iota((t, c), 0)
    odd = (_iota((t, c), 1) & 1) == 1
    before = jnp.where(first, 0.0, prev8[7:8])
    after = jnp.where(last, 0.0, next8[0:1])
    prev = jnp.where(row == 0, before, pltpu.roll(x, 1, 0))
    nxt = jnp.where(row == t - 1, after, pltpu.roll(x, t - 1, 0))
    return jnp.where(odd, nxt, prev)


def _head_sum(x):
    w = 2 * LANES
    ones = (_iota((w, w), 0) >> 6 == _iota((w, w), 1) >> 6).astype(BF16)
    outs = []
    for c in range(x.shape[1] // w):
        hi, lo = _split2(x[:, c * w:(c + 1) * w])
        outs.append(_dot(hi, ones) + _dot(lo, ones))
    return jnp.concatenate(outs, axis=1)


def _prepare_kernel(*refs, grid_mode):
    (r_ref, rp_ref, rn_ref, k_ref, kp_ref, kn_ref, v_ref, vp_ref, vn_ref,
     l_ref, lp_ref, ln_ref) = refs[:12]
    rest = refs[12:]
    (mur_ref, muk_ref, muv_ref, mul_ref, w2f_ref, w2b_ref, a2f_ref, a2b_ref,
     w0f_ref, w0b_ref, a0f_ref, a0b_ref, kk_ref, ka_ref, rk_ref,
     oaf_ref, obf_ref, okf_ref, orf_ref, owf_ref, oab_ref, obb_ref, okb_ref, orb_ref, owb_ref,
     ov_ref, obon_ref, ogd_ref) = rest

    first = pl.program_id(1) == 0
    last = pl.program_id(1) == pl.num_programs(1) - 1
    shift = _shift_grid if grid_mode else _shift_seq

    def mix(x_ref, p_ref, n_ref, mu_ref):
        x = x_ref[0]
        return x + (shift(x, p_ref[0], n_ref[0], first, last) - x) * mu_ref[...]

    r = mix(r_ref, rp_ref, rn_ref, mur_ref)
    k = mix(k_ref, kp_ref, kn_ref, muk_ref)
    v = mix(v_ref, vp_ref, vn_ref, muv_ref)
    lo = mix(l_ref, lp_ref, ln_ref, mul_ref)

    th = jnp.tanh(lo[:, :2 * LANES]).astype(BF16)
    zf = w0f_ref[...] + _dot(th[:, :LANES], w2f_ref[...])
    zb = w0b_ref[...] + _dot(th[:, LANES:], w2b_ref[...])
    lwf = -EXP_M05 * jax.nn.sigmoid(zf)
    lwb = -EXP_M05 * jax.nn.sigmoid(zb)
    ad = lo[:, 2 * LANES:4 * LANES].astype(BF16)
    af = jax.nn.sigmoid(a0f_ref[...] + _dot(ad[:, :LANES], a2f_ref[...]))
    ab = jax.nn.sigmoid(a0b_ref[...] + _dot(ad[:, LANES:], a2b_ref[...]))
    ogd_ref[0] = lo[:, 4 * LANES:]

    kk = k * kk_ref[...]
    kk = kk * lax.rsqrt(_head_sum(kk * kk) + 1e-12)
    ka = ka_ref[...]
    kf = k * (1.0 + (af - 1.0) * ka)
    kb = k * (1.0 + (ab - 1.0) * ka)
    bonus = _head_sum(r * (kf + kb) * rk_ref[...]) * v

    tt = r.shape[0]
    t2 = _iota((tt, tt), 0)
    s2 = _iota((tt, tt), 1)
    same = (t2 >> 6) == (s2 >> 6)

    def scan_operands(lw, b, kd, reverse):
        tri = (same & ((s2 >= t2) if reverse else (s2 <= t2))).astype(BF16)
        hi = lw.astype(BF16)
        r1 = lw - hi.astype(F32)
        mid = r1.astype(BF16)
        lo3 = (r1 - mid.astype(F32)).astype(BF16)
        cum = _dot(tri, hi) + _dot(tri, mid) + _dot(tri, lo3)
        w_inc = jnp.exp(cum)
        w_inv = jnp.exp(-cum)
        ends = [w_inc[c * CHUNK:c * CHUNK + 1] if reverse else w_inc[(c + 1) * CHUNK - 1:(c + 1) * CHUNK]
                for c in range(tt // CHUNK)]
        w_last = jnp.concatenate(ends + [jnp.zeros((8 - len(ends), lw.shape[1]), F32)], axis=0)
        return jnp.exp(cum - lw) * kk, b * w_inv, kd * w_inv, w_inc * r, w_last

    al_f, be_f, ka_f, rh_f, wl_f = scan_operands(lwf, kk * af, kf, False)
    al_b, be_b, ka_b, rh_b, wl_b = scan_operands(lwb, kk * ab, kb, True)
    outs = ((oaf_ref, al_f), (obf_ref, be_f), (okf_ref, ka_f), (orf_ref, rh_f), (owf_ref, wl_f),
            (oab_ref, al_b), (obb_ref, be_b), (okb_ref, ka_b), (orb_ref, rh_b), (owb_ref, wl_b),
            (ov_ref, v), (obon_ref, bonus))
    for o_ref, val in outs:
        for p in range(N_PAIRS):
            o_ref[0, p] = val[:, p * LANES:(p + 1) * LANES].astype(o_ref.dtype)


def _prepare(rkv, lora, pw, grid_mode):
    b, t, _ = rkv.shape
    d = D_ATT
    tt = 2 * GRID_W
    halo = GRID_W if grid_mode else 8
    hpt = tt // halo
    nhalo = t // halo
    grid = (b, t // tt)

    def tile_specs(w, c):
        return [pl.BlockSpec((1, tt, w), lambda i, j: (i, j, c)),
                pl.BlockSpec((1, halo, w), lambda i, j: (i, jnp.maximum(j * hpt - 1, 0), c)),
                pl.BlockSpec((1, halo, w), lambda i, j: (i, jnp.minimum((j + 1) * hpt, nhalo - 1), c))]

    in_specs = []
    args = []
    for c in range(3):
        in_specs += tile_specs(d, c)
        args += [rkv, rkv, rkv]
    in_specs += tile_specs(D_LORA_PAD, 0)
    args += [lora, lora, lora]

    def vec(c=0, w=d):
        return pl.BlockSpec((1, w), lambda i, j, c=c: (0, c))

    def full(shape):
        return pl.BlockSpec(shape, lambda i, j: (0,) * len(shape))

    in_specs += [vec(0), vec(1), vec(2), vec(0, D_LORA_PAD)]
    args += [pw["mu_rkv"], pw["mu_rkv"], pw["mu_rkv"], pw["mu_lora"]]
    in_specs += [full((LANES, d))] * 4
    args += [pw["w2f"], pw["w2b"], pw["a2f"], pw["a2b"]]
    in_specs += [vec()] * 7
    args += [pw["w0f"], pw["w0b"], pw["a0f"], pw["a0b"], pw["k_k"], pw["k_a"], pw["r_k"]]

    pair_spec = pl.BlockSpec((1, N_PAIRS, tt, LANES), lambda i, j: (i, 0, j, 0))
    wl_spec = pl.BlockSpec((1, N_PAIRS, 8, LANES), lambda i, j: (i, 0, j, 0))

    def pair(dtype):
        return jax.ShapeDtypeStruct((b, N_PAIRS, t, LANES), dtype)

    wl = jax.ShapeDtypeStruct((b, N_PAIRS, t // tt * 8, LANES), F32)
    direction = [pair(BF16)] * 4 + [wl]
    out_shape = direction * 2 + [pair(BF16), pair(F32), jax.ShapeDtypeStruct((b, t, D_GATE_LORA), F32)]
    out_specs = ([pair_spec] * 4 + [wl_spec]) * 2 + [pair_spec, pair_spec,
                                                    pl.BlockSpec((1, tt, D_GATE_LORA), lambda i, j: (i, j, 0))]
    return pl.pallas_call(
        functools.partial(_prepare_kernel, grid_mode=grid_mode),
        grid=grid, in_specs=in_specs, out_specs=out_specs, out_shape=out_shape,
        compiler_params=_cparams(("parallel", "parallel")),
        name="prepare_grid" if grid_mode else "prepare_seq",
    )(*args)


def _scan_kernel(*refs, reverse, pairs, tb, emit):
    al_ref, be_ref, ka_ref, rh_ref, wl_ref, v_ref, s0_ref = refs[:7]
    if emit:
        y_ref, st_ref, s_scr = refs[7:]
    else:
        st_ref, s_scr = refs[7:]
    L = CHUNK
    n_chunks = tb // L

    @pl.when(pl.program_id(1) == 0)
    def _():
        s_scr[...] = s0_ref[...]

    t2 = _iota((2 * L, 2 * L), 0)
    s2 = _iota((2 * L, 2 * L), 1)
    same = (t2 >> 6) == (s2 >> 6)
    tl = t2 & (L - 1)
    sl = s2 & (L - 1)
    strict = same & ((sl > tl) if reverse else (sl < tl))
    incl = same & ((sl >= tl) if reverse else (sl <= tl))
    eye = (t2 == s2).astype(F32)
    head_a = _iota((pairs, L, LANES), 2) < HEAD
    zero = jnp.zeros((), BF16)

    def stack(ref, rows):
        x = ref[:, rows, :]
        return jnp.concatenate([jnp.where(head_a, x, zero), jnp.where(head_a, zero, x)], axis=1)

    def chunk(ci, carry):
        cc = (n_chunks - 1 - ci) if reverse else ci
        rows = pl.ds(pl.multiple_of(cc * L, L), L)
        a_s = stack(al_ref, rows)
        b_s = stack(be_ref, rows)
        k_s = stack(ka_ref, rows)
        v_s = stack(v_ref, rows)
        lhs = jnp.concatenate([a_s, stack(rh_ref, rows)], axis=1) if emit else a_s
        xb = _bmm_nt(lhs, b_s)
        xk = _bmm_nt(lhs, k_s)
        m1 = jnp.where(strict, xb[:, :2 * L], 0.0)
        m2 = jnp.where(strict, xk[:, :2 * L], 0.0)
        xp = -m1
        tinv = eye + xp
        for _ in range(5):
            xq = xp.astype(BF16)
            xp = _bmm(xq, xq)
            tinv = tinv + _bmm(tinv.astype(BF16), xp.astype(BF16))
        s = s_scr[...]
        xs = _bmm_nt(lhs, s.astype(BF16))
        if emit:
            n1 = jnp.where(incl, xb[:, 2 * L:], 0.0)
            n2 = jnp.where(incl, xk[:, 2 * L:], 0.0)
            xv = _bmm(jnp.concatenate([m2, n2], axis=1).astype(BF16), v_s)
        else:
            xv = _bmm(m2.astype(BF16), v_s)
        g = xs[:, :2 * L] + xv[:, :2 * L]
        u_s = (-_bmm(tinv.astype(BF16), g.astype(BF16))).astype(BF16)
        if emit:
            y2 = xs[:, 2 * L:] + xv[:, 2 * L:] + _bmm(n1.astype(BF16), u_s)
            y_ref[:, rows, :] = y2[:, :L] + y2[:, L:]
        w_last = wl_ref[:, pl.ds((cc >> 1) * 8 + (cc & 1), 1), :]
        upd = _bmm_tn(jnp.concatenate([u_s, v_s], axis=1), jnp.concatenate([b_s, k_s], axis=1))
        s_scr[...] = (s + upd) * w_last
        return carry

    lax.fori_loop(0, n_chunks, chunk, 0)

    @pl.when(pl.program_id(1) == pl.num_programs(1) - 1)
    def _():
        st_ref[...] = s_scr[...]


def _scan(al, be, ka, rh, wl, v, s0, reverse, emit, pairs=16, tb=256):
    bp, t, _ = al.shape
    tb = min(t, tb)
    nb = t // tb

    def tmap(g, c):
        return (g, (nb - 1 - c) if reverse else c, 0)

    data = pl.BlockSpec((pairs, tb, LANES), tmap)
    wl_spec = pl.BlockSpec((pairs, tb // (2 * CHUNK) * 8, LANES), tmap)
    state = pl.BlockSpec((pairs, LANES, LANES), lambda g, c: (g, 0, 0))
    out_shape = [jax.ShapeDtypeStruct((bp, LANES, LANES), F32)]
    out_specs = [state]
    if emit:
        out_shape = [jax.ShapeDtypeStruct((bp, t, LANES), F32)] + out_shape
        out_specs = [data] + out_specs
    res = pl.pallas_call(
        functools.partial(_scan_kernel, reverse=reverse, pairs=pairs, tb=tb, emit=emit),
        grid=(bp // pairs, nb),
        in_specs=[data] * 4 + [wl_spec, data, state],
        out_specs=out_specs, out_shape=out_shape,
        scratch_shapes=[pltpu.VMEM((pairs, LANES, LANES), F32)],
        compiler_params=_cparams(("parallel", "arbitrary")),
        name="scan_" + ("bwd" if reverse else "fwd") + ("_emit" if emit else "_state"),
    )(al, be, ka, rh, wl, v, s0)
    return (res[0], res[1]) if emit else (None, res[0])


def _readout_kernel(yf_ref, yb_ref, bon_ref, gd_ref, lnw_ref, lnb_ref, g2_ref, w_ref, o_ref):
    y = jnp.concatenate([yf_ref[0, p] + yb_ref[0, p] for p in range(N_PAIRS)], axis=1)
    bonus = jnp.concatenate([bon_ref[0, p] for p in range(N_PAIRS)], axis=1)
    mean = _head_sum(y) * (1.0 / HEAD)
    dlt = y - mean
    var = _head_sum(dlt * dlt) * (1.0 / HEAD)
    yn = dlt * lax.rsqrt(var + GN_EPS) * lnw_ref[...] + lnb_ref[...]
    gate = _dot(jax.nn.sigmoid(gd_ref[0]).astype(BF16), g2_ref[...])
    out = ((yn + bonus) * gate).astype(BF16)
    o_ref[0] = _dot(out, w_ref[...]).astype(o_ref.dtype)


def _readout(yf, yb, bonus, gd, lnw, lnb, g2, w_out):
    b, _, t, _ = yf.shape
    tt = 256
    d = D_ATT
    pair_spec = pl.BlockSpec((1, N_PAIRS, tt, LANES), lambda i, j: (i, 0, j, 0))
    return pl.pallas_call(
        _readout_kernel,
        grid=(b, t // tt),
        in_specs=[pair_spec, pair_spec, pair_spec,
                  pl.BlockSpec((1, tt, D_GATE_LORA), lambda i, j: (i, j, 0)),
                  pl.BlockSpec((1, d), lambda i, j: (0, 0)),
                  pl.BlockSpec((1, d), lambda i, j: (0, 0)),
                  pl.BlockSpec((D_GATE_LORA, d), lambda i, j: (0, 0)),
                  pl.BlockSpec((d, D_MODEL), lambda i, j: (0, 0))],
        out_specs=pl.BlockSpec((1, tt, D_MODEL), lambda i, j: (i, j, 0)),
        out_shape=jax.ShapeDtypeStruct((b, t, D_MODEL), BF16),
        compiler_params=_cparams(("parallel", "parallel")),
        name="readout",
    )(yf, yb, bonus, gd, lnw, lnb, g2, w_out)


def _pool_kernel(u_ref, pw_ref, ps_ref, wo_ref, o_ref):
    u = u_ref[0]
    tt = u.shape[0]
    t2 = _iota((tt, tt), 0)
    s2 = _iota((tt, tt), 1)
    same = (t2 >> 6) == (s2 >> 6)
    tc = t2 & (GRID_W - 1)
    sc = s2 & (GRID_W - 1)
    col = _iota((tt, POOL_GROUP), 0) & (GRID_W - 1)
    ys = []
    for gi, w in enumerate(POOL_WINDOWS):
        ug = u[:, gi * POOL_GROUP:(gi + 1) * POOL_GROUP]
        win = (same & (sc >= tc - w // 2) & (sc < tc + (w - w // 2))).astype(BF16)
        hi, lo = _split2(ug)
        wsum = _dot(win, hi) + _dot(win, lo)
        cnt = (jnp.minimum(col + (w - w // 2), GRID_W) - jnp.maximum(col - w // 2, 0)).astype(F32)
        dlt = wsum / cnt - ug
        ys.append(_dot(dlt.astype(BF16), pw_ref[gi]))
    y1 = jnp.concatenate(ys, axis=1) * ps_ref[...]
    o_ref[0] = _dot(y1.astype(BF16), wo_ref[...]).astype(o_ref.dtype)


def _pool_branch(u, pool_w, pool_scale, w_pool_out):
    b, t, _ = u.shape
    tt = 256
    return pl.pallas_call(
        _pool_kernel,
        grid=(b, t // tt),
        in_specs=[pl.BlockSpec((1, tt, D_POOL), lambda i, j: (i, j, 0)),
                  pl.BlockSpec((4, POOL_GROUP, POOL_GROUP), lambda i, j: (0, 0, 0)),
                  pl.BlockSpec((1, D_POOL), lambda i, j: (0, 0)),
                  pl.BlockSpec((D_POOL, D_MODEL), lambda i, j: (0, 0))],
        out_specs=pl.BlockSpec((1, tt, D_MODEL), lambda i, j: (i, j, 0)),
        out_shape=jax.ShapeDtypeStruct((b, t, D_MODEL), BF16),
        compiler_params=_cparams(("parallel", "parallel")),
        name="pool_branch",
    )(u, pool_w, pool_scale, w_pool_out)


def _merge_kernel(yp_ref, yr_ref, gp_ref, gr_ref, x_ref, gt_ref, sc_ref, sh_ref, g_ref, w_ref, rw_ref,
                  x1_ref, h_ref, lg_ref):
    m = (jax.nn.sigmoid(gp_ref[0].astype(F32)) * yp_ref[0].astype(F32)
         + jax.nn.sigmoid(gr_ref[0].astype(F32)) * yr_ref[0].astype(F32))
    x1 = x_ref[0] + gt_ref[0] * _dot(m.astype(BF16), w_ref[...])
    x1_ref[0] = x1
    ms = jnp.mean(x1 * x1, axis=-1, keepdims=True)
    h = x1 * lax.rsqrt(ms + NORM_EPS) * g_ref[...]
    h = h * (1.0 + sc_ref[0]) + sh_ref[0]
    h_ref[0] = h
    hh, hl = _split2(h)
    rh, rl = _split2(rw_ref[...])
    lg_ref[0] = _dot(hh, rh) + _dot(hl, rh) + _dot(hh, rl)


def _merge(y_pool, y_rwkv, gates, x, gt1, sc2, sh2, g2, w_out, router_w_pad):
    b, t, d = x.shape
    tt = 256
    tile = pl.BlockSpec((1, tt, d), lambda i, j: (i, j, 0))
    mod = pl.BlockSpec((1, 1, d), lambda i, j: (i, 0, 0))
    return pl.pallas_call(
        _merge_kernel,
        grid=(b, t // tt),
        in_specs=[tile, tile,
                  pl.BlockSpec((1, tt, d), lambda i, j: (i, j, 0)),
                  pl.BlockSpec((1, tt, d), lambda i, j: (i, j, 1)),
                  tile, mod, mod, mod,
                  pl.BlockSpec((1, d), lambda i, j: (0, 0)),
                  pl.BlockSpec((d, d), lambda i, j: (0, 0)),
                  pl.BlockSpec((d, LANES), lambda i, j: (0, 0))],
        out_specs=[tile, tile,
                   pl.BlockSpec((1, tt, LANES), lambda i, j: (i, j, 0))],
        out_shape=[jax.ShapeDtypeStruct((b, t, d), F32),
                   jax.ShapeDtypeStruct((b, t, d), F32),
                   jax.ShapeDtypeStruct((b, t, LANES), F32)],
        compiler_params=_cparams(("parallel", "parallel")),
        name="merge",
    )(y_pool, y_rwkv, gates, gates, x, gt1, sc2, sh2, g2.reshape(1, d), w_out, router_w_pad)


def _router_kernel(lg_ref, bias_ref, e_ref, rk_ref, w_ref, cnt_ref, carry):
    tt = lg_ref.shape[0]
    shape = (tt, LANES)
    lane = _iota(shape, 1)
    valid = lane < N_EXPERTS
    grp = (lane & (N_EXPERTS - 1)) >> 3
    neg = jnp.float32(-jnp.inf)

    @pl.when(pl.program_id(0) == 0)
    def _():
        carry[...] = jnp.zeros_like(carry)

    scores = jax.nn.sigmoid(lg_ref[...])
    sel = scores + bias_ref[...]
    sel = jnp.where(valid, sel, pltpu.roll(sel, N_EXPERTS, 1))

    def group_reduce(x, op):
        for sh in (1, 2, 4):
            up = pltpu.roll(x, sh, 1)
            dn = pltpu.roll(x, LANES - sh, 1)
            x = op(x, jnp.where((lane & sh) != 0, up, dn))
        return x

    m1 = group_reduce(sel, jnp.maximum)
    first = group_reduce(jnp.where(sel == m1, lane, LANES), jnp.minimum)
    m2 = group_reduce(jnp.where(lane == first, neg, sel), jnp.maximum)
    gs = m1 + m2
    beaten = jnp.zeros(shape, I32)
    for k in range(1, N_GROUPS):
        other = pltpu.roll(gs, 8 * k, 1)
        og = (grp - k) & (N_GROUPS - 1)
        beaten = beaten + ((other > gs) | ((other == gs) & (og < grp))).astype(I32)
    cur = jnp.where((beaten < TOPK_GROUPS) & valid, sel, neg)

    picked = jnp.zeros(shape, jnp.bool_)
    e_acc = jnp.zeros(shape, I32)
    w_acc = jnp.zeros(shape, F32)
    idxs = []
    for k in range(TOP_K):
        m = jnp.max(cur, axis=1, keepdims=True)
        idx = jnp.min(jnp.where(cur == m, lane, LANES), axis=1, keepdims=True)
        oh = lane == idx
        sc = jnp.sum(jnp.where(oh, scores, 0.0), axis=1, keepdims=True)
        e_acc = jnp.where(lane == k, idx, e_acc)
        w_acc = jnp.where(lane == k, sc, w_acc)
        picked = picked | oh
        cur = jnp.where(oh, neg, cur)
        idxs.append(idx)
    wsum = jnp.sum(w_acc, axis=1, keepdims=True)
    w_ref[...] = w_acc / wsum * ROUTED_SCALE
    e_ref[...] = e_acc

    lower = (_iota((tt, tt), 1) < _iota((tt, tt), 0)).astype(BF16)
    pk = picked.astype(BF16)
    before = _dot(lower, pk) + carry[...]
    r_acc = jnp.zeros(shape, F32)
    for k in range(TOP_K):
        rk = jnp.sum(jnp.where(lane == idxs[k], before, 0.0), axis=1, keepdims=True)
        r_acc = jnp.where(lane == k, rk, r_acc)
    rk_ref[...] = r_acc.astype(I32)
    carry[...] = carry[...] + jnp.sum(picked.astype(F32), axis=0, keepdims=True)
    cnt_ref[...] = carry[...]


def _router(logits, bias_pad):
    n = logits.shape[0]
    tt = min(1024, n)
    tile = pl.BlockSpec((tt, LANES), lambda i: (i, 0))
    row = pl.BlockSpec((1, LANES), lambda i: (0, 0))
    return pl.pallas_call(
        _router_kernel,
        grid=(n // tt,),
        in_specs=[tile, row],
        out_specs=[tile, tile, tile, row],
        out_shape=[jax.ShapeDtypeStruct((n, LANES), I32), jax.ShapeDtypeStruct((n, LANES), I32),
                   jax.ShapeDtypeStruct((n, LANES), F32), jax.ShapeDtypeStruct((1, LANES), F32)],
        scratch_shapes=[pltpu.VMEM((1, LANES), F32)],
        compiler_params=_cparams(("arbitrary",)),
        name="router",
    )(logits, bias_pad)


def _dispatch_kernel(e_ref, rk_ref, ps_ref, cnt_ref, h_ref, xs_ref, zbuf, sem, zsem, *, tt):
    bm = EXPERT_BLOCK

    @pl.when(pl.program_id(0) == 0)
    def _():
        zbuf[...] = jnp.zeros_like(zbuf)

        def tail_copy(e):
            last = ps_ref[e] + ((cnt_ref[e] + bm - 1) & -bm) - bm
            return pltpu.make_async_copy(zbuf, xs_ref.at[pl.ds(pl.multiple_of(last, bm), bm)], zsem)

        def tail_start(e, c):
            @pl.when((cnt_ref[e] & (bm - 1)) != 0)
            def _():
                tail_copy(e).start()
            return c

        def tail_wait(e, c):
            @pl.when((cnt_ref[e] & (bm - 1)) != 0)
            def _():
                tail_copy(e).wait()
            return c

        lax.fori_loop(0, N_EXPERTS, tail_start, 0)
        lax.fori_loop(0, N_EXPERTS, tail_wait, 0)

    def row_copy(src_row, dst_row):
        return pltpu.make_async_copy(h_ref.at[pl.ds(src_row, 1)], xs_ref.at[pl.ds(dst_row, 1)], sem)

    def start(t, c):
        for k in range(TOP_K):
            j = t * TOP_K + k
            row_copy(t, ps_ref[e_ref[j]] + rk_ref[j]).start(priority=k % 2)
        return c

    lax.fori_loop(0, tt, start, 0)

    for _ in range(TOP_K):
        pltpu.make_async_copy(h_ref, xs_ref.at[pl.ds(0, tt)], sem).wait()


def _dispatch(e_flat, rk_flat, pstarts, counts, h_rows, cap):
    n, d = h_rows.shape
    tt = min(512, n)
    smem_blk = pl.BlockSpec((tt * TOP_K,), lambda i: (i,), memory_space=pltpu.SMEM)
    return pl.pallas_call(
        functools.partial(_dispatch_kernel, tt=tt),
        grid=(n // tt,),
        in_specs=[smem_blk, smem_blk,
                  pl.BlockSpec(memory_space=pltpu.SMEM),
                  pl.BlockSpec(memory_space=pltpu.SMEM),
                  pl.BlockSpec((tt, d), lambda i: (i, 0))],
        out_specs=pl.BlockSpec(memory_space=pl.ANY),
        out_shape=jax.ShapeDtypeStruct((cap, d), h_rows.dtype),
        scratch_shapes=[pltpu.VMEM((EXPERT_BLOCK, d), h_rows.dtype),
                        pltpu.SemaphoreType.DMA(()), pltpu.SemaphoreType.DMA(())],
        compiler_params=_cparams(("arbitrary",)),
        name="dispatch",
    )(e_flat, rk_flat, pstarts, counts, h_rows)


def _expert_kernel(be_ref, nu_ref, x_ref, wg_ref, wu_ref, wd_ref, o_ref):
    del be_ref
    used = pl.program_id(0) < nu_ref[0]

    @pl.when(used)
    def _():
        xb = x_ref[...].astype(BF16)
        g = _dot(xb, wg_ref[0].astype(BF16))
        u = _dot(xb, wu_ref[0].astype(BF16))
        act = (g * jax.nn.sigmoid(g) * u).astype(BF16)
        o_ref[...] = _dot(act, wd_ref[0].astype(BF16))

    @pl.when(jnp.logical_not(used))
    def _():
        o_ref[...] = jnp.zeros_like(o_ref)


def _experts(block_expert, n_used, xs, wg, wu, wd):
    cap, d = xs.shape
    bm = EXPERT_BLOCK
    de = wg.shape[2]
    grid_spec = pltpu.PrefetchScalarGridSpec(
        num_scalar_prefetch=2,
        grid=(cap // bm,),
        in_specs=[pl.BlockSpec((bm, d), lambda i, be, nu: (jnp.minimum(i, nu[0] - 1), 0)),
                  pl.BlockSpec((1, D_MODEL, de), lambda i, be, nu: (be[i], 0, 0)),
                  pl.BlockSpec((1, D_MODEL, de), lambda i, be, nu: (be[i], 0, 0)),
                  pl.BlockSpec((1, de, D_MODEL), lambda i, be, nu: (be[i], 0, 0))],
        out_specs=pl.BlockSpec((bm, d), lambda i, be, nu: (i, 0)),
    )
    return pl.pallas_call(
        _expert_kernel,
        grid_spec=grid_spec,
        out_shape=jax.ShapeDtypeStruct((cap, d), F32),
        compiler_params=_cparams(("arbitrary",)),
        name="experts",
    )(block_expert, n_used, xs, wg, wu, wd)


def _combine_kernel(e_ref, rk_ref, en_ref, rkn_ref, ps_ref, ys_ref, w_ref, sh_ref, x_ref, gt_ref, g_ref, o_ref,
                    buf, sem, *, tt):
    step = pl.program_id(0) * pl.num_programs(1) + pl.program_id(1)
    n_steps = pl.num_programs(0) * pl.num_programs(1)
    slot = step & 1

    def gather(idx_ref, rank_ref, dst_slot):
        def start(t, c):
            for k in range(TOP_K):
                j = t * TOP_K + k
                pltpu.make_async_copy(ys_ref.at[pl.ds(ps_ref[idx_ref[j]] + rank_ref[j], 1)],
                                      buf.at[dst_slot, k, pl.ds(t, 1)], sem.at[dst_slot]).start(priority=k % 2)
            return c

        lax.fori_loop(0, tt, start, 0)

    @pl.when(step == 0)
    def _():
        gather(e_ref, rk_ref, 0)

    @pl.when(step + 1 < n_steps)
    def _():
        gather(en_ref, rkn_ref, 1 - slot)

    for k in range(TOP_K):
        pltpu.make_async_copy(ys_ref.at[pl.ds(0, tt)], buf.at[slot, k], sem.at[slot]).wait()

    w = w_ref[0]
    moe = sh_ref[0]
    for k in range(TOP_K):
        moe = moe + w[:, k:k + 1] * buf[slot, k]
    x2 = x_ref[0] + gt_ref[0] * moe
    ms = jnp.mean(x2 * x2, axis=-1, keepdims=True)
    o_ref[0] = x2 * lax.rsqrt(ms + NORM_EPS) * g_ref[...]


def _combine(e_flat, rk_flat, pstarts, ys, wsel, shared, x1, gt2, final_g):
    b, t, d = x1.shape
    tt = 128
    nt = t // tt
    last = b * nt - 1
    smem_blk = pl.BlockSpec((tt * TOP_K,), lambda i, j: (i * nt + j,), memory_space=pltpu.SMEM)
    smem_nxt = pl.BlockSpec((tt * TOP_K,), lambda i, j: (jnp.minimum(i * nt + j + 1, last),),
                            memory_space=pltpu.SMEM)
    return pl.pallas_call(
        functools.partial(_combine_kernel, tt=tt),
        grid=(b, nt),
        in_specs=[smem_blk, smem_blk, smem_nxt, smem_nxt,
                  pl.BlockSpec(memory_space=pltpu.SMEM),
                  pl.BlockSpec(memory_space=pl.ANY),
                  pl.BlockSpec((1, tt, LANES), lambda i, j: (i, j, 0)),
                  pl.BlockSpec((1, tt, d), lambda i, j: (i, j, 0)),
                  pl.BlockSpec((1, tt, d), lambda i, j: (i, j, 0)),
                  pl.BlockSpec((1, 1, d), lambda i, j: (i, 0, 0)),
                  pl.BlockSpec((1, d), lambda i, j: (0, 0))],
        out_specs=pl.BlockSpec((1, tt, d), lambda i, j: (i, j, 0)),
        out_shape=jax.ShapeDtypeStruct((b, t, d), F32),
        scratch_shapes=[pltpu.VMEM((2, TOP_K, tt, d), F32), pltpu.SemaphoreType.DMA((2,))],
        compiler_params=_cparams(("arbitrary", "arbitrary")),
        name="combine",
    )(e_flat, rk_flat, e_flat, rk_flat, pstarts, ys, wsel, shared, x1, gt2, final_g.reshape(1, d))


def _pad_rows(w, rows):
    return jnp.zeros((rows, w.shape[1]), w.dtype).at[:w.shape[0]].set(w)


def _lora_pad_cols(w):
    out = jnp.zeros(w.shape[:-1] + (D_LORA_PAD,), w.dtype)
    for i in range(4):
        out = out.at[..., i * LANES:i * LANES + D_LORA].set(w[..., i * D_LORA:(i + 1) * D_LORA])
    return out.at[..., 4 * LANES:].set(w[..., 4 * D_LORA:])


def kernel(x, c, ctx, c_ctx, norm1_g, norm2_g, ada_w, ada_b, w_in, shift_mu, pool_w, pool_scale, w_pool_out, decay_w0, decay_w2, iclr_a0, iclr_a2, gate_g2, k_k, k_a, r_k, lnx_w, lnx_b, w_rwkv_out, w_out, router_w, router_bias, exp_w_gate, exp_w_up, exp_w_down, shared_w_gate, shared_w_up, shared_w_down, final_g):
    B, T, D = x.shape
    TC = ctx.shape[1]
    n = B * T
    l = 0

    w_in_l = w_in[l]
    w_u = w_in_l[:, :D_POOL].astype(BF16)
    w_rkv = w_in_l[:, D_POOL:D_POOL + 3 * D_ATT].astype(BF16)
    w_lora = _lora_pad_cols(w_in_l[:, D_POOL + 3 * D_ATT:D_POOL + 3 * D_ATT + 4 * D_LORA + D_GATE_LORA]).astype(BF16)
    w_gates = w_in_l[:, D_POOL + 3 * D_ATT + 4 * D_LORA + D_GATE_LORA:].astype(BF16)
    mu = shift_mu[l]
    row = lambda a: a.reshape(1, -1)
    pw = {
        "mu_rkv": row(mu[:3 * D_ATT]),
        "mu_lora": row(_lora_pad_cols(mu[3 * D_ATT:])),
        "w2f": _pad_rows(decay_w2[l, 0], LANES).astype(BF16),
        "w2b": _pad_rows(decay_w2[l, 1], LANES).astype(BF16),
        "a2f": _pad_rows(iclr_a2[l, 0], LANES).astype(BF16),
        "a2b": _pad_rows(iclr_a2[l, 1], LANES).astype(BF16),
        "w0f": row(decay_w0[l, 0]), "w0b": row(decay_w0[l, 1]),
        "a0f": row(iclr_a0[l, 0]), "a0b": row(iclr_a0[l, 1]),
        "k_k": row(k_k[l]), "k_a": row(k_a[l]), "r_k": row(r_k[l]),
    }

    cstack = jnp.zeros((8, D), F32).at[:B].set(c).at[B].set(c_ctx)
    mod = _ada_mod(cstack, ada_w[l], ada_b[l])
    sh1, sc1, gt1, sh2, sc2, gt2 = [mod[:B, i * D:(i + 1) * D].reshape(B, 1, D) for i in range(6)]
    sh1c = jnp.broadcast_to(mod[B, 0:D].reshape(1, 1, D), (B, 1, D))
    sc1c = jnp.broadcast_to(mod[B, D:2 * D].reshape(1, 1, D), (B, 1, D))

    hc = _norm_mod(ctx, norm1_g[l], sc1c, sh1c, TC).reshape(B * TC, D)
    rkv_c = _matmul(hc, w_rkv, F32, 512, 512).reshape(B, TC, 3 * D_ATT)
    lora_c = _matmul(hc, w_lora, F32, 512, D_LORA_PAD).reshape(B, TC, D_LORA_PAD)
    pc = _prepare(rkv_c, lora_c, pw, grid_mode=False)
    flat = lambda a: a.reshape(B * N_PAIRS, a.shape[2], LANES)
    s0 = jnp.zeros((B * N_PAIRS, LANES, LANES), F32)
    vc = flat(pc[10])
    _, st_f = _scan(*[flat(a) for a in pc[0:5]], vc, s0, reverse=False, emit=False)
    _, st_b = _scan(*[flat(a) for a in pc[5:10]], vc, s0, reverse=True, emit=False)

    h = _norm_mod(x, norm1_g[l], sc1, sh1, 512).reshape(n, D)
    u = _matmul(h, w_u, F32, 2048, 512).reshape(B, T, D_POOL)
    rkv = _matmul(h, w_rkv, F32, 2048, 512).reshape(B, T, 3 * D_ATT)
    lora = _matmul(h, w_lora, F32, 2048, D_LORA_PAD).reshape(B, T, D_LORA_PAD)
    gates = _matmul(h, w_gates, BF16, 2048, 512).reshape(B, T, 2 * D)
    pp = _prepare(rkv, lora, pw, grid_mode=True)
    v_, bonus, gd = flat(pp[10]), pp[11], pp[12]
    y_f, _ = _scan(*[flat(a) for a in pp[0:5]], v_, st_f, reverse=False, emit=True)
    y_b, _ = _scan(*[flat(a) for a in pp[5:10]], v_, st_b, reverse=True, emit=True)
    y_rwkv = _readout(y_f.reshape(B, N_PAIRS, T, LANES), y_b.reshape(B, N_PAIRS, T, LANES), bonus, gd,
                      row(lnx_w[l]), row(lnx_b[l]), gate_g2[l].astype(BF16), w_rwkv_out[l].astype(BF16))
    y_pool = _pool_branch(u, pool_w[l].astype(BF16), row(pool_scale[l]), w_pool_out[l].astype(BF16))

    router_w_pad = jnp.zeros((D, LANES), F32).at[:, :N_EXPERTS].set(router_w[l])
    x1, h2, logits = _merge(y_pool, y_rwkv, gates, x, gt1, sc2, sh2, norm2_g[l],
                                w_out[l].astype(BF16), router_w_pad)

    bias_pad = jnp.zeros((1, LANES), F32).at[0, :N_EXPERTS].set(router_bias[l])
    e_idx, e_rank, wsel, counts = _router(logits.reshape(n, LANES), bias_pad)
    e_flat = e_idx[:, :TOP_K].reshape(-1)
    rk_flat = e_rank[:, :TOP_K].reshape(-1)
    bm = EXPERT_BLOCK
    cnt = counts[0, :N_EXPERTS].astype(I32)
    padded = (cnt + bm - 1) // bm * bm
    pend = jnp.cumsum(padded)
    pstarts = (pend - padded).astype(I32)
    cap = n * TOP_K + N_EXPERTS * bm
    n_blocks = cap // bm
    block_start = jnp.arange(n_blocks, dtype=I32) * bm
    block_expert = jnp.minimum(jnp.sum(block_start[:, None] >= pend[None, :], axis=-1), N_EXPERTS - 1).astype(I32)
    n_used = (pend[-1] // bm).astype(I32).reshape(1)

    h2 = h2.reshape(n, D)
    xs = _dispatch(e_flat, rk_flat, pstarts, cnt, h2, cap)
    ys = _experts(block_expert, n_used, xs, exp_w_gate[l], exp_w_up[l], exp_w_down[l])
    shared = _experts(jnp.zeros((n // bm,), I32), jnp.full((1,), n // bm, I32), h2,
                      shared_w_gate[l][None], shared_w_up[l][None], shared_w_down[l][None])
    return _combine(e_flat, rk_flat, pstarts, ys, wsel.reshape(B, T, LANES),
                    shared.reshape(B, T, D), x1, gt2, final_g)
```

```python
import functools

import jax
import jax.numpy as jnp
from jax import lax
from jax.experimental import pallas as pl
from jax.experimental.pallas import tpu as pltpu

F32 = jnp.float32
BF16 = jnp.bfloat16
I32 = jnp.int32
U32 = jnp.uint32

D_MODEL = 2048
GRID_W = 64
POOL_WINDOWS = (2, 4, 8, 16)
POOL_GROUP = 256
D_POOL = 1024
HEAD = 64
N_HEADS = 32
N_PAIRS = N_HEADS // 2
D_ATT = 2048
D_LORA = 96
D_GATE_LORA = 256
D_LORA_PAD = 768
GN_EPS = 64e-5
NORM_EPS = 1e-6
N_EXPERTS = 64
TOP_K = 8
N_GROUPS = 8
TOPK_GROUPS = 4
D_EXPERT = 512
ROUTED_SCALE = 2.5
EXP_M05 = 0.6065306597126334

LANES = 128
CHUNK = 64
EXPERT_BLOCK = 512
VMEM_LIMIT = 56 * 1024 * 1024


def _cparams(sem):
    return pltpu.CompilerParams(dimension_semantics=sem, vmem_limit_bytes=VMEM_LIMIT)


def _dot(a, b):
    return jnp.dot(a, b, preferred_element_type=F32)


def _dot_nt(a, b):
    return lax.dot_general(a, b, (((1,), (1,)), ((), ())), preferred_element_type=F32)


def _bmm(a, b):
    return lax.dot_general(a, b, (((2,), (1,)), ((0,), (0,))), preferred_element_type=F32)


def _bmm_nt(a, b):
    return lax.dot_general(a, b, (((2,), (2,)), ((0,), (0,))), preferred_element_type=F32)


def _bmm_tn(a, b):
    return lax.dot_general(a, b, (((1,), (1,)), ((0,), (0,))), preferred_element_type=F32)


def _split2(x):
    hi = x.astype(BF16)
    lo = (x - hi.astype(F32)).astype(BF16)
    return hi, lo


def _iota(shape, axis):
    return lax.broadcasted_iota(I32, shape, axis)


def _ada_kernel(c_ref, w_ref, b_ref, o_ref):
    c = c_ref[...]
    a = c * jax.nn.sigmoid(c)
    o_ref[...] = _dot(a.astype(BF16), w_ref[...].astype(BF16)) + b_ref[...]


def _ada_mod(cstack, ada_w, ada_b):
    m, d = cstack.shape
    n = ada_w.shape[1]
    tn = 1024
    return pl.pallas_call(
        _ada_kernel,
        grid=(n // tn,),
        in_specs=[pl.BlockSpec((m, d), lambda j: (0, 0)),
                  pl.BlockSpec((d, tn), lambda j: (0, j)),
                  pl.BlockSpec((1, tn), lambda j: (0, j))],
        out_specs=pl.BlockSpec((m, tn), lambda j: (0, j)),
        out_shape=jax.ShapeDtypeStruct((m, n), F32),
        compiler_params=_cparams(("parallel",)),
        name="ada_mod",
    )(cstack, ada_w, ada_b.reshape(1, n))


def _norm_mod_kernel(x_ref, g_ref, sc_ref, sh_ref, o_ref):
    x = x_ref[0]
    ms = jnp.mean(x * x, axis=-1, keepdims=True)
    y = x * lax.rsqrt(ms + NORM_EPS) * g_ref[...]
    o_ref[0] = (y * (1.0 + sc_ref[0]) + sh_ref[0]).astype(o_ref.dtype)


def _norm_mod(x, g, sc, sh, tt):
    b, t, d = x.shape
    return pl.pallas_call(
        _norm_mod_kernel,
        grid=(b, t // tt),
        in_specs=[pl.BlockSpec((1, tt, d), lambda i, j: (i, j, 0)),
                  pl.BlockSpec((1, d), lambda i, j: (0, 0)),
                  pl.BlockSpec((1, 1, d), lambda i, j: (i, 0, 0)),
                  pl.BlockSpec((1, 1, d), lambda i, j: (i, 0, 0))],
        out_specs=pl.BlockSpec((1, tt, d), lambda i, j: (i, j, 0)),
        out_shape=jax.ShapeDtypeStruct((b, t, d), BF16),
        compiler_params=_cparams(("parallel", "parallel")),
        name="norm_mod",
    )(x, g.reshape(1, d), sc, sh)


def _mm_kernel(a_ref, b_ref, o_ref):
    o_ref[...] = _dot(a_ref[...], b_ref[...]).astype(o_ref.dtype)


def _matmul(a, b, out_dtype, tm, tn):
    m, k = a.shape
    n = b.shape[1]
    tm = min(tm, m)
    return pl.pallas_call(
        _mm_kernel,
        grid=(m // tm, n // tn),
        in_specs=[pl.BlockSpec((tm, k), lambda i, j: (i, 0)),
                  pl.BlockSpec((k, tn), lambda i, j: (0, j))],
        out_specs=pl.BlockSpec((tm, tn), lambda i, j: (i, j)),
        out_shape=jax.ShapeDtypeStruct((m, n), out_dtype),
        compiler_params=_cparams(("parallel", "parallel")),
        name="matmul",
    )(a, b)


def _shift_grid(x, prev, nxt, first, last):
    tt, c = x.shape
    col = _iota((tt, c), 0) & (GRID_W - 1)
    m = _iota((tt, c), 1) & 3
    left = jnp.where(col == 0, 0.0, pltpu.roll(x, 1, 0))
    right = jnp.where(col == GRID_W - 1, 0.0, pltpu.roll(x, tt - 1, 0))
    prev = jnp.where(first, 0.0, prev)
    nxt = jnp.where(last, 0.0, nxt)
    if tt > GRID_W:
        up = jnp.concatenate([prev, x[:tt - GRID_W]], axis=0)
        down = jnp.concatenate([x[GRID_W:], nxt], axis=0)
    else:
        up, down = prev, nxt
    return jnp.where(m == 0, left, jnp.where(m == 1, right, jnp.where(m == 2, up, down)))


def _shift_seq(x, prev8, next8, first, last):
    t, c = x.shape
    row = _iota((t, c), 0)
    odd = (_iota((t, c), 1) & 1) == 1
    before = jnp.where(first, 0.0, prev8[7:8])
    after = jnp.where(last, 0.0, next8[0:1])
    prev = jnp.where(row == 0, before, pltpu.roll(x, 1, 0))
    nxt = jnp.where(row == t - 1, after, pltpu.roll(x, t - 1, 0))
    return jnp.where(odd, nxt, prev)


def _head_sum(x):
    w = 2 * LANES
    ones = (_iota((w, w), 0) >> 6 == _iota((w, w), 1) >> 6).astype(BF16)
    outs = []
    for c in range(x.shape[1] // w):
        hi, lo = _split2(x[:, c * w:(c + 1) * w])
        outs.append(_dot(hi, ones) + _dot(lo, ones))
    return jnp.concatenate(outs, axis=1)


def _prepare_kernel(*refs, grid_mode):
    (r_ref, rp_ref, rn_ref, k_ref, kp_ref, kn_ref, v_ref, vp_ref, vn_ref,
     l_ref, lp_ref, ln_ref) = refs[:12]
    rest = refs[12:]
    (mur_ref, muk_ref, muv_ref, mul_ref, w2f_ref, w2b_ref, a2f_ref, a2b_ref,
     w0f_ref, w0b_ref, a0f_ref, a0b_ref, kk_ref, ka_ref, rk_ref,
     oaf_ref, obf_ref, okf_ref, orf_ref, owf_ref, oab_ref, obb_ref, okb_ref, orb_ref, owb_ref,
     ov_ref, obon_ref, ogd_ref) = rest

    first = pl.program_id(1) == 0
    last = pl.program_id(1) == pl.num_programs(1) - 1
    shift = _shift_grid if grid_mode else _shift_seq

    def mix(x_ref, p_ref, n_ref, mu_ref):
        x = x_ref[0]
        return x + (shift(x, p_ref[0], n_ref[0], first, last) - x) * mu_ref[...]

    r = mix(r_ref, rp_ref, rn_ref, mur_ref)
    k = mix(k_ref, kp_ref, kn_ref, muk_ref)
    v = mix(v_ref, vp_ref, vn_ref, muv_ref)
    lo = mix(l_ref, lp_ref, ln_ref, mul_ref)

    th = jnp.tanh(lo[:, :2 * LANES]).astype(BF16)
    zf = w0f_ref[...] + _dot(th[:, :LANES], w2f_ref[...])
    zb = w0b_ref[...] + _dot(th[:, LANES:], w2b_ref[...])
    lwf = -EXP_M05 * jax.nn.sigmoid(zf)
    lwb = -EXP_M05 * jax.nn.sigmoid(zb)
    ad = lo[:, 2 * LANES:4 * LANES].astype(BF16)
    af = jax.nn.sigmoid(a0f_ref[...] + _dot(ad[:, :LANES], a2f_ref[...]))
    ab = jax.nn.sigmoid(a0b_ref[...] + _dot(ad[:, LANES:], a2b_ref[...]))
    ogd_ref[0] = lo[:, 4 * LANES:]

    kk = k * kk_ref[...]
    kk = kk * lax.rsqrt(_head_sum(kk * kk) + 1e-12)
    ka = ka_ref[...]
    kf = k * (1.0 + (af - 1.0) * ka)
    kb = k * (1.0 + (ab - 1.0) * ka)
    bonus = _head_sum(r * (kf + kb) * rk_ref[...]) * v

    tt = r.shape[0]
    t2 = _iota((tt, tt), 0)
    s2 = _iota((tt, tt), 1)
    same = (t2 >> 6) == (s2 >> 6)

    def scan_operands(lw, b, kd, reverse):
        tri = (same & ((s2 >= t2) if reverse else (s2 <= t2))).astype(BF16)
        hi = lw.astype(BF16)
        r1 = lw - hi.astype(F32)
        mid = r1.astype(BF16)
        lo3 = (r1 - mid.astype(F32)).astype(BF16)
        cum = _dot(tri, hi) + _dot(tri, mid) + _dot(tri, lo3)
        w_inc = jnp.exp(cum)
        w_inv = jnp.exp(-cum)
        ends = [w_inc[c * CHUNK:c * CHUNK + 1] if reverse else w_inc[(c + 1) * CHUNK - 1:(c + 1) * CHUNK]
                for c in range(tt // CHUNK)]
        w_last = jnp.concatenate(ends + [jnp.zeros((8 - len(ends), lw.shape[1]), F32)], axis=0)
        return jnp.exp(cum - lw) * kk, b * w_inv, kd * w_inv, w_inc * r, w_last

    al_f, be_f, ka_f, rh_f, wl_f = scan_operands(lwf, kk * af, kf, False)
    al_b, be_b, ka_b, rh_b, wl_b = scan_operands(lwb, kk * ab, kb, True)
    outs = ((oaf_ref, al_f), (obf_ref, be_f), (okf_ref, ka_f), (orf_ref, rh_f), (owf_ref, wl_f),
            (oab_ref, al_b), (obb_ref, be_b), (okb_ref, ka_b), (orb_ref, rh_b), (owb_ref, wl_b),
            (ov_ref, v), (obon_ref, bonus))
    for o_ref, val in outs:
        for p in range(N_PAIRS):
            o_ref[0, p] = val[:, p * LANES:(p + 1) * LANES].astype(o_ref.dtype)


def _prepare(rkv, lora, pw, grid_mode):
    b, t, _ = rkv.shape
    d = D_ATT
    tt = 2 * GRID_W
    halo = GRID_W if grid_mode else 8
    hpt = tt // halo
    nhalo = t // halo
    grid = (b, t // tt)

    def tile_specs(w, c):
        return [pl.BlockSpec((1, tt, w), lambda i, j: (i, j, c)),
                pl.BlockSpec((1, halo, w), lambda i, j: (i, jnp.maximum(j * hpt - 1, 0), c)),
                pl.BlockSpec((1, halo, w), lambda i, j: (i, jnp.minimum((j + 1) * hpt, nhalo - 1), c))]

    in_specs = []
    args = []
    for c in range(3):
        in_specs += tile_specs(d, c)
        args += [rkv, rkv, rkv]
    in_specs += tile_specs(D_LORA_PAD, 0)
    args += [lora, lora, lora]

    def vec(c=0, w=d):
        return pl.BlockSpec((1, w), lambda i, j, c=c: (0, c))

    def full(shape):
        return pl.BlockSpec(shape, lambda i, j: (0,) * len(shape))

    in_specs += [vec(0), vec(1), vec(2), vec(0, D_LORA_PAD)]
    args += [pw["mu_rkv"], pw["mu_rkv"], pw["mu_rkv"], pw["mu_lora"]]
    in_specs += [full((LANES, d))] * 4
    args += [pw["w2f"], pw["w2b"], pw["a2f"], pw["a2b"]]
    in_specs += [vec()] * 7
    args += [pw["w0f"], pw["w0b"], pw["a0f"], pw["a0b"], pw["k_k"], pw["k_a"], pw["r_k"]]

    pair_spec = pl.BlockSpec((1, N_PAIRS, tt, LANES), lambda i, j: (i, 0, j, 0))
    wl_spec = pl.BlockSpec((1, N_PAIRS, 8, LANES), lambda i, j: (i, 0, j, 0))

    def pair(dtype):
        return jax.ShapeDtypeStruct((b, N_PAIRS, t, LANES), dtype)

    wl = jax.ShapeDtypeStruct((b, N_PAIRS, t // tt * 8, LANES), F32)
    direction = [pair(BF16)] * 4 + [wl]
    out_shape = direction * 2 + [pair(BF16), pair(F32), jax.ShapeDtypeStruct((b, t, D_GATE_LORA), F32)]
    out_specs = ([pair_spec] * 4 + [wl_spec]) * 2 + [pair_spec, pair_spec,
                                                    pl.BlockSpec((1, tt, D_GATE_LORA), lambda i, j: (i, j, 0))]
    return pl.pallas_call(
        functools.partial(_prepare_kernel, grid_mode=grid_mode),
        grid=grid, in_specs=in_specs, out_specs=out_specs, out_shape=out_shape,
        compiler_params=_cparams(("parallel", "parallel")),
        name="prepare_grid" if grid_mode else "prepare_seq",
    )(*args)


def _scan_kernel(*refs, reverse, pairs, tb, emit):
    al_ref, be_ref, ka_ref, rh_ref, wl_ref, v_ref, s0_ref = refs[:7]
    if emit:
        y_ref, st_ref, s_scr = refs[7:]
    else:
        st_ref, s_scr = refs[7:]
    L = CHUNK
    n_chunks = tb // L

    @pl.when(pl.program_id(1) == 0)
    def _():
        s_scr[...] = s0_ref[...]

    t2 = _iota((2 * L, 2 * L), 0)
    s2 = _iota((2 * L, 2 * L), 1)
    same = (t2 >> 6) == (s2 >> 6)
    tl = t2 & (L - 1)
    sl = s2 & (L - 1)
    strict = same & ((sl > tl) if reverse else (sl < tl))
    incl = same & ((sl >= tl) if reverse else (sl <= tl))
    eye = (t2 == s2).astype(F32)
    head_a = _iota((pairs, L, LANES), 2) < HEAD
    zero = jnp.zeros((), BF16)

    def stack(ref, rows):
        x = ref[:, rows, :]
        return jnp.concatenate([jnp.where(head_a, x, zero), jnp.where(head_a, zero, x)], axis=1)

    def chunk(ci, carry):
        cc = (n_chunks - 1 - ci) if reverse else ci
        rows = pl.ds(pl.multiple_of(cc * L, L), L)
        a_s = stack(al_ref, rows)
        b_s = stack(be_ref, rows)
        k_s = stack(ka_ref, rows)
        v_s = stack(v_ref, rows)
        lhs = jnp.concatenate([a_s, stack(rh_ref, rows)], axis=1) if emit else a_s
        xb = _bmm_nt(lhs, b_s)
        xk = _bmm_nt(lhs, k_s)
        m1 = jnp.where(strict, xb[:, :2 * L], 0.0)
        m2 = jnp.where(strict, xk[:, :2 * L], 0.0)
        xp = -m1
        tinv = eye + xp
        for _ in range(5):
            xq = xp.astype(BF16)
            xp = _bmm(xq, xq)
            tinv = tinv + _bmm(tinv.astype(BF16), xp.astype(BF16))
        s = s_scr[...]
        xs = _bmm_nt(lhs, s.astype(BF16))
        if emit:
            n1 = jnp.where(incl, xb[:, 2 * L:], 0.0)
            n2 = jnp.where(incl, xk[:, 2 * L:], 0.0)
            xv = _bmm(jnp.concatenate([m2, n2], axis=1).astype(BF16), v_s)
        else:
            xv = _bmm(m2.astype(BF16), v_s)
        g = xs[:, :2 * L] + xv[:, :2 * L]
        u_s = (-_bmm(tinv.astype(BF16), g.astype(BF16))).astype(BF16)
        if emit:
            y2 = xs[:, 2 * L:] + xv[:, 2 * L:] + _bmm(n1.astype(BF16), u_s)
            y_ref[:, rows, :] = y2[:, :L] + y2[:, L:]
        w_last = wl_ref[:, pl.ds((cc >> 1) * 8 + (cc & 1), 1), :]
        upd = _bmm_tn(jnp.concatenate([u_s, v_s], axis=1), jnp.concatenate([b_s, k_s], axis=1))
        s_scr[...] = (s + upd) * w_last
        return carry

    lax.fori_loop(0, n_chunks, chunk, 0)

    @pl.when(pl.program_id(1) == pl.num_programs(1) - 1)
    def _():
        st_ref[...] = s_scr[...]


def _scan(al, be, ka, rh, wl, v, s0, reverse, emit, pairs=16, tb=256):
    bp, t, _ = al.shape
    tb = min(t, tb)
    nb = t // tb

    def tmap(g, c):
        return (g, (nb - 1 - c) if reverse else c, 0)

    data = pl.BlockSpec((pairs, tb, LANES), tmap)
    wl_spec = pl.BlockSpec((pairs, tb // (2 * CHUNK) * 8, LANES), tmap)
    state = pl.BlockSpec((pairs, LANES, LANES), lambda g, c: (g, 0, 0))
    out_shape = [jax.ShapeDtypeStruct((bp, LANES, LANES), F32)]
    out_specs = [state]
    if emit:
        out_shape = [jax.ShapeDtypeStruct((bp, t, LANES), F32)] + out_shape
        out_specs = [data] + out_specs
    res = pl.pallas_call(
        functools.partial(_scan_kernel, reverse=reverse, pairs=pairs, tb=tb, emit=emit),
        grid=(bp // pairs, nb),
        in_specs=[data] * 4 + [wl_spec, data, state],
        out_specs=out_specs, out_shape=out_shape,
        scratch_shapes=[pltpu.VMEM((pairs, LANES, LANES), F32)],
        compiler_params=_cparams(("parallel", "arbitrary")),
        name="scan_" + ("bwd" if reverse else "fwd") + ("_emit" if emit else "_state"),
    )(al, be, ka, rh, wl, v, s0)
    return (res[0], res[1]) if emit else (None, res[0])


def _readout_kernel(yf_ref, yb_ref, bon_ref, gd_ref, lnw_ref, lnb_ref, g2_ref, w_ref, o_ref):
    y = jnp.concatenate([yf_ref[0, p] + yb_ref[0, p] for p in range(N_PAIRS)], axis=1)
    bonus = jnp.concatenate([bon_ref[0, p] for p in range(N_PAIRS)], axis=1)
    mean = _head_sum(y) * (1.0 / HEAD)
    dlt = y - mean
    var = _head_sum(dlt * dlt) * (1.0 / HEAD)
    yn = dlt * lax.rsqrt(var + GN_EPS) * lnw_ref[...] + lnb_ref[...]
    gate = _dot(jax.nn.sigmoid(gd_ref[0]).astype(BF16), g2_ref[...])
    out = ((yn + bonus) * gate).astype(BF16)
    o_ref[0] = _dot(out, w_ref[...]).astype(o_ref.dtype)


def _readout(yf, yb, bonus, gd, lnw, lnb, g2, w_out):
    b, _, t, _ = yf.shape
    tt = 256
    d = D_ATT
    pair_spec = pl.BlockSpec((1, N_PAIRS, tt, LANES), lambda i, j: (i, 0, j, 0))
    return pl.pallas_call(
        _readout_kernel,
        grid=(b, t // tt),
        in_specs=[pair_spec, pair_spec, pair_spec,
                  pl.BlockSpec((1, tt, D_GATE_LORA), lambda i, j: (i, j, 0)),
                  pl.BlockSpec((1, d), lambda i, j: (0, 0)),
                  pl.BlockSpec((1, d), lambda i, j: (0, 0)),
                  pl.BlockSpec((D_GATE_LORA, d), lambda i, j: (0, 0)),
                  pl.BlockSpec((d, D_MODEL), lambda i, j: (0, 0))],
        out_specs=pl.BlockSpec((1, tt, D_MODEL), lambda i, j: (i, j, 0)),
        out_shape=jax.ShapeDtypeStruct((b, t, D_MODEL), BF16),
        compiler_params=_cparams(("parallel", "parallel")),
        name="readout",
    )(yf, yb, bonus, gd, lnw, lnb, g2, w_out)


def _pool_kernel(u_ref, pw_ref, ps_ref, wo_ref, o_ref):
    u = u_ref[0]
    tt = u.shape[0]
    t2 = _iota((tt, tt), 0)
    s2 = _iota((tt, tt), 1)
    same = (t2 >> 6) == (s2 >> 6)
    tc = t2 & (GRID_W - 1)
    sc = s2 & (GRID_W - 1)
    col = _iota((tt, POOL_GROUP), 0) & (GRID_W - 1)
    ys = []
    for gi, w in enumerate(POOL_WINDOWS):
        ug = u[:, gi * POOL_GROUP:(gi + 1) * POOL_GROUP]
        win = (same & (sc >= tc - w // 2) & (sc < tc + (w - w // 2))).astype(BF16)
        hi, lo = _split2(ug)
        wsum = _dot(win, hi) + _dot(win, lo)
        cnt = (jnp.minimum(col + (w - w // 2), GRID_W) - jnp.maximum(col - w // 2, 0)).astype(F32)
        dlt = wsum / cnt - ug
        ys.append(_dot(dlt.astype(BF16), pw_ref[gi]))
    y1 = jnp.concatenate(ys, axis=1) * ps_ref[...]
    o_ref[0] = _dot(y1.astype(BF16), wo_ref[...]).astype(o_ref.dtype)


def _pool_branch(u, pool_w, pool_scale, w_pool_out):
    b, t, _ = u.shape
    tt = 256
    return pl.pallas_call(
        _pool_kernel,
        grid=(b, t // tt),
        in_specs=[pl.BlockSpec((1, tt, D_POOL), lambda i, j: (i, j, 0)),
                  pl.BlockSpec((4, POOL_GROUP, POOL_GROUP), lambda i, j: (0, 0, 0)),
                  pl.BlockSpec((1, D_POOL), lambda i, j: (0, 0)),
                  pl.BlockSpec((D_POOL, D_MODEL), lambda i, j: (0, 0))],
        out_specs=pl.BlockSpec((1, tt, D_MODEL), lambda i, j: (i, j, 0)),
        out_shape=jax.ShapeDtypeStruct((b, t, D_MODEL), BF16),
        compiler_params=_cparams(("parallel", "parallel")),
        name="pool_branch",
    )(u, pool_w, pool_scale, w_pool_out)


ROW_TILE = (D_MODEL // LANES, LANES)


def _to_row_tiles(x):
    return x.astype(BF16).reshape((x.shape[0],) + ROW_TILE)


def _from_row_tiles(x):
    return x.reshape(x.shape[0], D_MODEL)


def _merge_kernel(yp_ref, yr_ref, gp_ref, gr_ref, x_ref, gt_ref, sc_ref, sh_ref, g_ref, w_ref, rw_ref,
                  x1_ref, h_ref, lg_ref):
    m = (jax.nn.sigmoid(gp_ref[0].astype(F32)) * yp_ref[0].astype(F32)
         + jax.nn.sigmoid(gr_ref[0].astype(F32)) * yr_ref[0].astype(F32))
    x1 = x_ref[0] + gt_ref[0] * _dot(m.astype(BF16), w_ref[...])
    x1_ref[0] = x1
    ms = jnp.mean(x1 * x1, axis=-1, keepdims=True)
    h = x1 * lax.rsqrt(ms + NORM_EPS) * g_ref[...]
    h = h * (1.0 + sc_ref[0]) + sh_ref[0]
    h_ref[0] = _to_row_tiles(h)
    hh, hl = _split2(h)
    rh, rl = _split2(rw_ref[...])
    lg_ref[0] = _dot(hh, rh) + _dot(hl, rh) + _dot(hh, rl)


def _merge(y_pool, y_rwkv, gates, x, gt1, sc2, sh2, g2, w_out, router_w_pad):
    b, t, d = x.shape
    tt = 256
    tile = pl.BlockSpec((1, tt, d), lambda i, j: (i, j, 0))
    mod = pl.BlockSpec((1, 1, d), lambda i, j: (i, 0, 0))
    return pl.pallas_call(
        _merge_kernel,
        grid=(b, t // tt),
        in_specs=[tile, tile,
                  pl.BlockSpec((1, tt, d), lambda i, j: (i, j, 0)),
                  pl.BlockSpec((1, tt, d), lambda i, j: (i, j, 1)),
                  tile, mod, mod, mod,
                  pl.BlockSpec((1, d), lambda i, j: (0, 0)),
                  pl.BlockSpec((d, d), lambda i, j: (0, 0)),
                  pl.BlockSpec((d, LANES), lambda i, j: (0, 0))],
        out_specs=[tile,
                   pl.BlockSpec((1, tt) + ROW_TILE, lambda i, j: (i, j, 0, 0)),
                   pl.BlockSpec((1, tt, LANES), lambda i, j: (i, j, 0))],
        out_shape=[jax.ShapeDtypeStruct((b, t, d), F32),
                   jax.ShapeDtypeStruct((b, t) + ROW_TILE, BF16),
                   jax.ShapeDtypeStruct((b, t, LANES), F32)],
        compiler_params=_cparams(("parallel", "parallel")),
        name="merge",
    )(y_pool, y_rwkv, gates, gates, x, gt1, sc2, sh2, g2.reshape(1, d), w_out, router_w_pad)


def _router_kernel(lg_ref, bias_ref, e_ref, rk_ref, w_ref, cnt_ref, carry):
    tt = lg_ref.shape[0]
    shape = (tt, LANES)
    lane = _iota(shape, 1)
    valid = lane < N_EXPERTS
    grp = (lane & (N_EXPERTS - 1)) >> 3
    neg = jnp.float32(-jnp.inf)

    @pl.when(pl.program_id(0) == 0)
    def _():
        carry[...] = jnp.zeros_like(carry)

    scores = jax.nn.sigmoid(lg_ref[...])
    sel = scores + bias_ref[...]
    sel = jnp.where(valid, sel, pltpu.roll(sel, N_EXPERTS, 1))

    def group_reduce(x, op):
        for sh in (1, 2, 4):
            up = pltpu.roll(x, sh, 1)
            dn = pltpu.roll(x, LANES - sh, 1)
            x = op(x, jnp.where((lane & sh) != 0, up, dn))
        return x

    m1 = group_reduce(sel, jnp.maximum)
    first = group_reduce(jnp.where(sel == m1, lane, LANES), jnp.minimum)
    m2 = group_reduce(jnp.where(lane == first, neg, sel), jnp.maximum)
    gs = m1 + m2
    beaten = jnp.zeros(shape, I32)
    for k in range(1, N_GROUPS):
        other = pltpu.roll(gs, 8 * k, 1)
        og = (grp - k) & (N_GROUPS - 1)
        beaten = beaten + ((other > gs) | ((other == gs) & (og < grp))).astype(I32)
    cur = jnp.where((beaten < TOPK_GROUPS) & valid, sel, neg)

    picked = jnp.zeros(shape, jnp.bool_)
    e_acc = jnp.zeros(shape, I32)
    w_acc = jnp.zeros(shape, F32)
    idxs = []
    for k in range(TOP_K):
        m = jnp.max(cur, axis=1, keepdims=True)
        idx = jnp.min(jnp.where(cur == m, lane, LANES), axis=1, keepdims=True)
        oh = lane == idx
        sc = jnp.sum(jnp.where(oh, scores, 0.0), axis=1, keepdims=True)
        e_acc = jnp.where(lane == k, idx, e_acc)
        w_acc = jnp.where(lane == k, sc, w_acc)
        picked = picked | oh
        cur = jnp.where(oh, neg, cur)
        idxs.append(idx)
    wsum = jnp.sum(w_acc, axis=1, keepdims=True)
    w_ref[...] = w_acc / wsum * ROUTED_SCALE
    e_ref[...] = e_acc

    lower = (_iota((tt, tt), 1) < _iota((tt, tt), 0)).astype(BF16)
    pk = picked.astype(BF16)
    before = _dot(lower, pk) + carry[...]
    r_acc = jnp.zeros(shape, F32)
    for k in range(TOP_K):
        rk = jnp.sum(jnp.where(lane == idxs[k], before, 0.0), axis=1, keepdims=True)
        r_acc = jnp.where(lane == k, rk, r_acc)
    rk_ref[...] = r_acc.astype(I32)
    carry[...] = carry[...] + jnp.sum(picked.astype(F32), axis=0, keepdims=True)
    cnt_ref[...] = carry[...]


def _router(logits, bias_pad):
    n = logits.shape[0]
    tt = min(1024, n)
    tile = pl.BlockSpec((tt, LANES), lambda i: (i, 0))
    row = pl.BlockSpec((1, LANES), lambda i: (0, 0))
    return pl.pallas_call(
        _router_kernel,
        grid=(n // tt,),
        in_specs=[tile, row],
        out_specs=[tile, tile, tile, row],
        out_shape=[jax.ShapeDtypeStruct((n, LANES), I32), jax.ShapeDtypeStruct((n, LANES), I32),
                   jax.ShapeDtypeStruct((n, LANES), F32), jax.ShapeDtypeStruct((1, LANES), F32)],
        scratch_shapes=[pltpu.VMEM((1, LANES), F32)],
        compiler_params=_cparams(("arbitrary",)),
        name="router",
    )(logits, bias_pad)


def _dispatch_kernel(e_ref, rk_ref, ps_ref, cnt_ref, h_ref, xs_ref, zbuf, sem, zsem, *, tt):
    bm = EXPERT_BLOCK

    @pl.when(pl.program_id(0) == 0)
    def _():
        zbuf[...] = jnp.zeros_like(zbuf)

        def tail_copy(e):
            last = ps_ref[e] + ((cnt_ref[e] + bm - 1) & -bm) - bm
            return pltpu.make_async_copy(zbuf, xs_ref.at[pl.ds(pl.multiple_of(last, bm), bm)], zsem)

        def tail_start(e, c):
            @pl.when((cnt_ref[e] & (bm - 1)) != 0)
            def _():
                tail_copy(e).start()
            return c

        def tail_wait(e, c):
            @pl.when((cnt_ref[e] & (bm - 1)) != 0)
            def _():
                tail_copy(e).wait()
            return c

        lax.fori_loop(0, N_EXPERTS, tail_start, 0)
        lax.fori_loop(0, N_EXPERTS, tail_wait, 0)

    def row_copy(src_row, dst_row):
        return pltpu.make_async_copy(h_ref.at[pl.ds(src_row, 1)], xs_ref.at[pl.ds(dst_row, 1)], sem)

    def start(t, c):
        for k in range(TOP_K):
            j = t * TOP_K + k
            row_copy(t, ps_ref[e_ref[j]] + rk_ref[j]).start(priority=k % 2)
        return c

    lax.fori_loop(0, tt, start, 0)

    for _ in range(TOP_K):
        pltpu.make_async_copy(h_ref, xs_ref.at[pl.ds(0, tt)], sem).wait()


def _dispatch(e_flat, rk_flat, pstarts, counts, h_rows, cap):
    n = h_rows.shape[0]
    tt = min(512, n)
    smem_blk = pl.BlockSpec((tt * TOP_K,), lambda i: (i,), memory_space=pltpu.SMEM)
    return pl.pallas_call(
        functools.partial(_dispatch_kernel, tt=tt),
        grid=(n // tt,),
        in_specs=[smem_blk, smem_blk,
                  pl.BlockSpec(memory_space=pltpu.SMEM),
                  pl.BlockSpec(memory_space=pltpu.SMEM),
                  pl.BlockSpec((tt,) + ROW_TILE, lambda i: (i, 0, 0))],
        out_specs=pl.BlockSpec(memory_space=pl.ANY),
        out_shape=jax.ShapeDtypeStruct((cap,) + ROW_TILE, h_rows.dtype),
        scratch_shapes=[pltpu.VMEM((EXPERT_BLOCK,) + ROW_TILE, h_rows.dtype),
                        pltpu.SemaphoreType.DMA(()), pltpu.SemaphoreType.DMA(())],
        compiler_params=_cparams(("arbitrary",)),
        name="dispatch",
    )(e_flat, rk_flat, pstarts, counts, h_rows)


def _expert_kernel(be_ref, nu_ref, x_ref, wg_ref, wu_ref, wd_ref, o_ref):
    del be_ref
    used = pl.program_id(0) < nu_ref[0]

    @pl.when(used)
    def _():
        xb = _from_row_tiles(x_ref[...])
        g = _dot(xb, wg_ref[0].astype(BF16))
        u = _dot(xb, wu_ref[0].astype(BF16))
        act = (g * jax.nn.sigmoid(g) * u).astype(BF16)
        o_ref[...] = _to_row_tiles(_dot(act, wd_ref[0].astype(BF16)))

    @pl.when(jnp.logical_not(used))
    def _():
        o_ref[...] = jnp.zeros_like(o_ref)


def _experts(block_expert, n_used, xs, wg, wu, wd):
    cap = xs.shape[0]
    bm = EXPERT_BLOCK
    de = wg.shape[2]
    grid_spec = pltpu.PrefetchScalarGridSpec(
        num_scalar_prefetch=2,
        grid=(cap // bm,),
        in_specs=[pl.BlockSpec((bm,) + ROW_TILE, lambda i, be, nu: (jnp.minimum(i, nu[0] - 1), 0, 0)),
                  pl.BlockSpec((1, D_MODEL, de), lambda i, be, nu: (be[i], 0, 0)),
                  pl.BlockSpec((1, D_MODEL, de), lambda i, be, nu: (be[i], 0, 0)),
                  pl.BlockSpec((1, de, D_MODEL), lambda i, be, nu: (be[i], 0, 0))],
        out_specs=pl.BlockSpec((bm,) + ROW_TILE, lambda i, be, nu: (i, 0, 0)),
    )
    return pl.pallas_call(
        _expert_kernel,
        grid_spec=grid_spec,
        out_shape=jax.ShapeDtypeStruct((cap,) + ROW_TILE, BF16),
        compiler_params=_cparams(("arbitrary",)),
        name="experts",
    )(block_expert, n_used, xs, wg, wu, wd)


def _combine_kernel(e_ref, rk_ref, en_ref, rkn_ref, ps_ref, ys_ref, w_ref, sh_ref, x_ref, gt_ref, g_ref, o_ref,
                    buf, sem, *, tt):
    step = pl.program_id(0) * pl.num_programs(1) + pl.program_id(1)
    n_steps = pl.num_programs(0) * pl.num_programs(1)
    slot = step & 1

    def gather(idx_ref, rank_ref, dst_slot):
        def start(t, c):
            for k in range(TOP_K):
                j = t * TOP_K + k
                pltpu.make_async_copy(ys_ref.at[pl.ds(ps_ref[idx_ref[j]] + rank_ref[j], 1)],
                                      buf.at[dst_slot, k, pl.ds(t, 1)], sem.at[dst_slot]).start(priority=k % 2)
            return c

        lax.fori_loop(0, tt, start, 0)

    @pl.when(step == 0)
    def _():
        gather(e_ref, rk_ref, 0)

    @pl.when(step + 1 < n_steps)
    def _():
        gather(en_ref, rkn_ref, 1 - slot)

    for k in range(TOP_K):
        pltpu.make_async_copy(ys_ref.at[pl.ds(0, tt)], buf.at[slot, k], sem.at[slot]).wait()

    w = w_ref[0]
    moe = _from_row_tiles(sh_ref[0]).astype(F32)
    for k in range(TOP_K):
        moe = moe + w[:, k:k + 1] * _from_row_tiles(buf[slot, k]).astype(F32)
    x2 = x_ref[0] + gt_ref[0] * moe
    ms = jnp.mean(x2 * x2, axis=-1, keepdims=True)
    o_ref[0] = x2 * lax.rsqrt(ms + NORM_EPS) * g_ref[...]


def _combine(e_flat, rk_flat, pstarts, ys, wsel, shared, x1, gt2, final_g):
    b, t, d = x1.shape
    tt = 128
    nt = t // tt
    last = b * nt - 1
    smem_blk = pl.BlockSpec((tt * TOP_K,), lambda i, j: (i * nt + j,), memory_space=pltpu.SMEM)
    smem_nxt = pl.BlockSpec((tt * TOP_K,), lambda i, j: (jnp.minimum(i * nt + j + 1, last),),
                            memory_space=pltpu.SMEM)
    return pl.pallas_call(
        functools.partial(_combine_kernel, tt=tt),
        grid=(b, nt),
        in_specs=[smem_blk, smem_blk, smem_nxt, smem_nxt,
                  pl.BlockSpec(memory_space=pltpu.SMEM),
                  pl.BlockSpec(memory_space=pl.ANY),
                  pl.BlockSpec((1, tt, LANES), lambda i, j: (i, j, 0)),
                  pl.BlockSpec((1, tt) + ROW_TILE, lambda i, j: (i, j, 0, 0)),
                  pl.BlockSpec((1, tt, d), lambda i, j: (i, j, 0)),
                  pl.BlockSpec((1, 1, d), lambda i, j: (i, 0, 0)),
                  pl.BlockSpec((1, d), lambda i, j: (0, 0))],
        out_specs=pl.BlockSpec((1, tt, d), lambda i, j: (i, j, 0)),
        out_shape=jax.ShapeDtypeStruct((b, t, d), F32),
        scratch_shapes=[pltpu.VMEM((2, TOP_K, tt) + ROW_TILE, BF16), pltpu.SemaphoreType.DMA((2,))],
        compiler_params=_cparams(("arbitrary", "arbitrary")),
        name="combine",
    )(e_flat, rk_flat, e_flat, rk_flat, pstarts, ys, wsel, shared, x1, gt2, final_g.reshape(1, d))


def _pad_rows(w, rows):
    return jnp.zeros((rows, w.shape[1]), w.dtype).at[:w.shape[0]].set(w)


def _lora_pad_cols(w):
    out = jnp.zeros(w.shape[:-1] + (D_LORA_PAD,), w.dtype)
    for i in range(4):
        out = out.at[..., i * LANES:i * LANES + D_LORA].set(w[..., i * D_LORA:(i + 1) * D_LORA])
    return out.at[..., 4 * LANES:].set(w[..., 4 * D_LORA:])


def kernel(x, c, ctx, c_ctx, norm1_g, norm2_g, ada_w, ada_b, w_in, shift_mu, pool_w, pool_scale, w_pool_out, decay_w0, decay_w2, iclr_a0, iclr_a2, gate_g2, k_k, k_a, r_k, lnx_w, lnx_b, w_rwkv_out, w_out, router_w, router_bias, exp_w_gate, exp_w_up, exp_w_down, shared_w_gate, shared_w_up, shared_w_down, final_g):
    B, T, D = x.shape
    TC = ctx.shape[1]
    n = B * T
    l = 0

    w_in_l = w_in[l]
    w_u = w_in_l[:, :D_POOL].astype(BF16)
    w_rkv = w_in_l[:, D_POOL:D_POOL + 3 * D_ATT].astype(BF16)
    w_lora = _lora_pad_cols(w_in_l[:, D_POOL + 3 * D_ATT:D_POOL + 3 * D_ATT + 4 * D_LORA + D_GATE_LORA]).astype(BF16)
    w_gates = w_in_l[:, D_POOL + 3 * D_ATT + 4 * D_LORA + D_GATE_LORA:].astype(BF16)
    mu = shift_mu[l]
    row = lambda a: a.reshape(1, -1)
    pw = {
        "mu_rkv": row(mu[:3 * D_ATT]),
        "mu_lora": row(_lora_pad_cols(mu[3 * D_ATT:])),
        "w2f": _pad_rows(decay_w2[l, 0], LANES).astype(BF16),
        "w2b": _pad_rows(decay_w2[l, 1], LANES).astype(BF16),
        "a2f": _pad_rows(iclr_a2[l, 0], LANES).astype(BF16),
        "a2b": _pad_rows(iclr_a2[l, 1], LANES).astype(BF16),
        "w0f": row(decay_w0[l, 0]), "w0b": row(decay_w0[l, 1]),
        "a0f": row(iclr_a0[l, 0]), "a0b": row(iclr_a0[l, 1]),
        "k_k": row(k_k[l]), "k_a": row(k_a[l]), "r_k": row(r_k[l]),
    }

    cstack = jnp.zeros((8, D), F32).at[:B].set(c).at[B].set(c_ctx)
    mod = _ada_mod(cstack, ada_w[l], ada_b[l])
    sh1, sc1, gt1, sh2, sc2, gt2 = [mod[:B, i * D:(i + 1) * D].reshape(B, 1, D) for i in range(6)]
    sh1c = jnp.broadcast_to(mod[B, 0:D].reshape(1, 1, D), (B, 1, D))
    sc1c = jnp.broadcast_to(mod[B, D:2 * D].reshape(1, 1, D), (B, 1, D))

    hc = _norm_mod(ctx, norm1_g[l], sc1c, sh1c, TC).reshape(B * TC, D)
    rkv_c = _matmul(hc, w_rkv, F32, 512, 512).reshape(B, TC, 3 * D_ATT)
    lora_c = _matmul(hc, w_lora, F32, 512, D_LORA_PAD).reshape(B, TC, D_LORA_PAD)
    pc = _prepare(rkv_c, lora_c, pw, grid_mode=False)
    flat = lambda a: a.reshape(B * N_PAIRS, a.shape[2], LANES)
    s0 = jnp.zeros((B * N_PAIRS, LANES, LANES), F32)
    vc = flat(pc[10])
    _, st_f = _scan(*[flat(a) for a in pc[0:5]], vc, s0, reverse=False, emit=False)
    _, st_b = _scan(*[flat(a) for a in pc[5:10]], vc, s0, reverse=True, emit=False)

    h = _norm_mod(x, norm1_g[l], sc1, sh1, 512).reshape(n, D)
    u = _matmul(h, w_u, F32, 2048, 512).reshape(B, T, D_POOL)
    rkv = _matmul(h, w_rkv, F32, 2048, 512).reshape(B, T, 3 * D_ATT)
    lora = _matmul(h, w_lora, F32, 2048, D_LORA_PAD).reshape(B, T, D_LORA_PAD)
    gates = _matmul(h, w_gates, BF16, 2048, 512).reshape(B, T, 2 * D)
    pp = _prepare(rkv, lora, pw, grid_mode=True)
    v_, bonus, gd = flat(pp[10]), pp[11], pp[12]
    y_f, _ = _scan(*[flat(a) for a in pp[0:5]], v_, st_f, reverse=False, emit=True)
    y_b, _ = _scan(*[flat(a) for a in pp[5:10]], v_, st_b, reverse=True, emit=True)
    y_rwkv = _readout(y_f.reshape(B, N_PAIRS, T, LANES), y_b.reshape(B, N_PAIRS, T, LANES), bonus, gd,
                      row(lnx_w[l]), row(lnx_b[l]), gate_g2[l].astype(BF16), w_rwkv_out[l].astype(BF16))
    y_pool = _pool_branch(u, pool_w[l].astype(BF16), row(pool_scale[l]), w_pool_out[l].astype(BF16))

    router_w_pad = jnp.zeros((D, LANES), F32).at[:, :N_EXPERTS].set(router_w[l])
    x1, h2, logits = _merge(y_pool, y_rwkv, gates, x, gt1, sc2, sh2, norm2_g[l],
                                w_out[l].astype(BF16), router_w_pad)

    bias_pad = jnp.zeros((1, LANES), F32).at[0, :N_EXPERTS].set(router_bias[l])
    e_idx, e_rank, wsel, counts = _router(logits.reshape(n, LANES), bias_pad)
    e_flat = e_idx[:, :TOP_K].reshape(-1)
    rk_flat = e_rank[:, :TOP_K].reshape(-1)
    bm = EXPERT_BLOCK
    cnt = counts[0, :N_EXPERTS].astype(I32)
    padded = (cnt + bm - 1) // bm * bm
    pend = jnp.cumsum(padded)
    pstarts = (pend - padded).astype(I32)
    cap = n * TOP_K + N_EXPERTS * bm
    n_blocks = cap // bm
    block_start = jnp.arange(n_blocks, dtype=I32) * bm
    block_expert = jnp.minimum(jnp.sum(block_start[:, None] >= pend[None, :], axis=-1), N_EXPERTS - 1).astype(I32)
    n_used = (pend[-1] // bm).astype(I32).reshape(1)

    h2 = h2.reshape((n,) + ROW_TILE)
    xs = _dispatch(e_flat, rk_flat, pstarts, cnt, h2, cap)
    ys = _experts(block_expert, n_used, xs, exp_w_gate[l], exp_w_up[l], exp_w_down[l])
    shared = _experts(jnp.zeros((n // bm,), I32), jnp.full((1,), n // bm, I32), h2,
                      shared_w_gate[l][None], shared_w_up[l][None], shared_w_down[l][None])
    return _combine(e_flat, rk_flat, pstarts, ys, wsel.reshape(B, T, LANES),
                    shared.reshape((B, T) + ROW_TILE), x1, gt2, final_g)
```

```python
import functools

import jax
import jax.numpy as jnp
from jax import lax
from jax.experimental import pallas as pl
from jax.experimental.pallas import tpu as pltpu

F32 = jnp.float32
BF16 = jnp.bfloat16
I32 = jnp.int32
U32 = jnp.uint32

D_MODEL = 2048
GRID_W = 64
POOL_WINDOWS = (2, 4, 8, 16)
POOL_GROUP = 256
D_POOL = 1024
HEAD = 64
N_HEADS = 32
N_PAIRS = N_HEADS // 2
D_ATT = 2048
D_LORA = 96
D_GATE_LORA = 256
D_LORA_PAD = 768
GN_EPS = 64e-5
NORM_EPS = 1e-6
N_EXPERTS = 64
TOP_K = 8
N_GROUPS = 8
TOPK_GROUPS = 4
D_EXPERT = 512
ROUTED_SCALE = 2.5
EXP_M05 = 0.6065306597126334

LANES = 128
CHUNK = 64
EXPERT_BLOCK = 512
VMEM_LIMIT = 56 * 1024 * 1024


def _cparams(sem):
    return pltpu.CompilerParams(dimension_semantics=sem, vmem_limit_bytes=VMEM_LIMIT)


def _dot(a, b):
    return jnp.dot(a, b, preferred_element_type=F32)


def _dot_nt(a, b):
    return lax.dot_general(a, b, (((1,), (1,)), ((), ())), preferred_element_type=F32)


def _bmm(a, b):
    return lax.dot_general(a, b, (((2,), (1,)), ((0,), (0,))), preferred_element_type=F32)


def _bmm_nt(a, b):
    return lax.dot_general(a, b, (((2,), (2,)), ((0,), (0,))), preferred_element_type=F32)


def _bmm_tn(a, b):
    return lax.dot_general(a, b, (((1,), (1,)), ((0,), (0,))), preferred_element_type=F32)


def _sigmoid(x):
    return 0.5 * jnp.tanh(0.5 * x) + 0.5


def _split2(x):
    hi = x.astype(BF16)
    lo = (x - hi.astype(F32)).astype(BF16)
    return hi, lo


def _iota(shape, axis):
    return lax.broadcasted_iota(I32, shape, axis)


def _ada_kernel(c_ref, w_ref, b_ref, o_ref):
    c = c_ref[...]
    a = c * _sigmoid(c)
    o_ref[...] = _dot(a.astype(BF16), w_ref[...].astype(BF16)) + b_ref[...]


def _ada_mod(cstack, ada_w, ada_b):
    m, d = cstack.shape
    n = ada_w.shape[1]
    tn = 1024
    return pl.pallas_call(
        _ada_kernel,
        grid=(n // tn,),
        in_specs=[pl.BlockSpec((m, d), lambda j: (0, 0)),
                  pl.BlockSpec((d, tn), lambda j: (0, j)),
                  pl.BlockSpec((1, tn), lambda j: (0, j))],
        out_specs=pl.BlockSpec((m, tn), lambda j: (0, j)),
        out_shape=jax.ShapeDtypeStruct((m, n), F32),
        compiler_params=_cparams(("parallel",)),
        name="ada_mod",
    )(cstack, ada_w, ada_b.reshape(1, n))


def _norm_mod_kernel(x_ref, g_ref, sc_ref, sh_ref, o_ref):
    x = x_ref[0]
    ms = jnp.mean(x * x, axis=-1, keepdims=True)
    y = x * lax.rsqrt(ms + NORM_EPS) * g_ref[...]
    o_ref[0] = (y * (1.0 + sc_ref[0]) + sh_ref[0]).astype(o_ref.dtype)


def _norm_mod(x, g, sc, sh, tt):
    b, t, d = x.shape
    return pl.pallas_call(
        _norm_mod_kernel,
        grid=(b, t // tt),
        in_specs=[pl.BlockSpec((1, tt, d), lambda i, j: (i, j, 0)),
                  pl.BlockSpec((1, d), lambda i, j: (0, 0)),
                  pl.BlockSpec((1, 1, d), lambda i, j: (i, 0, 0)),
                  pl.BlockSpec((1, 1, d), lambda i, j: (i, 0, 0))],
        out_specs=pl.BlockSpec((1, tt, d), lambda i, j: (i, j, 0)),
        out_shape=jax.ShapeDtypeStruct((b, t, d), BF16),
        compiler_params=_cparams(("parallel", "parallel")),
        name="norm_mod",
    )(x, g.reshape(1, d), sc, sh)


def _mm_kernel(a_ref, b_ref, o_ref):
    o_ref[...] = _dot(a_ref[...], b_ref[...]).astype(o_ref.dtype)


def _matmul(a, b, out_dtype, tm, tn):
    m, k = a.shape
    n = b.shape[1]
    tm = min(tm, m)
    return pl.pallas_call(
        _mm_kernel,
        grid=(m // tm, n // tn),
        in_specs=[pl.BlockSpec((tm, k), lambda i, j: (i, 0)),
                  pl.BlockSpec((k, tn), lambda i, j: (0, j))],
        out_specs=pl.BlockSpec((tm, tn), lambda i, j: (i, j)),
        out_shape=jax.ShapeDtypeStruct((m, n), out_dtype),
        compiler_params=_cparams(("parallel", "parallel")),
        name="matmul",
    )(a, b)


def _shift_grid(x, prev, nxt, first, last):
    tt, c = x.shape
    col = _iota((tt, c), 0) & (GRID_W - 1)
    m = _iota((tt, c), 1) & 3
    left = jnp.where(col == 0, 0.0, pltpu.roll(x, 1, 0))
    right = jnp.where(col == GRID_W - 1, 0.0, pltpu.roll(x, tt - 1, 0))
    prev = jnp.where(first, 0.0, prev)
    nxt = jnp.where(last, 0.0, nxt)
    if tt > GRID_W:
        up = jnp.concatenate([prev, x[:tt - GRID_W]], axis=0)
        down = jnp.concatenate([x[GRID_W:], nxt], axis=0)
    else:
        up, down = prev, nxt
    return jnp.where(m == 0, left, jnp.where(m == 1, right, jnp.where(m == 2, up, down)))


def _shift_seq(x, prev8, next8, first, last):
    t, c = x.shape
    row = _iota((t, c), 0)
    odd = (_iota((t, c), 1) & 1) == 1
    before = jnp.where(first, 0.0, prev8[7:8])
    after = jnp.where(last, 0.0, next8[0:1])
    prev = jnp.where(row == 0, before, pltpu.roll(x, 1, 0))
    nxt = jnp.where(row == t - 1, after, pltpu.roll(x, t - 1, 0))
    return jnp.where(odd, nxt, prev)


def _head_sum(x):
    w = 2 * LANES
    ones = (_iota((w, w), 0) >> 6 == _iota((w, w), 1) >> 6).astype(BF16)
    outs = []
    for c in range(x.shape[1] // w):
        hi, lo = _split2(x[:, c * w:(c + 1) * w])
        outs.append(_dot(hi, ones) + _dot(lo, ones))
    return jnp.concatenate(outs, axis=1)


def _prepare_kernel(*refs, grid_mode):
    (r_ref, rp_ref, rn_ref, k_ref, kp_ref, kn_ref, v_ref, vp_ref, vn_ref,
     l_ref, lp_ref, ln_ref) = refs[:12]
    rest = refs[12:]
    (mur_ref, muk_ref, muv_ref, mul_ref, w2f_ref, w2b_ref, a2f_ref, a2b_ref,
     w0f_ref, w0b_ref, a0f_ref, a0b_ref, kk_ref, ka_ref, rk_ref,
     oaf_ref, obf_ref, okf_ref, orf_ref, owf_ref, oab_ref, obb_ref, okb_ref, orb_ref, owb_ref,
     ov_ref, obon_ref, ogd_ref) = rest

    first = pl.program_id(1) == 0
    last = pl.program_id(1) == pl.num_programs(1) - 1
    shift = _shift_grid if grid_mode else _shift_seq

    def mix(x_ref, p_ref, n_ref, mu_ref):
        x = x_ref[0]
        return x + (shift(x, p_ref[0], n_ref[0], first, last) - x) * mu_ref[...]

    r = mix(r_ref, rp_ref, rn_ref, mur_ref)
    k = mix(k_ref, kp_ref, kn_ref, muk_ref)
    v = mix(v_ref, vp_ref, vn_ref, muv_ref)
    lo = mix(l_ref, lp_ref, ln_ref, mul_ref)

    th = jnp.tanh(lo[:, :2 * LANES]).astype(BF16)
    zf = w0f_ref[...] + _dot(th[:, :LANES], w2f_ref[...])
    zb = w0b_ref[...] + _dot(th[:, LANES:], w2b_ref[...])
    lwf = -EXP_M05 * _sigmoid(zf)
    lwb = -EXP_M05 * _sigmoid(zb)
    ad = lo[:, 2 * LANES:4 * LANES].astype(BF16)
    af = _sigmoid(a0f_ref[...] + _dot(ad[:, :LANES], a2f_ref[...]))
    ab = _sigmoid(a0b_ref[...] + _dot(ad[:, LANES:], a2b_ref[...]))
    ogd_ref[0] = lo[:, 4 * LANES:]

    kk = k * kk_ref[...]
    kk = kk * lax.rsqrt(_head_sum(kk * kk) + 1e-12)
    ka = ka_ref[...]
    kf = k * (1.0 + (af - 1.0) * ka)
    kb = k * (1.0 + (ab - 1.0) * ka)
    bonus = _head_sum(r * (kf + kb) * rk_ref[...]) * v

    tt = r.shape[0]
    t2 = _iota((tt, tt), 0)
    s2 = _iota((tt, tt), 1)
    same = (t2 >> 6) == (s2 >> 6)

    def scan_operands(lw, b, kd, reverse):
        tri = (same & ((s2 >= t2) if reverse else (s2 <= t2))).astype(BF16)
        hi = lw.astype(BF16)
        r1 = lw - hi.astype(F32)
        mid = r1.astype(BF16)
        lo3 = (r1 - mid.astype(F32)).astype(BF16)
        cum = _dot(tri, hi) + _dot(tri, mid) + _dot(tri, lo3)
        w_inc = jnp.exp(cum)
        w_inv = jnp.exp(-cum)
        ends = [w_inc[c * CHUNK:c * CHUNK + 1] if reverse else w_inc[(c + 1) * CHUNK - 1:(c + 1) * CHUNK]
                for c in range(tt // CHUNK)]
        w_last = jnp.concatenate(ends + [jnp.zeros((8 - len(ends), lw.shape[1]), F32)], axis=0)
        return jnp.exp(cum - lw) * kk, b * w_inv, kd * w_inv, w_inc * r, w_last

    al_f, be_f, ka_f, rh_f, wl_f = scan_operands(lwf, kk * af, kf, False)
    al_b, be_b, ka_b, rh_b, wl_b = scan_operands(lwb, kk * ab, kb, True)
    outs = ((oaf_ref, al_f), (obf_ref, be_f), (okf_ref, ka_f), (orf_ref, rh_f), (owf_ref, wl_f),
            (oab_ref, al_b), (obb_ref, be_b), (okb_ref, ka_b), (orb_ref, rh_b), (owb_ref, wl_b),
            (ov_ref, v), (obon_ref, bonus))
    for o_ref, val in outs:
        for p in range(N_PAIRS):
            o_ref[0, p] = val[:, p * LANES:(p + 1) * LANES].astype(o_ref.dtype)


def _prepare(rkv, lora, pw, grid_mode):
    b, t, _ = rkv.shape
    d = D_ATT
    tt = 2 * GRID_W
    halo = GRID_W if grid_mode else 8
    hpt = tt // halo
    nhalo = t // halo
    grid = (b, t // tt)

    def tile_specs(w, c):
        return [pl.BlockSpec((1, tt, w), lambda i, j: (i, j, c)),
                pl.BlockSpec((1, halo, w), lambda i, j: (i, jnp.maximum(j * hpt - 1, 0), c)),
                pl.BlockSpec((1, halo, w), lambda i, j: (i, jnp.minimum((j + 1) * hpt, nhalo - 1), c))]

    in_specs = []
    args = []
    for c in range(3):
        in_specs += tile_specs(d, c)
        args += [rkv, rkv, rkv]
    in_specs += tile_specs(D_LORA_PAD, 0)
    args += [lora, lora, lora]

    def vec(c=0, w=d):
        return pl.BlockSpec((1, w), lambda i, j, c=c: (0, c))

    def full(shape):
        return pl.BlockSpec(shape, lambda i, j: (0,) * len(shape))

    in_specs += [vec(0), vec(1), vec(2), vec(0, D_LORA_PAD)]
    args += [pw["mu_rkv"], pw["mu_rkv"], pw["mu_rkv"], pw["mu_lora"]]
    in_specs += [full((LANES, d))] * 4
    args += [pw["w2f"], pw["w2b"], pw["a2f"], pw["a2b"]]
    in_specs += [vec()] * 7
    args += [pw["w0f"], pw["w0b"], pw["a0f"], pw["a0b"], pw["k_k"], pw["k_a"], pw["r_k"]]

    pair_spec = pl.BlockSpec((1, N_PAIRS, tt, LANES), lambda i, j: (i, 0, j, 0))
    wl_spec = pl.BlockSpec((1, N_PAIRS, 8, LANES), lambda i, j: (i, 0, j, 0))

    def pair(dtype):
        return jax.ShapeDtypeStruct((b, N_PAIRS, t, LANES), dtype)

    wl = jax.ShapeDtypeStruct((b, N_PAIRS, t // tt * 8, LANES), F32)
    direction = [pair(BF16)] * 4 + [wl]
    out_shape = direction * 2 + [pair(BF16), pair(F32), jax.ShapeDtypeStruct((b, t, D_GATE_LORA), F32)]
    out_specs = ([pair_spec] * 4 + [wl_spec]) * 2 + [pair_spec, pair_spec,
                                                    pl.BlockSpec((1, tt, D_GATE_LORA), lambda i, j: (i, j, 0))]
    return pl.pallas_call(
        functools.partial(_prepare_kernel, grid_mode=grid_mode),
        grid=grid, in_specs=in_specs, out_specs=out_specs, out_shape=out_shape,
        compiler_params=_cparams(("parallel", "parallel")),
        name="prepare_grid" if grid_mode else "prepare_seq",
    )(*args)


def _scan_kernel(*refs, reverse, pairs, tb, emit):
    al_ref, be_ref, ka_ref, rh_ref, wl_ref, v_ref, s0_ref = refs[:7]
    if emit:
        y_ref, st_ref, s_scr = refs[7:]
    else:
        st_ref, s_scr = refs[7:]
    L = CHUNK
    n_chunks = tb // L

    @pl.when(pl.program_id(1) == 0)
    def _():
        s_scr[...] = s0_ref[...]

    t2 = _iota((2 * L, 2 * L), 0)
    s2 = _iota((2 * L, 2 * L), 1)
    same = (t2 >> 6) == (s2 >> 6)
    tl = t2 & (L - 1)
    sl = s2 & (L - 1)
    strict = same & ((sl > tl) if reverse else (sl < tl))
    incl = same & ((sl >= tl) if reverse else (sl <= tl))
    eye = (t2 == s2).astype(F32)
    head_a = _iota((pairs, L, LANES), 2) < HEAD
    zero = jnp.zeros((), BF16)

    def stack(ref, rows):
        x = ref[:, rows, :]
        return jnp.concatenate([jnp.where(head_a, x, zero), jnp.where(head_a, zero, x)], axis=1)

    def chunk(ci, carry):
        cc = (n_chunks - 1 - ci) if reverse else ci
        rows = pl.ds(pl.multiple_of(cc * L, L), L)
        a_s = stack(al_ref, rows)
        b_s = stack(be_ref, rows)
        k_s = stack(ka_ref, rows)
        v_s = stack(v_ref, rows)
        lhs = jnp.concatenate([a_s, stack(rh_ref, rows)], axis=1) if emit else a_s
        xb = _bmm_nt(lhs, b_s)
        xk = _bmm_nt(lhs, k_s)
        m1 = jnp.where(strict, xb[:, :2 * L], 0.0)
        m2 = jnp.where(strict, xk[:, :2 * L], 0.0)
        xp = -m1
        tinv = eye + xp
        for _ in range(5):
            xq = xp.astype(BF16)
            xp = _bmm(xq, xq)
            tinv = tinv + _bmm(tinv.astype(BF16), xp.astype(BF16))
        s = s_scr[...]
        xs = _bmm_nt(lhs, s.astype(BF16))
        if emit:
            n1 = jnp.where(incl, xb[:, 2 * L:], 0.0)
            n2 = jnp.where(incl, xk[:, 2 * L:], 0.0)
            xv = _bmm(jnp.concatenate([m2, n2], axis=1).astype(BF16), v_s)
        else:
            xv = _bmm(m2.astype(BF16), v_s)
        g = xs[:, :2 * L] + xv[:, :2 * L]
        u_s = (-_bmm(tinv.astype(BF16), g.astype(BF16))).astype(BF16)
        if emit:
            y2 = xs[:, 2 * L:] + xv[:, 2 * L:] + _bmm(n1.astype(BF16), u_s)
            y_ref[:, rows, :] = y2[:, :L] + y2[:, L:]
        w_last = wl_ref[:, pl.ds((cc >> 1) * 8 + (cc & 1), 1), :]
        upd = _bmm_tn(jnp.concatenate([u_s, v_s], axis=1), jnp.concatenate([b_s, k_s], axis=1))
        s_scr[...] = (s + upd) * w_last
        return carry

    lax.fori_loop(0, n_chunks, chunk, 0)

    @pl.when(pl.program_id(1) == pl.num_programs(1) - 1)
    def _():
        st_ref[...] = s_scr[...]


def _scan(al, be, ka, rh, wl, v, s0, reverse, emit, pairs=16, tb=256):
    bp, t, _ = al.shape
    tb = min(t, tb)
    nb = t // tb

    def tmap(g, c):
        return (g, (nb - 1 - c) if reverse else c, 0)

    data = pl.BlockSpec((pairs, tb, LANES), tmap)
    wl_spec = pl.BlockSpec((pairs, tb // (2 * CHUNK) * 8, LANES), tmap)
    state = pl.BlockSpec((pairs, LANES, LANES), lambda g, c: (g, 0, 0))
    out_shape = [jax.ShapeDtypeStruct((bp, LANES, LANES), F32)]
    out_specs = [state]
    if emit:
        out_shape = [jax.ShapeDtypeStruct((bp, t, LANES), F32)] + out_shape
        out_specs = [data] + out_specs
    res = pl.pallas_call(
        functools.partial(_scan_kernel, reverse=reverse, pairs=pairs, tb=tb, emit=emit),
        grid=(bp // pairs, nb),
        in_specs=[data] * 4 + [wl_spec, data, state],
        out_specs=out_specs, out_shape=out_shape,
        scratch_shapes=[pltpu.VMEM((pairs, LANES, LANES), F32)],
        compiler_params=_cparams(("parallel", "arbitrary")),
        name="scan_" + ("bwd" if reverse else "fwd") + ("_emit" if emit else "_state"),
    )(al, be, ka, rh, wl, v, s0)
    return (res[0], res[1]) if emit else (None, res[0])


def _readout_kernel(yf_ref, yb_ref, bon_ref, gd_ref, lnw_ref, lnb_ref, g2_ref, w_ref, o_ref):
    y = jnp.concatenate([yf_ref[0, p] + yb_ref[0, p] for p in range(N_PAIRS)], axis=1)
    bonus = jnp.concatenate([bon_ref[0, p] for p in range(N_PAIRS)], axis=1)
    mean = _head_sum(y) * (1.0 / HEAD)
    dlt = y - mean
    var = _head_sum(dlt * dlt) * (1.0 / HEAD)
    yn = dlt * lax.rsqrt(var + GN_EPS) * lnw_ref[...] + lnb_ref[...]
    gate = _dot(_sigmoid(gd_ref[0]).astype(BF16), g2_ref[...])
    out = ((yn + bonus) * gate).astype(BF16)
    o_ref[0] = _dot(out, w_ref[...]).astype(o_ref.dtype)


def _readout(yf, yb, bonus, gd, lnw, lnb, g2, w_out):
    b, _, t, _ = yf.shape
    tt = 256
    d = D_ATT
    pair_spec = pl.BlockSpec((1, N_PAIRS, tt, LANES), lambda i, j: (i, 0, j, 0))
    return pl.pallas_call(
        _readout_kernel,
        grid=(b, t // tt),
        in_specs=[pair_spec, pair_spec, pair_spec,
                  pl.BlockSpec((1, tt, D_GATE_LORA), lambda i, j: (i, j, 0)),
                  pl.BlockSpec((1, d), lambda i, j: (0, 0)),
                  pl.BlockSpec((1, d), lambda i, j: (0, 0)),
                  pl.BlockSpec((D_GATE_LORA, d), lambda i, j: (0, 0)),
                  pl.BlockSpec((d, D_MODEL), lambda i, j: (0, 0))],
        out_specs=pl.BlockSpec((1, tt, D_MODEL), lambda i, j: (i, j, 0)),
        out_shape=jax.ShapeDtypeStruct((b, t, D_MODEL), BF16),
        compiler_params=_cparams(("parallel", "parallel")),
        name="readout",
    )(yf, yb, bonus, gd, lnw, lnb, g2, w_out)


def _pool_kernel(u_ref, pw_ref, ps_ref, wo_ref, o_ref):
    u = u_ref[0]
    tt = u.shape[0]
    t2 = _iota((tt, tt), 0)
    s2 = _iota((tt, tt), 1)
    same = (t2 >> 6) == (s2 >> 6)
    tc = t2 & (GRID_W - 1)
    sc = s2 & (GRID_W - 1)
    col = _iota((tt, POOL_GROUP), 0) & (GRID_W - 1)
    ys = []
    for gi, w in enumerate(POOL_WINDOWS):
        ug = u[:, gi * POOL_GROUP:(gi + 1) * POOL_GROUP]
        win = (same & (sc >= tc - w // 2) & (sc < tc + (w - w // 2))).astype(BF16)
        hi, lo = _split2(ug)
        wsum = _dot(win, hi) + _dot(win, lo)
        cnt = (jnp.minimum(col + (w - w // 2), GRID_W) - jnp.maximum(col - w // 2, 0)).astype(F32)
        dlt = wsum / cnt - ug
        ys.append(_dot(dlt.astype(BF16), pw_ref[gi]))
    y1 = jnp.concatenate(ys, axis=1) * ps_ref[...]
    o_ref[0] = _dot(y1.astype(BF16), wo_ref[...]).astype(o_ref.dtype)


def _pool_branch(u, pool_w, pool_scale, w_pool_out):
    b, t, _ = u.shape
    tt = 256
    return pl.pallas_call(
        _pool_kernel,
        grid=(b, t // tt),
        in_specs=[pl.BlockSpec((1, tt, D_POOL), lambda i, j: (i, j, 0)),
                  pl.BlockSpec((4, POOL_GROUP, POOL_GROUP), lambda i, j: (0, 0, 0)),
                  pl.BlockSpec((1, D_POOL), lambda i, j: (0, 0)),
                  pl.BlockSpec((D_POOL, D_MODEL), lambda i, j: (0, 0))],
        out_specs=pl.BlockSpec((1, tt, D_MODEL), lambda i, j: (i, j, 0)),
        out_shape=jax.ShapeDtypeStruct((b, t, D_MODEL), BF16),
        compiler_params=_cparams(("parallel", "parallel")),
        name="pool_branch",
    )(u, pool_w, pool_scale, w_pool_out)


ROW_TILE = (D_MODEL // LANES, LANES)


def _to_row_tiles(x):
    return x.astype(BF16).reshape((x.shape[0],) + ROW_TILE)


def _from_row_tiles(x):
    return x.reshape(x.shape[0], D_MODEL)


def _merge_kernel(yp_ref, yr_ref, gp_ref, gr_ref, x_ref, gt_ref, sc_ref, sh_ref, g_ref, w_ref, rw_ref,
                  x1_ref, h_ref, lg_ref):
    m = (_sigmoid(gp_ref[0].astype(F32)) * yp_ref[0].astype(F32)
         + _sigmoid(gr_ref[0].astype(F32)) * yr_ref[0].astype(F32))
    x1 = x_ref[0] + gt_ref[0] * _dot(m.astype(BF16), w_ref[...])
    x1_ref[0] = x1
    ms = jnp.mean(x1 * x1, axis=-1, keepdims=True)
    h = x1 * lax.rsqrt(ms + NORM_EPS) * g_ref[...]
    h = h * (1.0 + sc_ref[0]) + sh_ref[0]
    h_ref[0] = _to_row_tiles(h)
    hh, hl = _split2(h)
    rh, rl = _split2(rw_ref[...])
    lg_ref[0] = _dot(hh, rh) + _dot(hl, rh) + _dot(hh, rl)


def _merge(y_pool, y_rwkv, gates, x, gt1, sc2, sh2, g2, w_out, router_w_pad):
    b, t, d = x.shape
    tt = 256
    tile = pl.BlockSpec((1, tt, d), lambda i, j: (i, j, 0))
    mod = pl.BlockSpec((1, 1, d), lambda i, j: (i, 0, 0))
    return pl.pallas_call(
        _merge_kernel,
        grid=(b, t // tt),
        in_specs=[tile, tile,
                  pl.BlockSpec((1, tt, d), lambda i, j: (i, j, 0)),
                  pl.BlockSpec((1, tt, d), lambda i, j: (i, j, 1)),
                  tile, mod, mod, mod,
                  pl.BlockSpec((1, d), lambda i, j: (0, 0)),
                  pl.BlockSpec((d, d), lambda i, j: (0, 0)),
                  pl.BlockSpec((d, LANES), lambda i, j: (0, 0))],
        out_specs=[tile,
                   pl.BlockSpec((1, tt) + ROW_TILE, lambda i, j: (i, j, 0, 0)),
                   pl.BlockSpec((1, tt, LANES), lambda i, j: (i, j, 0))],
        out_shape=[jax.ShapeDtypeStruct((b, t, d), F32),
                   jax.ShapeDtypeStruct((b, t) + ROW_TILE, BF16),
                   jax.ShapeDtypeStruct((b, t, LANES), F32)],
        compiler_params=_cparams(("parallel", "parallel")),
        name="merge",
    )(y_pool, y_rwkv, gates, gates, x, gt1, sc2, sh2, g2.reshape(1, d), w_out, router_w_pad)


def _router_kernel(lg_ref, bias_ref, e_ref, rk_ref, w_ref, cnt_ref, carry):
    tt = lg_ref.shape[0]
    shape = (tt, LANES)
    lane = _iota(shape, 1)
    valid = lane < N_EXPERTS
    grp = (lane & (N_EXPERTS - 1)) >> 3
    neg = jnp.float32(-jnp.inf)

    @pl.when(pl.program_id(0) == 0)
    def _():
        carry[...] = jnp.zeros_like(carry)

    scores = _sigmoid(lg_ref[...])
    sel = scores + bias_ref[...]
    sel = jnp.where(valid, sel, pltpu.roll(sel, N_EXPERTS, 1))

    def group_reduce(x, op):
        for sh in (1, 2, 4):
            up = pltpu.roll(x, sh, 1)
            dn = pltpu.roll(x, LANES - sh, 1)
            x = op(x, jnp.where((lane & sh) != 0, up, dn))
        return x

    m1 = group_reduce(sel, jnp.maximum)
    first = group_reduce(jnp.where(sel == m1, lane, LANES), jnp.minimum)
    m2 = group_reduce(jnp.where(lane == first, neg, sel), jnp.maximum)
    gs = m1 + m2
    beaten = jnp.zeros(shape, I32)
    for k in range(1, N_GROUPS):
        other = pltpu.roll(gs, 8 * k, 1)
        og = (grp - k) & (N_GROUPS - 1)
        beaten = beaten + ((other > gs) | ((other == gs) & (og < grp))).astype(I32)
    cur = jnp.where((beaten < TOPK_GROUPS) & valid, sel, neg)

    picked = jnp.zeros(shape, jnp.bool_)
    e_acc = jnp.zeros(shape, I32)
    w_acc = jnp.zeros(shape, F32)
    idxs = []
    for k in range(TOP_K):
        m = jnp.max(cur, axis=1, keepdims=True)
        idx = jnp.min(jnp.where(cur == m, lane, LANES), axis=1, keepdims=True)
        oh = lane == idx
        sc = jnp.sum(jnp.where(oh, scores, 0.0), axis=1, keepdims=True)
        e_acc = jnp.where(lane == k, idx, e_acc)
        w_acc = jnp.where(lane == k, sc, w_acc)
        picked = picked | oh
        cur = jnp.where(oh, neg, cur)
        idxs.append(idx)
    wsum = jnp.sum(w_acc, axis=1, keepdims=True)
    w_ref[...] = w_acc / wsum * ROUTED_SCALE
    e_ref[...] = e_acc

    lower = (_iota((tt, tt), 1) < _iota((tt, tt), 0)).astype(BF16)
    pk = picked.astype(BF16)
    before = _dot(lower, pk) + carry[...]
    r_acc = jnp.zeros(shape, F32)
    for k in range(TOP_K):
        rk = jnp.sum(jnp.where(lane == idxs[k], before, 0.0), axis=1, keepdims=True)
        r_acc = jnp.where(lane == k, rk, r_acc)
    rk_ref[...] = r_acc.astype(I32)
    carry[...] = carry[...] + jnp.sum(picked.astype(F32), axis=0, keepdims=True)
    cnt_ref[...] = carry[...]


def _router(logits, bias_pad):
    n = logits.shape[0]
    tt = min(1024, n)
    tile = pl.BlockSpec((tt, LANES), lambda i: (i, 0))
    row = pl.BlockSpec((1, LANES), lambda i: (0, 0))
    return pl.pallas_call(
        _router_kernel,
        grid=(n // tt,),
        in_specs=[tile, row],
        out_specs=[tile, tile, tile, row],
        out_shape=[jax.ShapeDtypeStruct((n, LANES), I32), jax.ShapeDtypeStruct((n, LANES), I32),
                   jax.ShapeDtypeStruct((n, LANES), F32), jax.ShapeDtypeStruct((1, LANES), F32)],
        scratch_shapes=[pltpu.VMEM((1, LANES), F32)],
        compiler_params=_cparams(("arbitrary",)),
        name="router",
    )(logits, bias_pad)


def _swiglu(xb, wg, wu, wd):
    g = _dot(xb, wg.astype(BF16))
    u = _dot(xb, wu.astype(BF16))
    act = (g * _sigmoid(g) * u).astype(BF16)
    return _dot(act, wd.astype(BF16))


def _dispatch_kernel(e_ref, rk_ref, ps_ref, cnt_ref, h_ref, wg_ref, wu_ref, wd_ref, xs_ref, sh_ref, zbuf, sem, zsem,
                     *, tt):
    bm = EXPERT_BLOCK

    @pl.when(pl.program_id(0) == 0)
    def _():
        zbuf[...] = jnp.zeros_like(zbuf)

        def tail_copy(e):
            last = ps_ref[e] + ((cnt_ref[e] + bm - 1) & -bm) - bm
            return pltpu.make_async_copy(zbuf, xs_ref.at[pl.ds(pl.multiple_of(last, bm), bm)], zsem)

        def tail_start(e, c):
            @pl.when((cnt_ref[e] & (bm - 1)) != 0)
            def _():
                tail_copy(e).start()
            return c

        def tail_wait(e, c):
            @pl.when((cnt_ref[e] & (bm - 1)) != 0)
            def _():
                tail_copy(e).wait()
            return c

        lax.fori_loop(0, N_EXPERTS, tail_start, 0)
        lax.fori_loop(0, N_EXPERTS, tail_wait, 0)

    def row_copy(src_row, dst_row):
        return pltpu.make_async_copy(h_ref.at[pl.ds(src_row, 1)], xs_ref.at[pl.ds(dst_row, 1)], sem)

    def start(t, c):
        for k in range(TOP_K):
            j = t * TOP_K + k
            row_copy(t, ps_ref[e_ref[j]] + rk_ref[j]).start(priority=k % 2)
        return c

    lax.fori_loop(0, tt, start, 0)

    sh_ref[...] = _to_row_tiles(_swiglu(_from_row_tiles(h_ref[...]), wg_ref[...], wu_ref[...], wd_ref[...]))

    for _ in range(TOP_K):
        pltpu.make_async_copy(h_ref, xs_ref.at[pl.ds(0, tt)], sem).wait()


def _dispatch(e_flat, rk_flat, pstarts, counts, h_rows, cap, wg, wu, wd):
    n = h_rows.shape[0]
    tt = min(512, n)
    smem_blk = pl.BlockSpec((tt * TOP_K,), lambda i: (i,), memory_space=pltpu.SMEM)
    rows = pl.BlockSpec((tt,) + ROW_TILE, lambda i: (i, 0, 0))

    def const(w):
        return pl.BlockSpec(w.shape, lambda i: (0, 0), pipeline_mode=pl.Buffered(1))

    return pl.pallas_call(
        functools.partial(_dispatch_kernel, tt=tt),
        grid=(n // tt,),
        in_specs=[smem_blk, smem_blk,
                  pl.BlockSpec(memory_space=pltpu.SMEM),
                  pl.BlockSpec(memory_space=pltpu.SMEM),
                  rows, const(wg), const(wu), const(wd)],
        out_specs=[pl.BlockSpec(memory_space=pl.ANY), rows],
        out_shape=[jax.ShapeDtypeStruct((cap,) + ROW_TILE, h_rows.dtype),
                   jax.ShapeDtypeStruct((n,) + ROW_TILE, h_rows.dtype)],
        scratch_shapes=[pltpu.VMEM((EXPERT_BLOCK,) + ROW_TILE, h_rows.dtype),
                        pltpu.SemaphoreType.DMA(()), pltpu.SemaphoreType.DMA(())],
        compiler_params=_cparams(("arbitrary",)),
        name="dispatch",
    )(e_flat, rk_flat, pstarts, counts, h_rows, wg, wu, wd)


def _expert_kernel(be_ref, nu_ref, nx_ref, sl_ref, x_ref, wg_hbm, wu_hbm, wd_hbm, o_ref, wg_buf, wu_buf, wd_buf, sem):
    i = pl.program_id(0)
    used = i < nu_ref[0]
    e = be_ref[i]
    first = (i == 0) | (e != be_ref[jnp.maximum(i - 1, 0)])
    slot = sl_ref[i]
    nxt = nx_ref[i]

    def fetch(expert, s):
        return (pltpu.make_async_copy(wg_hbm.at[expert], wg_buf.at[s], sem.at[s, 0]),
                pltpu.make_async_copy(wu_hbm.at[expert], wu_buf.at[s], sem.at[s, 1]),
                pltpu.make_async_copy(wd_hbm.at[expert], wd_buf.at[s], sem.at[s, 2]))

    @pl.when(used & (i == 0))
    def _():
        for c in fetch(e, slot):
            c.start()

    @pl.when(used & first)
    def _():
        for c in fetch(e, slot):
            c.wait()

        @pl.when(nxt < N_EXPERTS)
        def _():
            for c in fetch(nxt, 1 - slot):
                c.start()

    @pl.when(used)
    def _():
        y = _swiglu(_from_row_tiles(x_ref[...]), wg_buf[slot], wu_buf[slot], wd_buf[slot])
        o_ref[...] = _to_row_tiles(y)

    @pl.when(jnp.logical_not(used))
    def _():
        o_ref[...] = jnp.zeros_like(o_ref)


def _experts(block_expert, n_used, next_expert, slot, xs, wg, wu, wd):
    cap = xs.shape[0]
    bm = EXPERT_BLOCK
    de = wg.shape[2]
    grid_spec = pltpu.PrefetchScalarGridSpec(
        num_scalar_prefetch=4,
        grid=(cap // bm,),
        in_specs=[pl.BlockSpec((bm,) + ROW_TILE, lambda i, be, nu, nx, sl: (jnp.minimum(i, nu[0] - 1), 0, 0)),
                  pl.BlockSpec(memory_space=pl.ANY),
                  pl.BlockSpec(memory_space=pl.ANY),
                  pl.BlockSpec(memory_space=pl.ANY)],
        out_specs=pl.BlockSpec((bm,) + ROW_TILE, lambda i, be, nu, nx, sl: (i, 0, 0)),
        scratch_shapes=[pltpu.VMEM((2, D_MODEL, de), F32), pltpu.VMEM((2, D_MODEL, de), F32),
                        pltpu.VMEM((2, de, D_MODEL), F32), pltpu.SemaphoreType.DMA((2, 3))],
    )
    return pl.pallas_call(
        _expert_kernel,
        grid_spec=grid_spec,
        out_shape=jax.ShapeDtypeStruct((cap,) + ROW_TILE, BF16),
        compiler_params=_cparams(("arbitrary",)),
        name="experts",
    )(block_expert, n_used, next_expert, slot, xs, wg, wu, wd)


def _combine_kernel(e_ref, rk_ref, en_ref, rkn_ref, ps_ref, ys_ref, w_ref, sh_ref, x_ref, gt_ref, g_ref, o_ref,
                    buf, sem, *, tt):
    step = pl.program_id(0) * pl.num_programs(1) + pl.program_id(1)
    n_steps = pl.num_programs(0) * pl.num_programs(1)
    slot = step & 1

    def gather(idx_ref, rank_ref, dst_slot):
        def start(t, c):
            for k in range(TOP_K):
                j = t * TOP_K + k
                pltpu.make_async_copy(ys_ref.at[pl.ds(ps_ref[idx_ref[j]] + rank_ref[j], 1)],
                                      buf.at[dst_slot, k, pl.ds(t, 1)], sem.at[dst_slot]).start(priority=k % 2)
            return c

        lax.fori_loop(0, tt, start, 0)

    @pl.when(step == 0)
    def _():
        gather(e_ref, rk_ref, 0)

    @pl.when(step + 1 < n_steps)
    def _():
        gather(en_ref, rkn_ref, 1 - slot)

    for k in range(TOP_K):
        pltpu.make_async_copy(ys_ref.at[pl.ds(0, tt)], buf.at[slot, k], sem.at[slot]).wait()

    w = w_ref[0]
    moe = _from_row_tiles(sh_ref[0]).astype(F32)
    for k in range(TOP_K):
        moe = moe + w[:, k:k + 1] * _from_row_tiles(buf[slot, k]).astype(F32)
    x2 = x_ref[0] + gt_ref[0] * moe
    ms = jnp.mean(x2 * x2, axis=-1, keepdims=True)
    o_ref[0] = x2 * lax.rsqrt(ms + NORM_EPS) * g_ref[...]


def _combine(e_flat, rk_flat, pstarts, ys, wsel, shared, x1, gt2, final_g):
    b, t, d = x1.shape
    tt = 128
    nt = t // tt
    last = b * nt - 1
    smem_blk = pl.BlockSpec((tt * TOP_K,), lambda i, j: (i * nt + j,), memory_space=pltpu.SMEM)
    smem_nxt = pl.BlockSpec((tt * TOP_K,), lambda i, j: (jnp.minimum(i * nt + j + 1, last),),
                            memory_space=pltpu.SMEM)
    return pl.pallas_call(
        functools.partial(_combine_kernel, tt=tt),
        grid=(b, nt),
        in_specs=[smem_blk, smem_blk, smem_nxt, smem_nxt,
                  pl.BlockSpec(memory_space=pltpu.SMEM),
                  pl.BlockSpec(memory_space=pl.ANY),
                  pl.BlockSpec((1, tt, LANES), lambda i, j: (i, j, 0)),
                  pl.BlockSpec((1, tt) + ROW_TILE, lambda i, j: (i, j, 0, 0)),
                  pl.BlockSpec((1, tt, d), lambda i, j: (i, j, 0)),
                  pl.BlockSpec((1, 1, d), lambda i, j: (i, 0, 0)),
                  pl.BlockSpec((1, d), lambda i, j: (0, 0))],
        out_specs=pl.BlockSpec((1, tt, d), lambda i, j: (i, j, 0)),
        out_shape=jax.ShapeDtypeStruct((b, t, d), F32),
        scratch_shapes=[pltpu.VMEM((2, TOP_K, tt) + ROW_TILE, BF16), pltpu.SemaphoreType.DMA((2,))],
        compiler_params=_cparams(("arbitrary", "arbitrary")),
        name="combine",
    )(e_flat, rk_flat, e_flat, rk_flat, pstarts, ys, wsel, shared, x1, gt2, final_g.reshape(1, d))


def _pad_rows(w, rows):
    return jnp.zeros((rows, w.shape[1]), w.dtype).at[:w.shape[0]].set(w)


def _lora_pad_cols(w):
    out = jnp.zeros(w.shape[:-1] + (D_LORA_PAD,), w.dtype)
    for i in range(4):
        out = out.at[..., i * LANES:i * LANES + D_LORA].set(w[..., i * D_LORA:(i + 1) * D_LORA])
    return out.at[..., 4 * LANES:].set(w[..., 4 * D_LORA:])


def kernel(x, c, ctx, c_ctx, norm1_g, norm2_g, ada_w, ada_b, w_in, shift_mu, pool_w, pool_scale, w_pool_out, decay_w0, decay_w2, iclr_a0, iclr_a2, gate_g2, k_k, k_a, r_k, lnx_w, lnx_b, w_rwkv_out, w_out, router_w, router_bias, exp_w_gate, exp_w_up, exp_w_down, shared_w_gate, shared_w_up, shared_w_down, final_g):
    B, T, D = x.shape
    TC = ctx.shape[1]
    n = B * T
    l = 0

    w_in_l = w_in[l]
    w_u = w_in_l[:, :D_POOL].astype(BF16)
    w_rkv = w_in_l[:, D_POOL:D_POOL + 3 * D_ATT].astype(BF16)
    w_lora = _lora_pad_cols(w_in_l[:, D_POOL + 3 * D_ATT:D_POOL + 3 * D_ATT + 4 * D_LORA + D_GATE_LORA]).astype(BF16)
    w_gates = w_in_l[:, D_POOL + 3 * D_ATT + 4 * D_LORA + D_GATE_LORA:].astype(BF16)
    mu = shift_mu[l]
    row = lambda a: a.reshape(1, -1)
    pw = {
        "mu_rkv": row(mu[:3 * D_ATT]),
        "mu_lora": row(_lora_pad_cols(mu[3 * D_ATT:])),
        "w2f": _pad_rows(decay_w2[l, 0], LANES).astype(BF16),
        "w2b": _pad_rows(decay_w2[l, 1], LANES).astype(BF16),
        "a2f": _pad_rows(iclr_a2[l, 0], LANES).astype(BF16),
        "a2b": _pad_rows(iclr_a2[l, 1], LANES).astype(BF16),
        "w0f": row(decay_w0[l, 0]), "w0b": row(decay_w0[l, 1]),
        "a0f": row(iclr_a0[l, 0]), "a0b": row(iclr_a0[l, 1]),
        "k_k": row(k_k[l]), "k_a": row(k_a[l]), "r_k": row(r_k[l]),
    }

    cstack = jnp.zeros((8, D), F32).at[:B].set(c).at[B].set(c_ctx)
    mod = _ada_mod(cstack, ada_w[l], ada_b[l])
    sh1, sc1, gt1, sh2, sc2, gt2 = [mod[:B, i * D:(i + 1) * D].reshape(B, 1, D) for i in range(6)]
    sh1c = jnp.broadcast_to(mod[B, 0:D].reshape(1, 1, D), (B, 1, D))
    sc1c = jnp.broadcast_to(mod[B, D:2 * D].reshape(1, 1, D), (B, 1, D))

    hc = _norm_mod(ctx, norm1_g[l], sc1c, sh1c, TC).reshape(B * TC, D)
    rkv_c = _matmul(hc, w_rkv, F32, 512, 512).reshape(B, TC, 3 * D_ATT)
    lora_c = _matmul(hc, w_lora, F32, 512, D_LORA_PAD).reshape(B, TC, D_LORA_PAD)
    pc = _prepare(rkv_c, lora_c, pw, grid_mode=False)
    flat = lambda a: a.reshape(B * N_PAIRS, a.shape[2], LANES)
    s0 = jnp.zeros((B * N_PAIRS, LANES, LANES), F32)
    vc = flat(pc[10])
    _, st_f = _scan(*[flat(a) for a in pc[0:5]], vc, s0, reverse=False, emit=False)
    _, st_b = _scan(*[flat(a) for a in pc[5:10]], vc, s0, reverse=True, emit=False)

    h = _norm_mod(x, norm1_g[l], sc1, sh1, 512).reshape(n, D)
    u = _matmul(h, w_u, F32, 2048, 512).reshape(B, T, D_POOL)
    rkv = _matmul(h, w_rkv, F32, 2048, 512).reshape(B, T, 3 * D_ATT)
    lora = _matmul(h, w_lora, F32, 2048, D_LORA_PAD).reshape(B, T, D_LORA_PAD)
    gates = _matmul(h, w_gates, BF16, 2048, 512).reshape(B, T, 2 * D)
    pp = _prepare(rkv, lora, pw, grid_mode=True)
    v_, bonus, gd = flat(pp[10]), pp[11], pp[12]
    y_f, _ = _scan(*[flat(a) for a in pp[0:5]], v_, st_f, reverse=False, emit=True)
    y_b, _ = _scan(*[flat(a) for a in pp[5:10]], v_, st_b, reverse=True, emit=True)
    y_rwkv = _readout(y_f.reshape(B, N_PAIRS, T, LANES), y_b.reshape(B, N_PAIRS, T, LANES), bonus, gd,
                      row(lnx_w[l]), row(lnx_b[l]), gate_g2[l].astype(BF16), w_rwkv_out[l].astype(BF16))
    y_pool = _pool_branch(u, pool_w[l].astype(BF16), row(pool_scale[l]), w_pool_out[l].astype(BF16))

    router_w_pad = jnp.zeros((D, LANES), F32).at[:, :N_EXPERTS].set(router_w[l])
    x1, h2, logits = _merge(y_pool, y_rwkv, gates, x, gt1, sc2, sh2, norm2_g[l],
                                w_out[l].astype(BF16), router_w_pad)

    bias_pad = jnp.zeros((1, LANES), F32).at[0, :N_EXPERTS].set(router_bias[l])
    e_idx, e_rank, wsel, counts = _router(logits.reshape(n, LANES), bias_pad)
    e_flat = e_idx[:, :TOP_K].reshape(-1)
    rk_flat = e_rank[:, :TOP_K].reshape(-1)
    bm = EXPERT_BLOCK
    cnt = counts[0, :N_EXPERTS].astype(I32)
    padded = (cnt + bm - 1) // bm * bm
    pend = jnp.cumsum(padded)
    pstarts = (pend - padded).astype(I32)
    cap = n * TOP_K + N_EXPERTS * bm
    n_blocks = cap // bm
    block_start = jnp.arange(n_blocks, dtype=I32) * bm
    block_expert = jnp.minimum(jnp.sum(block_start[:, None] >= pend[None, :], axis=-1), N_EXPERTS - 1).astype(I32)
    n_used = (pend[-1] // bm).astype(I32).reshape(1)
    has = cnt > 0
    ids = jnp.where(has, jnp.arange(N_EXPERTS, dtype=I32), N_EXPERTS)
    next_e = jnp.concatenate([lax.cummin(ids, axis=0, reverse=True)[1:], jnp.full((1,), N_EXPERTS, I32)])
    slot_e = (jnp.cumsum(has.astype(I32)) - 1) & 1
    next_expert = next_e[block_expert].astype(I32)
    slot = slot_e[block_expert].astype(I32)

    h2 = h2.reshape((n,) + ROW_TILE)
    xs, shared = _dispatch(e_flat, rk_flat, pstarts, cnt, h2, cap,
                           shared_w_gate[l], shared_w_up[l], shared_w_down[l])
    ys = _experts(block_expert, n_used, next_expert, slot, xs, exp_w_gate[l], exp_w_up[l], exp_w_down[l])
    return _combine(e_flat, rk_flat, pstarts, ys, wsel.reshape(B, T, LANES),
                    shared.reshape((B, T) + ROW_TILE), x1, gt2, final_g)
```

```python
import functools

import jax
import jax.numpy as jnp
from jax import lax
from jax.experimental import pallas as pl
from jax.experimental.pallas import tpu as pltpu

F32 = jnp.float32
BF16 = jnp.bfloat16
I32 = jnp.int32
U32 = jnp.uint32

D_MODEL = 2048
GRID_W = 64
POOL_WINDOWS = (2, 4, 8, 16)
POOL_GROUP = 256
D_POOL = 1024
HEAD = 64
N_HEADS = 32
N_PAIRS = N_HEADS // 2
D_ATT = 2048
D_LORA = 96
D_GATE_LORA = 256
D_LORA_PAD = 768
GN_EPS = 64e-5
NORM_EPS = 1e-6
N_EXPERTS = 64
TOP_K = 8
N_GROUPS = 8
TOPK_GROUPS = 4
D_EXPERT = 512
ROUTED_SCALE = 2.5
EXP_M05 = 0.6065306597126334

LANES = 128
CHUNK = 64
EXPERT_BLOCK = 512
VMEM_LIMIT = 56 * 1024 * 1024


def _cparams(sem):
    return pltpu.CompilerParams(dimension_semantics=sem, vmem_limit_bytes=VMEM_LIMIT)


def _dot(a, b):
    return jnp.dot(a, b, preferred_element_type=F32)


def _dot_nt(a, b):
    return lax.dot_general(a, b, (((1,), (1,)), ((), ())), preferred_element_type=F32)


def _bmm(a, b):
    return lax.dot_general(a, b, (((2,), (1,)), ((0,), (0,))), preferred_element_type=F32)


def _bmm_nt(a, b):
    return lax.dot_general(a, b, (((2,), (2,)), ((0,), (0,))), preferred_element_type=F32)


def _bmm_tn(a, b):
    return lax.dot_general(a, b, (((1,), (1,)), ((0,), (0,))), preferred_element_type=F32)


def _sigmoid(x):
    return 0.5 * jnp.tanh(0.5 * x) + 0.5


def _split2(x):
    hi = x.astype(BF16)
    lo = (x - hi.astype(F32)).astype(BF16)
    return hi, lo


def _iota(shape, axis):
    return lax.broadcasted_iota(I32, shape, axis)


def _ada_kernel(c_ref, w_ref, b_ref, o_ref):
    c = c_ref[...]
    a = c * _sigmoid(c)
    o_ref[...] = _dot(a.astype(BF16), w_ref[...].astype(BF16)) + b_ref[...]


def _ada_mod(cstack, ada_w, ada_b):
    m, d = cstack.shape
    n = ada_w.shape[1]
    tn = 1024
    return pl.pallas_call(
        _ada_kernel,
        grid=(n // tn,),
        in_specs=[pl.BlockSpec((m, d), lambda j: (0, 0)),
                  pl.BlockSpec((d, tn), lambda j: (0, j)),
                  pl.BlockSpec((1, tn), lambda j: (0, j))],
        out_specs=pl.BlockSpec((m, tn), lambda j: (0, j)),
        out_shape=jax.ShapeDtypeStruct((m, n), F32),
        compiler_params=_cparams(("parallel",)),
        name="ada_mod",
    )(cstack, ada_w, ada_b.reshape(1, n))


def _norm_mod_kernel(x_ref, g_ref, sc_ref, sh_ref, o_ref):
    x = x_ref[0]
    ms = jnp.mean(x * x, axis=-1, keepdims=True)
    y = x * lax.rsqrt(ms + NORM_EPS) * g_ref[...]
    o_ref[0] = (y * (1.0 + sc_ref[0]) + sh_ref[0]).astype(o_ref.dtype)


def _norm_mod(x, g, sc, sh, tt):
    b, t, d = x.shape
    return pl.pallas_call(
        _norm_mod_kernel,
        grid=(b, t // tt),
        in_specs=[pl.BlockSpec((1, tt, d), lambda i, j: (i, j, 0)),
                  pl.BlockSpec((1, d), lambda i, j: (0, 0)),
                  pl.BlockSpec((1, 1, d), lambda i, j: (i, 0, 0)),
                  pl.BlockSpec((1, 1, d), lambda i, j: (i, 0, 0))],
        out_specs=pl.BlockSpec((1, tt, d), lambda i, j: (i, j, 0)),
        out_shape=jax.ShapeDtypeStruct((b, t, d), BF16),
        compiler_params=_cparams(("parallel", "parallel")),
        name="norm_mod",
    )(x, g.reshape(1, d), sc, sh)


def _mm_kernel(a_ref, b_ref, o_ref):
    o_ref[...] = _dot(a_ref[...], b_ref[...]).astype(o_ref.dtype)


def _matmul(a, b, out_dtype, tm, tn):
    m, k = a.shape
    n = b.shape[1]
    tm = min(tm, m)
    return pl.pallas_call(
        _mm_kernel,
        grid=(m // tm, n // tn),
        in_specs=[pl.BlockSpec((tm, k), lambda i, j: (i, 0)),
                  pl.BlockSpec((k, tn), lambda i, j: (0, j))],
        out_specs=pl.BlockSpec((tm, tn), lambda i, j: (i, j)),
        out_shape=jax.ShapeDtypeStruct((m, n), out_dtype),
        compiler_params=_cparams(("parallel", "parallel")),
        name="matmul",
    )(a, b)


def _shift_grid(x, prev, nxt, first, last):
    tt, c = x.shape
    col = _iota((tt, c), 0) & (GRID_W - 1)
    m = _iota((tt, c), 1) & 3
    left = jnp.where(col == 0, 0.0, pltpu.roll(x, 1, 0))
    right = jnp.where(col == GRID_W - 1, 0.0, pltpu.roll(x, tt - 1, 0))
    prev = jnp.where(first, 0.0, prev)
    nxt = jnp.where(last, 0.0, nxt)
    if tt > GRID_W:
        up = jnp.concatenate([prev, x[:tt - GRID_W]], axis=0)
        down = jnp.concatenate([x[GRID_W:], nxt], axis=0)
    else:
        up, down = prev, nxt
    return jnp.where(m == 0, left, jnp.where(m == 1, right, jnp.where(m == 2, up, down)))


def _shift_seq(x, prev8, next8, first, last):
    t, c = x.shape
    row = _iota((t, c), 0)
    odd = (_iota((t, c), 1) & 1) == 1
    before = jnp.where(first, 0.0, prev8[7:8])
    after = jnp.where(last, 0.0, next8[0:1])
    prev = jnp.where(row == 0, before, pltpu.roll(x, 1, 0))
    nxt = jnp.where(row == t - 1, after, pltpu.roll(x, t - 1, 0))
    return jnp.where(odd, nxt, prev)


def _head_sum(x):
    w = 2 * LANES
    ones = (_iota((w, w), 0) >> 6 == _iota((w, w), 1) >> 6).astype(BF16)
    outs = []
    for c in range(x.shape[1] // w):
        hi, lo = _split2(x[:, c * w:(c + 1) * w])
        outs.append(_dot(hi, ones) + _dot(lo, ones))
    return jnp.concatenate(outs, axis=1)


def _prepare_kernel(*refs, grid_mode):
    (r_ref, rp_ref, rn_ref, k_ref, kp_ref, kn_ref, v_ref, vp_ref, vn_ref,
     l_ref, lp_ref, ln_ref) = refs[:12]
    rest = refs[12:]
    (mur_ref, muk_ref, muv_ref, mul_ref, w2f_ref, w2b_ref, a2f_ref, a2b_ref,
     w0f_ref, w0b_ref, a0f_ref, a0b_ref, kk_ref, ka_ref, rk_ref,
     oaf_ref, obf_ref, okf_ref, orf_ref, owf_ref, oab_ref, obb_ref, okb_ref, orb_ref, owb_ref,
     ov_ref, obon_ref, ogd_ref) = rest

    first = pl.program_id(1) == 0
    last = pl.program_id(1) == pl.num_programs(1) - 1
    shift = _shift_grid if grid_mode else _shift_seq

    def mix(x_ref, p_ref, n_ref, mu_ref):
        x = x_ref[0]
        return x + (shift(x, p_ref[0], n_ref[0], first, last) - x) * mu_ref[...]

    r = mix(r_ref, rp_ref, rn_ref, mur_ref)
    k = mix(k_ref, kp_ref, kn_ref, muk_ref)
    v = mix(v_ref, vp_ref, vn_ref, muv_ref)
    lo = mix(l_ref, lp_ref, ln_ref, mul_ref)

    th = jnp.tanh(lo[:, :2 * LANES]).astype(BF16)
    zf = w0f_ref[...] + _dot(th[:, :LANES], w2f_ref[...])
    zb = w0b_ref[...] + _dot(th[:, LANES:], w2b_ref[...])
    lwf = -EXP_M05 * _sigmoid(zf)
    lwb = -EXP_M05 * _sigmoid(zb)
    ad = lo[:, 2 * LANES:4 * LANES].astype(BF16)
    af = _sigmoid(a0f_ref[...] + _dot(ad[:, :LANES], a2f_ref[...]))
    ab = _sigmoid(a0b_ref[...] + _dot(ad[:, LANES:], a2b_ref[...]))
    ogd_ref[0] = lo[:, 4 * LANES:]

    kk = k * kk_ref[...]
    kk = kk * lax.rsqrt(_head_sum(kk * kk) + 1e-12)
    ka = ka_ref[...]
    kf = k * (1.0 + (af - 1.0) * ka)
    kb = k * (1.0 + (ab - 1.0) * ka)
    bonus = _head_sum(r * (kf + kb) * rk_ref[...]) * v

    tt = r.shape[0]
    t2 = _iota((tt, tt), 0)
    s2 = _iota((tt, tt), 1)
    same = (t2 >> 6) == (s2 >> 6)

    def scan_operands(lw, b, kd, reverse):
        tri = (same & ((s2 >= t2) if reverse else (s2 <= t2))).astype(BF16)
        hi = lw.astype(BF16)
        r1 = lw - hi.astype(F32)
        mid = r1.astype(BF16)
        lo3 = (r1 - mid.astype(F32)).astype(BF16)
        cum = _dot(tri, hi) + _dot(tri, mid) + _dot(tri, lo3)
        w_inc = jnp.exp(cum)
        w_inv = jnp.exp(-cum)
        ends = [w_inc[c * CHUNK:c * CHUNK + 1] if reverse else w_inc[(c + 1) * CHUNK - 1:(c + 1) * CHUNK]
                for c in range(tt // CHUNK)]
        w_last = jnp.concatenate(ends + [jnp.zeros((8 - len(ends), lw.shape[1]), F32)], axis=0)
        return jnp.exp(cum - lw) * kk, b * w_inv, kd * w_inv, w_inc * r, w_last

    al_f, be_f, ka_f, rh_f, wl_f = scan_operands(lwf, kk * af, kf, False)
    al_b, be_b, ka_b, rh_b, wl_b = scan_operands(lwb, kk * ab, kb, True)
    outs = ((oaf_ref, al_f), (obf_ref, be_f), (okf_ref, ka_f), (orf_ref, rh_f), (owf_ref, wl_f),
            (oab_ref, al_b), (obb_ref, be_b), (okb_ref, ka_b), (orb_ref, rh_b), (owb_ref, wl_b),
            (ov_ref, v), (obon_ref, bonus))
    for o_ref, val in outs:
        for p in range(N_PAIRS):
            o_ref[0, p] = val[:, p * LANES:(p + 1) * LANES].astype(o_ref.dtype)


def _prepare(rkv, lora, pw, grid_mode):
    b, t, _ = rkv.shape
    d = D_ATT
    tt = 2 * GRID_W
    halo = GRID_W if grid_mode else 8
    hpt = tt // halo
    nhalo = t // halo
    grid = (b, t // tt)

    def tile_specs(w, c):
        return [pl.BlockSpec((1, tt, w), lambda i, j: (i, j, c)),
                pl.BlockSpec((1, halo, w), lambda i, j: (i, jnp.maximum(j * hpt - 1, 0), c)),
                pl.BlockSpec((1, halo, w), lambda i, j: (i, jnp.minimum((j + 1) * hpt, nhalo - 1), c))]

    in_specs = []
    args = []
    for c in range(3):
        in_specs += tile_specs(d, c)
        args += [rkv, rkv, rkv]
    in_specs += tile_specs(D_LORA_PAD, 0)
    args += [lora, lora, lora]

    def vec(c=0, w=d):
        return pl.BlockSpec((1, w), lambda i, j, c=c: (0, c))

    def full(shape):
        return pl.BlockSpec(shape, lambda i, j: (0,) * len(shape))

    in_specs += [vec(0), vec(1), vec(2), vec(0, D_LORA_PAD)]
    args += [pw["mu_rkv"], pw["mu_rkv"], pw["mu_rkv"], pw["mu_lora"]]
    in_specs += [full((LANES, d))] * 4
    args += [pw["w2f"], pw["w2b"], pw["a2f"], pw["a2b"]]
    in_specs += [vec()] * 7
    args += [pw["w0f"], pw["w0b"], pw["a0f"], pw["a0b"], pw["k_k"], pw["k_a"], pw["r_k"]]

    pair_spec = pl.BlockSpec((1, N_PAIRS, tt, LANES), lambda i, j: (i, 0, j, 0))
    wl_spec = pl.BlockSpec((1, N_PAIRS, 8, LANES), lambda i, j: (i, 0, j, 0))

    def pair(dtype):
        return jax.ShapeDtypeStruct((b, N_PAIRS, t, LANES), dtype)

    wl = jax.ShapeDtypeStruct((b, N_PAIRS, t // tt * 8, LANES), F32)
    direction = [pair(BF16)] * 4 + [wl]
    out_shape = direction * 2 + [pair(BF16), pair(F32), jax.ShapeDtypeStruct((b, t, D_GATE_LORA), F32)]
    out_specs = ([pair_spec] * 4 + [wl_spec]) * 2 + [pair_spec, pair_spec,
                                                    pl.BlockSpec((1, tt, D_GATE_LORA), lambda i, j: (i, j, 0))]
    return pl.pallas_call(
        functools.partial(_prepare_kernel, grid_mode=grid_mode),
        grid=grid, in_specs=in_specs, out_specs=out_specs, out_shape=out_shape,
        compiler_params=_cparams(("parallel", "parallel")),
        name="prepare_grid" if grid_mode else "prepare_seq",
    )(*args)


def _scan_kernel(*refs, reverse, pairs, tb, emit):
    al_ref, be_ref, ka_ref, rh_ref, wl_ref, v_ref, s0_ref = refs[:7]
    if emit:
        y_ref, st_ref, s_scr = refs[7:]
    else:
        st_ref, s_scr = refs[7:]
    L = CHUNK
    n_chunks = tb // L

    @pl.when(pl.program_id(1) == 0)
    def _():
        s_scr[...] = s0_ref[...]

    t2 = _iota((2 * L, 2 * L), 0)
    s2 = _iota((2 * L, 2 * L), 1)
    same = (t2 >> 6) == (s2 >> 6)
    tl = t2 & (L - 1)
    sl = s2 & (L - 1)
    strict = same & ((sl > tl) if reverse else (sl < tl))
    incl = same & ((sl >= tl) if reverse else (sl <= tl))
    eye = (t2 == s2).astype(F32)
    head_a = _iota((pairs, L, LANES), 2) < HEAD
    zero = jnp.zeros((), BF16)

    def stack(ref, rows):
        x = ref[:, rows, :]
        return jnp.concatenate([jnp.where(head_a, x, zero), jnp.where(head_a, zero, x)], axis=1)

    def chunk(ci, carry):
        cc = (n_chunks - 1 - ci) if reverse else ci
        rows = pl.ds(pl.multiple_of(cc * L, L), L)
        a_s = stack(al_ref, rows)
        b_s = stack(be_ref, rows)
        k_s = stack(ka_ref, rows)
        v_s = stack(v_ref, rows)
        lhs = jnp.concatenate([a_s, stack(rh_ref, rows)], axis=1) if emit else a_s
        xb = _bmm_nt(lhs, b_s)
        xk = _bmm_nt(lhs, k_s)
        m1 = jnp.where(strict, xb[:, :2 * L], 0.0)
        m2 = jnp.where(strict, xk[:, :2 * L], 0.0)
        xp = -m1
        tinv = eye + xp
        for _ in range(5):
            xq = xp.astype(BF16)
            xp = _bmm(xq, xq)
            tinv = tinv + _bmm(tinv.astype(BF16), xp.astype(BF16))
        s = s_scr[...]
        xs = _bmm_nt(lhs, s.astype(BF16))
        if emit:
            n1 = jnp.where(incl, xb[:, 2 * L:], 0.0)
            n2 = jnp.where(incl, xk[:, 2 * L:], 0.0)
            xv = _bmm(jnp.concatenate([m2, n2], axis=1).astype(BF16), v_s)
        else:
            xv = _bmm(m2.astype(BF16), v_s)
        g = xs[:, :2 * L] + xv[:, :2 * L]
        u_s = (-_bmm(tinv.astype(BF16), g.astype(BF16))).astype(BF16)
        if emit:
            y2 = xs[:, 2 * L:] + xv[:, 2 * L:] + _bmm(n1.astype(BF16), u_s)
            y_ref[:, rows, :] = y2[:, :L] + y2[:, L:]
        w_last = wl_ref[:, pl.ds((cc >> 1) * 8 + (cc & 1), 1), :]
        upd = _bmm_tn(jnp.concatenate([u_s, v_s], axis=1), jnp.concatenate([b_s, k_s], axis=1))
        s_scr[...] = (s + upd) * w_last
        return carry

    lax.fori_loop(0, n_chunks, chunk, 0)

    @pl.when(pl.program_id(1) == pl.num_programs(1) - 1)
    def _():
        st_ref[...] = s_scr[...]


def _scan(al, be, ka, rh, wl, v, s0, reverse, emit, pairs=32, tb=256):
    bp, t, _ = al.shape
    pairs = min(pairs, bp)
    tb = min(t, tb)
    nb = t // tb

    def tmap(g, c):
        return (g, (nb - 1 - c) if reverse else c, 0)

    data = pl.BlockSpec((pairs, tb, LANES), tmap)
    wl_spec = pl.BlockSpec((pairs, tb // (2 * CHUNK) * 8, LANES), tmap)
    state = pl.BlockSpec((pairs, LANES, LANES), lambda g, c: (g, 0, 0))
    out_shape = [jax.ShapeDtypeStruct((bp, LANES, LANES), F32)]
    out_specs = [state]
    if emit:
        out_shape = [jax.ShapeDtypeStruct((bp, t, LANES), F32)] + out_shape
        out_specs = [data] + out_specs
    res = pl.pallas_call(
        functools.partial(_scan_kernel, reverse=reverse, pairs=pairs, tb=tb, emit=emit),
        grid=(bp // pairs, nb),
        in_specs=[data] * 4 + [wl_spec, data, state],
        out_specs=out_specs, out_shape=out_shape,
        scratch_shapes=[pltpu.VMEM((pairs, LANES, LANES), F32)],
        compiler_params=_cparams(("parallel", "arbitrary")),
        name="scan_" + ("bwd" if reverse else "fwd") + ("_emit" if emit else "_state"),
    )(al, be, ka, rh, wl, v, s0)
    return (res[0], res[1]) if emit else (None, res[0])


def _readout_kernel(yf_ref, yb_ref, bon_ref, gd_ref, lnw_ref, lnb_ref, g2_ref, w_ref, o_ref):
    y = jnp.concatenate([yf_ref[0, p] + yb_ref[0, p] for p in range(N_PAIRS)], axis=1)
    bonus = jnp.concatenate([bon_ref[0, p] for p in range(N_PAIRS)], axis=1)
    mean = _head_sum(y) * (1.0 / HEAD)
    dlt = y - mean
    var = _head_sum(dlt * dlt) * (1.0 / HEAD)
    yn = dlt * lax.rsqrt(var + GN_EPS) * lnw_ref[...] + lnb_ref[...]
    gate = _dot(_sigmoid(gd_ref[0]).astype(BF16), g2_ref[...])
    out = ((yn + bonus) * gate).astype(BF16)
    o_ref[0] = _dot(out, w_ref[...]).astype(o_ref.dtype)


def _readout(yf, yb, bonus, gd, lnw, lnb, g2, w_out):
    b, _, t, _ = yf.shape
    tt = 256
    d = D_ATT
    pair_spec = pl.BlockSpec((1, N_PAIRS, tt, LANES), lambda i, j: (i, 0, j, 0))
    return pl.pallas_call(
        _readout_kernel,
        grid=(b, t // tt),
        in_specs=[pair_spec, pair_spec, pair_spec,
                  pl.BlockSpec((1, tt, D_GATE_LORA), lambda i, j: (i, j, 0)),
                  pl.BlockSpec((1, d), lambda i, j: (0, 0)),
                  pl.BlockSpec((1, d), lambda i, j: (0, 0)),
                  pl.BlockSpec((D_GATE_LORA, d), lambda i, j: (0, 0)),
                  pl.BlockSpec((d, D_MODEL), lambda i, j: (0, 0))],
        out_specs=pl.BlockSpec((1, tt, D_MODEL), lambda i, j: (i, j, 0)),
        out_shape=jax.ShapeDtypeStruct((b, t, D_MODEL), BF16),
        compiler_params=_cparams(("parallel", "parallel")),
        name="readout",
    )(yf, yb, bonus, gd, lnw, lnb, g2, w_out)


def _pool_kernel(u_ref, pw_ref, ps_ref, wo_ref, o_ref):
    u = u_ref[0]
    tt = u.shape[0]
    t2 = _iota((tt, tt), 0)
    s2 = _iota((tt, tt), 1)
    same = (t2 >> 6) == (s2 >> 6)
    tc = t2 & (GRID_W - 1)
    sc = s2 & (GRID_W - 1)
    col = _iota((tt, POOL_GROUP), 0) & (GRID_W - 1)
    ys = []
    for gi, w in enumerate(POOL_WINDOWS):
        ug = u[:, gi * POOL_GROUP:(gi + 1) * POOL_GROUP]
        win = (same & (sc >= tc - w // 2) & (sc < tc + (w - w // 2))).astype(BF16)
        hi, lo = _split2(ug)
        wsum = _dot(win, hi) + _dot(win, lo)
        cnt = (jnp.minimum(col + (w - w // 2), GRID_W) - jnp.maximum(col - w // 2, 0)).astype(F32)
        dlt = wsum / cnt - ug
        ys.append(_dot(dlt.astype(BF16), pw_ref[gi]))
    y1 = jnp.concatenate(ys, axis=1) * ps_ref[...]
    o_ref[0] = _dot(y1.astype(BF16), wo_ref[...]).astype(o_ref.dtype)


def _pool_branch(u, pool_w, pool_scale, w_pool_out):
    b, t, _ = u.shape
    tt = 256
    return pl.pallas_call(
        _pool_kernel,
        grid=(b, t // tt),
        in_specs=[pl.BlockSpec((1, tt, D_POOL), lambda i, j: (i, j, 0)),
                  pl.BlockSpec((4, POOL_GROUP, POOL_GROUP), lambda i, j: (0, 0, 0)),
                  pl.BlockSpec((1, D_POOL), lambda i, j: (0, 0)),
                  pl.BlockSpec((D_POOL, D_MODEL), lambda i, j: (0, 0))],
        out_specs=pl.BlockSpec((1, tt, D_MODEL), lambda i, j: (i, j, 0)),
        out_shape=jax.ShapeDtypeStruct((b, t, D_MODEL), BF16),
        compiler_params=_cparams(("parallel", "parallel")),
        name="pool_branch",
    )(u, pool_w, pool_scale, w_pool_out)


ROW_TILE = (D_MODEL // LANES, LANES)


def _to_row_tiles(x):
    return x.astype(BF16).reshape((x.shape[0],) + ROW_TILE)


def _from_row_tiles(x):
    return x.reshape(x.shape[0], D_MODEL)


def _merge_kernel(yp_ref, yr_ref, gp_ref, gr_ref, x_ref, gt_ref, sc_ref, sh_ref, g_ref, w_ref, rw_ref,
                  x1_ref, h_ref, lg_ref):
    m = (_sigmoid(gp_ref[0].astype(F32)) * yp_ref[0].astype(F32)
         + _sigmoid(gr_ref[0].astype(F32)) * yr_ref[0].astype(F32))
    x1 = x_ref[0] + gt_ref[0] * _dot(m.astype(BF16), w_ref[...])
    x1_ref[0] = x1
    ms = jnp.mean(x1 * x1, axis=-1, keepdims=True)
    h = x1 * lax.rsqrt(ms + NORM_EPS) * g_ref[...]
    h = h * (1.0 + sc_ref[0]) + sh_ref[0]
    h_ref[0] = _to_row_tiles(h)
    hh, hl = _split2(h)
    rh, rl = _split2(rw_ref[...])
    lg_ref[0] = _dot(hh, rh) + _dot(hl, rh) + _dot(hh, rl)


def _merge(y_pool, y_rwkv, gates, x, gt1, sc2, sh2, g2, w_out, router_w_pad):
    b, t, d = x.shape
    tt = 256
    tile = pl.BlockSpec((1, tt, d), lambda i, j: (i, j, 0))
    mod = pl.BlockSpec((1, 1, d), lambda i, j: (i, 0, 0))
    return pl.pallas_call(
        _merge_kernel,
        grid=(b, t // tt),
        in_specs=[tile, tile,
                  pl.BlockSpec((1, tt, d), lambda i, j: (i, j, 0)),
                  pl.BlockSpec((1, tt, d), lambda i, j: (i, j, 1)),
                  tile, mod, mod, mod,
                  pl.BlockSpec((1, d), lambda i, j: (0, 0)),
                  pl.BlockSpec((d, d), lambda i, j: (0, 0)),
                  pl.BlockSpec((d, LANES), lambda i, j: (0, 0))],
        out_specs=[tile,
                   pl.BlockSpec((1, tt) + ROW_TILE, lambda i, j: (i, j, 0, 0)),
                   pl.BlockSpec((1, tt, LANES), lambda i, j: (i, j, 0))],
        out_shape=[jax.ShapeDtypeStruct((b, t, d), F32),
                   jax.ShapeDtypeStruct((b, t) + ROW_TILE, BF16),
                   jax.ShapeDtypeStruct((b, t, LANES), F32)],
        compiler_params=_cparams(("parallel", "parallel")),
        name="merge",
    )(y_pool, y_rwkv, gates, gates, x, gt1, sc2, sh2, g2.reshape(1, d), w_out, router_w_pad)


def _router_kernel(lg_ref, bias_ref, e_ref, rk_ref, w_ref, cnt_ref, carry):
    tt = lg_ref.shape[0]
    shape = (tt, LANES)
    lane = _iota(shape, 1)
    valid = lane < N_EXPERTS
    grp = (lane & (N_EXPERTS - 1)) >> 3
    neg = jnp.float32(-jnp.inf)

    @pl.when(pl.program_id(0) == 0)
    def _():
        carry[...] = jnp.zeros_like(carry)

    scores = _sigmoid(lg_ref[...])
    sel = scores + bias_ref[...]
    sel = jnp.where(valid, sel, pltpu.roll(sel, N_EXPERTS, 1))

    def group_reduce(x, op):
        for sh in (1, 2, 4):
            up = pltpu.roll(x, sh, 1)
            dn = pltpu.roll(x, LANES - sh, 1)
            x = op(x, jnp.where((lane & sh) != 0, up, dn))
        return x

    m1 = group_reduce(sel, jnp.maximum)
    first = group_reduce(jnp.where(sel == m1, lane, LANES), jnp.minimum)
    m2 = group_reduce(jnp.where(lane == first, neg, sel), jnp.maximum)
    gs = m1 + m2
    beaten = jnp.zeros(shape, I32)
    for k in range(1, N_GROUPS):
        other = pltpu.roll(gs, 8 * k, 1)
        og = (grp - k) & (N_GROUPS - 1)
        beaten = beaten + ((other > gs) | ((other == gs) & (og < grp))).astype(I32)
    cur = jnp.where((beaten < TOPK_GROUPS) & valid, sel, neg)

    picked = jnp.zeros(shape, jnp.bool_)
    e_acc = jnp.zeros(shape, I32)
    w_acc = jnp.zeros(shape, F32)
    idxs = []
    for k in range(TOP_K):
        m = jnp.max(cur, axis=1, keepdims=True)
        idx = jnp.min(jnp.where(cur == m, lane, LANES), axis=1, keepdims=True)
        oh = lane == idx
        sc = jnp.sum(jnp.where(oh, scores, 0.0), axis=1, keepdims=True)
        e_acc = jnp.where(lane == k, idx, e_acc)
        w_acc = jnp.where(lane == k, sc, w_acc)
        picked = picked | oh
        cur = jnp.where(oh, neg, cur)
        idxs.append(idx)
    wsum = jnp.sum(w_acc, axis=1, keepdims=True)
    w_ref[...] = w_acc / wsum * ROUTED_SCALE
    e_ref[...] = e_acc

    lower = (_iota((tt, tt), 1) < _iota((tt, tt), 0)).astype(BF16)
    pk = picked.astype(BF16)
    before = _dot(lower, pk) + carry[...]
    r_acc = jnp.zeros(shape, F32)
    for k in range(TOP_K):
        rk = jnp.sum(jnp.where(lane == idxs[k], before, 0.0), axis=1, keepdims=True)
        r_acc = jnp.where(lane == k, rk, r_acc)
    rk_ref[...] = r_acc.astype(I32)
    carry[...] = carry[...] + jnp.sum(picked.astype(F32), axis=0, keepdims=True)
    cnt_ref[...] = carry[...]


def _router(logits, bias_pad):
    n = logits.shape[0]
    tt = min(1024, n)
    tile = pl.BlockSpec((tt, LANES), lambda i: (i, 0))
    row = pl.BlockSpec((1, LANES), lambda i: (0, 0))
    return pl.pallas_call(
        _router_kernel,
        grid=(n // tt,),
        in_specs=[tile, row],
        out_specs=[tile, tile, tile, row],
        out_shape=[jax.ShapeDtypeStruct((n, LANES), I32), jax.ShapeDtypeStruct((n, LANES), I32),
                   jax.ShapeDtypeStruct((n, LANES), F32), jax.ShapeDtypeStruct((1, LANES), F32)],
        scratch_shapes=[pltpu.VMEM((1, LANES), F32)],
        compiler_params=_cparams(("arbitrary",)),
        name="router",
    )(logits, bias_pad)


def _swiglu(xb, wg, wu, wd):
    g = _dot(xb, wg.astype(BF16))
    u = _dot(xb, wu.astype(BF16))
    act = (g * _sigmoid(g) * u).astype(BF16)
    return _dot(act, wd.astype(BF16))


def _dispatch_kernel(e_ref, rk_ref, ps_ref, cnt_ref, h_ref, wg_ref, wu_ref, wd_ref, xs_ref, sh_ref, zbuf, sem, zsem,
                     *, tt):
    bm = EXPERT_BLOCK

    @pl.when(pl.program_id(0) == 0)
    def _():
        zbuf[...] = jnp.zeros_like(zbuf)

        def tail_copy(e):
            last = ps_ref[e] + ((cnt_ref[e] + bm - 1) & -bm) - bm
            return pltpu.make_async_copy(zbuf, xs_ref.at[pl.ds(pl.multiple_of(last, bm), bm)], zsem)

        def tail_start(e, c):
            @pl.when((cnt_ref[e] & (bm - 1)) != 0)
            def _():
                tail_copy(e).start()
            return c

        def tail_wait(e, c):
            @pl.when((cnt_ref[e] & (bm - 1)) != 0)
            def _():
                tail_copy(e).wait()
            return c

        lax.fori_loop(0, N_EXPERTS, tail_start, 0)
        lax.fori_loop(0, N_EXPERTS, tail_wait, 0)

    def row_copy(src_row, dst_row):
        return pltpu.make_async_copy(h_ref.at[pl.ds(src_row, 1)], xs_ref.at[pl.ds(dst_row, 1)], sem)

    def start(t, c):
        for k in range(TOP_K):
            j = t * TOP_K + k
            row_copy(t, ps_ref[e_ref[j]] + rk_ref[j]).start(priority=k % 2)
        return c

    lax.fori_loop(0, tt, start, 0)

    sh_ref[...] = _to_row_tiles(_swiglu(_from_row_tiles(h_ref[...]), wg_ref[...], wu_ref[...], wd_ref[...]))

    for _ in range(TOP_K):
        pltpu.make_async_copy(h_ref, xs_ref.at[pl.ds(0, tt)], sem).wait()


def _dispatch(e_flat, rk_flat, pstarts, counts, h_rows, cap, wg, wu, wd):
    n = h_rows.shape[0]
    tt = min(512, n)
    smem_blk = pl.BlockSpec((tt * TOP_K,), lambda i: (i,), memory_space=pltpu.SMEM)
    rows = pl.BlockSpec((tt,) + ROW_TILE, lambda i: (i, 0, 0))

    def const(w):
        return pl.BlockSpec(w.shape, lambda i: (0, 0), pipeline_mode=pl.Buffered(1))

    return pl.pallas_call(
        functools.partial(_dispatch_kernel, tt=tt),
        grid=(n // tt,),
        in_specs=[smem_blk, smem_blk,
                  pl.BlockSpec(memory_space=pltpu.SMEM),
                  pl.BlockSpec(memory_space=pltpu.SMEM),
                  rows, const(wg), const(wu), const(wd)],
        out_specs=[pl.BlockSpec(memory_space=pl.ANY), rows],
        out_shape=[jax.ShapeDtypeStruct((cap,) + ROW_TILE, h_rows.dtype),
                   jax.ShapeDtypeStruct((n,) + ROW_TILE, h_rows.dtype)],
        scratch_shapes=[pltpu.VMEM((EXPERT_BLOCK,) + ROW_TILE, h_rows.dtype),
                        pltpu.SemaphoreType.DMA(()), pltpu.SemaphoreType.DMA(())],
        compiler_params=_cparams(("arbitrary",)),
        name="dispatch",
    )(e_flat, rk_flat, pstarts, counts, h_rows, wg, wu, wd)


def _expert_kernel(be_ref, nu_ref, nx_ref, sl_ref, x_ref, wg_hbm, wu_hbm, wd_hbm, o_ref, wg_buf, wu_buf, wd_buf, sem):
    i = pl.program_id(0)
    used = i < nu_ref[0]
    e = be_ref[i]
    first = (i == 0) | (e != be_ref[jnp.maximum(i - 1, 0)])
    slot = sl_ref[e]
    nxt = nx_ref[e]

    def fetch(expert, s):
        return (pltpu.make_async_copy(wg_hbm.at[expert], wg_buf.at[s], sem.at[s, 0]),
                pltpu.make_async_copy(wu_hbm.at[expert], wu_buf.at[s], sem.at[s, 1]),
                pltpu.make_async_copy(wd_hbm.at[expert], wd_buf.at[s], sem.at[s, 2]))

    @pl.when(used & (i == 0))
    def _():
        for c in fetch(e, slot):
            c.start()

    @pl.when(used & first)
    def _():
        for c in fetch(e, slot):
            c.wait()

        @pl.when(nxt < N_EXPERTS)
        def _():
            for c in fetch(nxt, 1 - slot):
                c.start()

    @pl.when(used)
    def _():
        y = _swiglu(_from_row_tiles(x_ref[...]), wg_buf[slot], wu_buf[slot], wd_buf[slot])
        o_ref[...] = _to_row_tiles(y)

    @pl.when(jnp.logical_not(used))
    def _():
        o_ref[...] = jnp.zeros_like(o_ref)


def _experts(block_expert, n_used, next_expert, slot, xs, wg, wu, wd):
    cap = xs.shape[0]
    bm = EXPERT_BLOCK
    de = wg.shape[2]
    grid_spec = pltpu.PrefetchScalarGridSpec(
        num_scalar_prefetch=4,
        grid=(cap // bm,),
        in_specs=[pl.BlockSpec((bm,) + ROW_TILE, lambda i, be, nu, nx, sl: (jnp.minimum(i, nu[0] - 1), 0, 0)),
                  pl.BlockSpec(memory_space=pl.ANY),
                  pl.BlockSpec(memory_space=pl.ANY),
                  pl.BlockSpec(memory_space=pl.ANY)],
        out_specs=pl.BlockSpec((bm,) + ROW_TILE, lambda i, be, nu, nx, sl: (i, 0, 0)),
        scratch_shapes=[pltpu.VMEM((2, D_MODEL, de), F32), pltpu.VMEM((2, D_MODEL, de), F32),
                        pltpu.VMEM((2, de, D_MODEL), F32), pltpu.SemaphoreType.DMA((2, 3))],
    )
    return pl.pallas_call(
        _expert_kernel,
        grid_spec=grid_spec,
        out_shape=jax.ShapeDtypeStruct((cap,) + ROW_TILE, BF16),
        compiler_params=_cparams(("arbitrary",)),
        name="experts",
    )(block_expert, n_used, next_expert, slot, xs, wg, wu, wd)


def _combine_kernel(e_ref, rk_ref, en_ref, rkn_ref, ps_ref, ys_ref, w_ref, sh_ref, x_ref, gt_ref, g_ref, o_ref,
                    buf, sem, *, tt):
    step = pl.program_id(0) * pl.num_programs(1) + pl.program_id(1)
    n_steps = pl.num_programs(0) * pl.num_programs(1)
    slot = step & 1

    def gather(idx_ref, rank_ref, dst_slot):
        def start(t, c):
            for k in range(TOP_K):
                j = t * TOP_K + k
                pltpu.make_async_copy(ys_ref.at[pl.ds(ps_ref[idx_ref[j]] + rank_ref[j], 1)],
                                      buf.at[dst_slot, k, pl.ds(t, 1)], sem.at[dst_slot]).start(priority=k % 2)
            return c

        lax.fori_loop(0, tt, start, 0)

    @pl.when(step == 0)
    def _():
        gather(e_ref, rk_ref, 0)

    @pl.when(step + 1 < n_steps)
    def _():
        gather(en_ref, rkn_ref, 1 - slot)

    for k in range(TOP_K):
        pltpu.make_async_copy(ys_ref.at[pl.ds(0, tt)], buf.at[slot, k], sem.at[slot]).wait()

    w = w_ref[0]
    moe = _from_row_tiles(sh_ref[0]).astype(F32)
    for k in range(TOP_K):
        moe = moe + w[:, k:k + 1] * _from_row_tiles(buf[slot, k]).astype(F32)
    x2 = x_ref[0] + gt_ref[0] * moe
    ms = jnp.mean(x2 * x2, axis=-1, keepdims=True)
    o_ref[0] = x2 * lax.rsqrt(ms + NORM_EPS) * g_ref[...]


def _combine(e_flat, rk_flat, pstarts, ys, wsel, shared, x1, gt2, final_g):
    b, t, d = x1.shape
    tt = 128
    nt = t // tt
    last = b * nt - 1
    smem_blk = pl.BlockSpec((tt * TOP_K,), lambda i, j: (i * nt + j,), memory_space=pltpu.SMEM)
    smem_nxt = pl.BlockSpec((tt * TOP_K,), lambda i, j: (jnp.minimum(i * nt + j + 1, last),),
                            memory_space=pltpu.SMEM)
    return pl.pallas_call(
        functools.partial(_combine_kernel, tt=tt),
        grid=(b, nt),
        in_specs=[smem_blk, smem_blk, smem_nxt, smem_nxt,
                  pl.BlockSpec(memory_space=pltpu.SMEM),
                  pl.BlockSpec(memory_space=pl.ANY),
                  pl.BlockSpec((1, tt, LANES), lambda i, j: (i, j, 0)),
                  pl.BlockSpec((1, tt) + ROW_TILE, lambda i, j: (i, j, 0, 0)),
                  pl.BlockSpec((1, tt, d), lambda i, j: (i, j, 0)),
                  pl.BlockSpec((1, 1, d), lambda i, j: (i, 0, 0)),
                  pl.BlockSpec((1, d), lambda i, j: (0, 0))],
        out_specs=pl.BlockSpec((1, tt, d), lambda i, j: (i, j, 0)),
        out_shape=jax.ShapeDtypeStruct((b, t, d), F32),
        scratch_shapes=[pltpu.VMEM((2, TOP_K, tt) + ROW_TILE, BF16), pltpu.SemaphoreType.DMA((2,))],
        compiler_params=_cparams(("arbitrary", "arbitrary")),
        name="combine",
    )(e_flat, rk_flat, e_flat, rk_flat, pstarts, ys, wsel, shared, x1, gt2, final_g.reshape(1, d))


def _pad_rows(w, rows):
    return jnp.zeros((rows, w.shape[1]), w.dtype).at[:w.shape[0]].set(w)


def _lora_pad_cols(w):
    out = jnp.zeros(w.shape[:-1] + (D_LORA_PAD,), w.dtype)
    for i in range(4):
        out = out.at[..., i * LANES:i * LANES + D_LORA].set(w[..., i * D_LORA:(i + 1) * D_LORA])
    return out.at[..., 4 * LANES:].set(w[..., 4 * D_LORA:])


def kernel(x, c, ctx, c_ctx, norm1_g, norm2_g, ada_w, ada_b, w_in, shift_mu, pool_w, pool_scale, w_pool_out, decay_w0, decay_w2, iclr_a0, iclr_a2, gate_g2, k_k, k_a, r_k, lnx_w, lnx_b, w_rwkv_out, w_out, router_w, router_bias, exp_w_gate, exp_w_up, exp_w_down, shared_w_gate, shared_w_up, shared_w_down, final_g):
    B, T, D = x.shape
    TC = ctx.shape[1]
    n = B * T
    l = 0

    w_in_l = w_in[l]
    w_u = w_in_l[:, :D_POOL].astype(BF16)
    w_rkv = w_in_l[:, D_POOL:D_POOL + 3 * D_ATT].astype(BF16)
    w_lora = _lora_pad_cols(w_in_l[:, D_POOL + 3 * D_ATT:D_POOL + 3 * D_ATT + 4 * D_LORA + D_GATE_LORA]).astype(BF16)
    w_gates = w_in_l[:, D_POOL + 3 * D_ATT + 4 * D_LORA + D_GATE_LORA:].astype(BF16)
    mu = shift_mu[l]
    row = lambda a: a.reshape(1, -1)
    pw = {
        "mu_rkv": row(mu[:3 * D_ATT]),
        "mu_lora": row(_lora_pad_cols(mu[3 * D_ATT:])),
        "w2f": _pad_rows(decay_w2[l, 0], LANES).astype(BF16),
        "w2b": _pad_rows(decay_w2[l, 1], LANES).astype(BF16),
        "a2f": _pad_rows(iclr_a2[l, 0], LANES).astype(BF16),
        "a2b": _pad_rows(iclr_a2[l, 1], LANES).astype(BF16),
        "w0f": row(decay_w0[l, 0]), "w0b": row(decay_w0[l, 1]),
        "a0f": row(iclr_a0[l, 0]), "a0b": row(iclr_a0[l, 1]),
        "k_k": row(k_k[l]), "k_a": row(k_a[l]), "r_k": row(r_k[l]),
    }

    cstack = jnp.zeros((8, D), F32).at[:B].set(c).at[B].set(c_ctx)
    mod = _ada_mod(cstack, ada_w[l], ada_b[l])
    sh1, sc1, gt1, sh2, sc2, gt2 = [mod[:B, i * D:(i + 1) * D].reshape(B, 1, D) for i in range(6)]
    sh1c = jnp.broadcast_to(mod[B, 0:D].reshape(1, 1, D), (B, 1, D))
    sc1c = jnp.broadcast_to(mod[B, D:2 * D].reshape(1, 1, D), (B, 1, D))

    hc = _norm_mod(ctx, norm1_g[l], sc1c, sh1c, TC).reshape(B * TC, D)
    rkv_c = _matmul(hc, w_rkv, F32, 512, 512).reshape(B, TC, 3 * D_ATT)
    lora_c = _matmul(hc, w_lora, F32, 512, D_LORA_PAD).reshape(B, TC, D_LORA_PAD)
    pc = _prepare(rkv_c, lora_c, pw, grid_mode=False)
    flat = lambda a: a.reshape(B * N_PAIRS, a.shape[2], LANES)
    s0 = jnp.zeros((B * N_PAIRS, LANES, LANES), F32)
    vc = flat(pc[10])
    _, st_f = _scan(*[flat(a) for a in pc[0:5]], vc, s0, reverse=False, emit=False)
    _, st_b = _scan(*[flat(a) for a in pc[5:10]], vc, s0, reverse=True, emit=False)

    h = _norm_mod(x, norm1_g[l], sc1, sh1, 512).reshape(n, D)
    u = _matmul(h, w_u, F32, 2048, 512).reshape(B, T, D_POOL)
    rkv = _matmul(h, w_rkv, F32, 2048, 1024).reshape(B, T, 3 * D_ATT)
    lora = _matmul(h, w_lora, F32, 2048, D_LORA_PAD).reshape(B, T, D_LORA_PAD)
    gates = _matmul(h, w_gates, BF16, 2048, 1024).reshape(B, T, 2 * D)
    pp = _prepare(rkv, lora, pw, grid_mode=True)
    v_, bonus, gd = flat(pp[10]), pp[11], pp[12]
    y_f, _ = _scan(*[flat(a) for a in pp[0:5]], v_, st_f, reverse=False, emit=True)
    y_b, _ = _scan(*[flat(a) for a in pp[5:10]], v_, st_b, reverse=True, emit=True)
    y_rwkv = _readout(y_f.reshape(B, N_PAIRS, T, LANES), y_b.reshape(B, N_PAIRS, T, LANES), bonus, gd,
                      row(lnx_w[l]), row(lnx_b[l]), gate_g2[l].astype(BF16), w_rwkv_out[l].astype(BF16))
    y_pool = _pool_branch(u, pool_w[l].astype(BF16), row(pool_scale[l]), w_pool_out[l].astype(BF16))

    router_w_pad = jnp.zeros((D, LANES), F32).at[:, :N_EXPERTS].set(router_w[l])
    x1, h2, logits = _merge(y_pool, y_rwkv, gates, x, gt1, sc2, sh2, norm2_g[l],
                                w_out[l].astype(BF16), router_w_pad)

    bias_pad = jnp.zeros((1, LANES), F32).at[0, :N_EXPERTS].set(router_bias[l])
    e_idx, e_rank, wsel, counts = _router(logits.reshape(n, LANES), bias_pad)
    e_flat = e_idx[:, :TOP_K].reshape(-1)
    rk_flat = e_rank[:, :TOP_K].reshape(-1)
    bm = EXPERT_BLOCK
    cnt = counts[0, :N_EXPERTS].astype(I32)
    padded = (cnt + bm - 1) // bm * bm
    pend = jnp.cumsum(padded)
    pstarts = (pend - padded).astype(I32)
    cap = n * TOP_K + N_EXPERTS * bm
    n_blocks = cap // bm
    block_start = jnp.arange(n_blocks, dtype=I32) * bm
    block_expert = jnp.minimum(jnp.sum(block_start[:, None] >= pend[None, :], axis=-1), N_EXPERTS - 1).astype(I32)
    n_used = (pend[-1] // bm).astype(I32).reshape(1)
    has = cnt > 0
    ids = jnp.where(has, jnp.arange(N_EXPERTS, dtype=I32), N_EXPERTS)
    next_e = jnp.concatenate([lax.cummin(ids, axis=0, reverse=True)[1:], jnp.full((1,), N_EXPERTS, I32)])
    slot_e = ((jnp.cumsum(has.astype(I32)) - 1) & 1).astype(I32)

    h2 = h2.reshape((n,) + ROW_TILE)
    xs, shared = _dispatch(e_flat, rk_flat, pstarts, cnt, h2, cap,
                           shared_w_gate[l], shared_w_up[l], shared_w_down[l])
    ys = _experts(block_expert, n_used, next_e, slot_e, xs, exp_w_gate[l], exp_w_up[l], exp_w_down[l])
    return _combine(e_flat, rk_flat, pstarts, ys, wsel.reshape(B, T, LANES),
                    shared.reshape((B, T) + ROW_TILE), x1, gt2, final_g)
```

```python
import functools

import jax
import jax.numpy as jnp
from jax import lax
from jax.experimental import pallas as pl
from jax.experimental.pallas import tpu as pltpu

F32 = jnp.float32
BF16 = jnp.bfloat16
I32 = jnp.int32

D_MODEL = 2048
GRID_W = 64
POOL_WINDOWS = (2, 4, 8, 16)
POOL_GROUP = 256
D_POOL = 1024
HEAD = 64
N_HEADS = 32
N_PAIRS = N_HEADS // 2
D_ATT = 2048
D_LORA = 96
D_GATE_LORA = 256
D_LORA_PAD = 768
GN_EPS = 64e-5
NORM_EPS = 1e-6
N_EXPERTS = 64
TOP_K = 8
N_GROUPS = 8
TOPK_GROUPS = 4
D_EXPERT = 512
ROUTED_SCALE = 2.5
EXP_M05 = 0.6065306597126334

LANES = 128
CHUNK = 64
EXPERT_BLOCK = 512
VMEM_LIMIT = 56 * 1024 * 1024


def _cparams(sem):
    return pltpu.CompilerParams(dimension_semantics=sem, vmem_limit_bytes=VMEM_LIMIT)


def _dot(a, b):
    return jnp.dot(a, b, preferred_element_type=F32)


def _bmm(a, b):
    return lax.dot_general(a, b, (((2,), (1,)), ((0,), (0,))), preferred_element_type=F32)


def _bmm_nt(a, b):
    return lax.dot_general(a, b, (((2,), (2,)), ((0,), (0,))), preferred_element_type=F32)


def _bmm_tn(a, b):
    return lax.dot_general(a, b, (((1,), (1,)), ((0,), (0,))), preferred_element_type=F32)


def _sigmoid(x):
    return 0.5 * jnp.tanh(0.5 * x) + 0.5


def _split2(x):
    hi = x.astype(BF16)
    lo = (x - hi.astype(F32)).astype(BF16)
    return hi, lo


def _iota(shape, axis):
    return lax.broadcasted_iota(I32, shape, axis)


def _ada_kernel(c_ref, w_ref, b_ref, o_ref):
    c = c_ref[...]
    a = c * _sigmoid(c)
    o_ref[...] = _dot(a.astype(BF16), w_ref[...].astype(BF16)) + b_ref[...]


def _ada_mod(cstack, ada_w, ada_b):
    m, d = cstack.shape
    n = ada_w.shape[1]
    tn = 1024
    return pl.pallas_call(
        _ada_kernel,
        grid=(n // tn,),
        in_specs=[pl.BlockSpec((m, d), lambda j: (0, 0)),
                  pl.BlockSpec((d, tn), lambda j: (0, j)),
                  pl.BlockSpec((1, tn), lambda j: (0, j))],
        out_specs=pl.BlockSpec((m, tn), lambda j: (0, j)),
        out_shape=jax.ShapeDtypeStruct((m, n), F32),
        compiler_params=_cparams(("parallel",)),
        name="ada_mod",
    )(cstack, ada_w, ada_b.reshape(1, n))


def _norm_mod_kernel(x_ref, g_ref, sc_ref, sh_ref, o_ref):
    x = x_ref[0]
    ms = jnp.mean(x * x, axis=-1, keepdims=True)
    y = x * lax.rsqrt(ms + NORM_EPS) * g_ref[...]
    o_ref[0] = (y * (1.0 + sc_ref[0]) + sh_ref[0]).astype(o_ref.dtype)


def _norm_mod(x, g, sc, sh, tt):
    b, t, d = x.shape
    return pl.pallas_call(
        _norm_mod_kernel,
        grid=(b, t // tt),
        in_specs=[pl.BlockSpec((1, tt, d), lambda i, j: (i, j, 0)),
                  pl.BlockSpec((1, d), lambda i, j: (0, 0)),
                  pl.BlockSpec((1, 1, d), lambda i, j: (i, 0, 0)),
                  pl.BlockSpec((1, 1, d), lambda i, j: (i, 0, 0))],
        out_specs=pl.BlockSpec((1, tt, d), lambda i, j: (i, j, 0)),
        out_shape=jax.ShapeDtypeStruct((b, t, d), BF16),
        compiler_params=_cparams(("parallel", "parallel")),
        name="norm_mod",
    )(x, g.reshape(1, d), sc, sh)


def _mm_kernel(a_ref, b_ref, o_ref):
    o_ref[...] = _dot(a_ref[...], b_ref[...]).astype(o_ref.dtype)


def _matmul(a, b, out_dtype, tm, tn):
    m, k = a.shape
    n = b.shape[1]
    tm = min(tm, m)
    return pl.pallas_call(
        _mm_kernel,
        grid=(m // tm, n // tn),
        in_specs=[pl.BlockSpec((tm, k), lambda i, j: (i, 0)),
                  pl.BlockSpec((k, tn), lambda i, j: (0, j))],
        out_specs=pl.BlockSpec((tm, tn), lambda i, j: (i, j)),
        out_shape=jax.ShapeDtypeStruct((m, n), out_dtype),
        compiler_params=_cparams(("parallel", "parallel")),
        name="matmul",
    )(a, b)


def _shift_grid(x, prev, nxt, first, last):
    tt, c = x.shape
    col = _iota((tt, c), 0) & (GRID_W - 1)
    m = _iota((tt, c), 1) & 3
    horiz = jnp.where(m == 0, pltpu.roll(x, 1, 0), pltpu.roll(x, tt - 1, 0))
    at_border = ((m == 0) & (col == 0)) | ((m == 1) & (col == GRID_W - 1))
    horiz = jnp.where(at_border, 0.0, horiz)
    prev = jnp.where(first, 0.0, prev)
    nxt = jnp.where(last, 0.0, nxt)
    if tt > GRID_W:
        up = jnp.concatenate([prev, x[:tt - GRID_W]], axis=0)
        down = jnp.concatenate([x[GRID_W:], nxt], axis=0)
    else:
        up, down = prev, nxt
    return jnp.where(m < 2, horiz, jnp.where(m == 2, up, down))


def _shift_seq(x, prev8, next8, first, last):
    t, c = x.shape
    row = _iota((t, c), 0)
    odd = (_iota((t, c), 1) & 1) == 1
    before = jnp.where(first, 0.0, prev8[7:8])
    after = jnp.where(last, 0.0, next8[0:1])
    prev = jnp.where(row == 0, before, pltpu.roll(x, 1, 0))
    nxt = jnp.where(row == t - 1, after, pltpu.roll(x, t - 1, 0))
    return jnp.where(odd, nxt, prev)


def _head_sum(x):
    w = 2 * LANES
    ones = (_iota((w, w), 0) >> 6 == _iota((w, w), 1) >> 6).astype(BF16)
    outs = []
    for c in range(x.shape[1] // w):
        hi, lo = _split2(x[:, c * w:(c + 1) * w])
        outs.append(_dot(hi, ones) + _dot(lo, ones))
    return jnp.concatenate(outs, axis=1)


def _prepare_kernel(*refs, grid_mode):
    (r_ref, rp_ref, rn_ref, k_ref, kp_ref, kn_ref, v_ref, vp_ref, vn_ref,
     l_ref, lp_ref, ln_ref) = refs[:12]
    rest = refs[12:]
    (mur_ref, muk_ref, muv_ref, mul_ref, w2f_ref, w2b_ref, a2f_ref, a2b_ref,
     w0f_ref, w0b_ref, a0f_ref, a0b_ref, kk_ref, ka_ref, rk_ref,
     oaf_ref, obf_ref, okf_ref, orf_ref, owf_ref, oab_ref, obb_ref, okb_ref, orb_ref, owb_ref,
     ov_ref, obon_ref, ogd_ref) = rest

    first = pl.program_id(1) == 0
    last = pl.program_id(1) == pl.num_programs(1) - 1
    shift = _shift_grid if grid_mode else _shift_seq

    def mix(x_ref, p_ref, n_ref, mu_ref):
        x = x_ref[0]
        return x + (shift(x, p_ref[0], n_ref[0], first, last) - x) * mu_ref[...]

    r = mix(r_ref, rp_ref, rn_ref, mur_ref)
    k = mix(k_ref, kp_ref, kn_ref, muk_ref)
    v = mix(v_ref, vp_ref, vn_ref, muv_ref)
    lo = mix(l_ref, lp_ref, ln_ref, mul_ref)

    th = jnp.tanh(lo[:, :2 * LANES]).astype(BF16)
    tzf = jnp.tanh(w0f_ref[...] + _dot(th[:, :LANES], w2f_ref[...]))
    tzb = jnp.tanh(w0b_ref[...] + _dot(th[:, LANES:], w2b_ref[...]))
    c = -0.5 * EXP_M05
    lwf = tzf * c + c
    lwb = tzb * c + c
    ad = lo[:, 2 * LANES:4 * LANES].astype(BF16)
    af = 0.5 * jnp.tanh(a0f_ref[...] + _dot(ad[:, :LANES], a2f_ref[...])) + 0.5
    ab = 0.5 * jnp.tanh(a0b_ref[...] + _dot(ad[:, LANES:], a2b_ref[...])) + 0.5
    ogd_ref[0] = lo[:, 4 * LANES:]

    kk = k * kk_ref[...]
    kk = kk * lax.rsqrt(_head_sum(kk * kk) + 1e-12)
    ka = ka_ref[...]
    one_minus_ka = 1.0 - ka
    kf = k * (af * ka + one_minus_ka)
    kb = k * (ab * ka + one_minus_ka)
    bonus = _head_sum(r * (kf + kb) * rk_ref[...]) * v

    tt = r.shape[0]
    t2 = _iota((tt, tt), 0)
    s2 = _iota((tt, tt), 1)
    same = (t2 >> 6) == (s2 >> 6)

    def scan_operands(lw, b, kd, reverse):
        tri = (same & ((s2 >= t2) if reverse else (s2 <= t2))).astype(BF16)
        hi = lw.astype(BF16)
        r1 = lw - hi.astype(F32)
        mid = r1.astype(BF16)
        lo3 = (r1 - mid.astype(F32)).astype(BF16)
        cum = _dot(tri, hi) + _dot(tri, mid) + _dot(tri, lo3)
        w_inc = jnp.exp(cum)
        w_inv = jnp.exp(-cum)
        ends = [w_inc[c * CHUNK:c * CHUNK + 1] if reverse else w_inc[(c + 1) * CHUNK - 1:(c + 1) * CHUNK]
                for c in range(tt // CHUNK)]
        w_last = jnp.concatenate(ends + [jnp.zeros((8 - len(ends), lw.shape[1]), F32)], axis=0)
        return jnp.exp(cum - lw) * kk, b * w_inv, kd * w_inv, w_inc * r, w_last

    al_f, be_f, ka_f, rh_f, wl_f = scan_operands(lwf, kk * af, kf, False)
    al_b, be_b, ka_b, rh_b, wl_b = scan_operands(lwb, kk * ab, kb, True)
    outs = ((oaf_ref, al_f), (obf_ref, be_f), (okf_ref, ka_f), (orf_ref, rh_f), (owf_ref, wl_f),
            (oab_ref, al_b), (obb_ref, be_b), (okb_ref, ka_b), (orb_ref, rh_b), (owb_ref, wl_b),
            (ov_ref, v), (obon_ref, bonus))
    for o_ref, val in outs:
        for p in range(N_PAIRS):
            o_ref[0, p] = val[:, p * LANES:(p + 1) * LANES].astype(o_ref.dtype)


def _prepare(rkv, lora, pw, grid_mode):
    b, t, _ = rkv.shape
    d = D_ATT
    tt = 2 * GRID_W
    halo = GRID_W if grid_mode else 8
    hpt = tt // halo
    nhalo = t // halo
    grid = (b, t // tt)

    def tile_specs(w, c):
        return [pl.BlockSpec((1, tt, w), lambda i, j: (i, j, c)),
                pl.BlockSpec((1, halo, w), lambda i, j: (i, jnp.maximum(j * hpt - 1, 0), c)),
                pl.BlockSpec((1, halo, w), lambda i, j: (i, jnp.minimum((j + 1) * hpt, nhalo - 1), c))]

    in_specs = []
    args = []
    for c in range(3):
        in_specs += tile_specs(d, c)
        args += [rkv, rkv, rkv]
    in_specs += tile_specs(D_LORA_PAD, 0)
    args += [lora, lora, lora]

    def vec(c=0, w=d):
        return pl.BlockSpec((1, w), lambda i, j, c=c: (0, c))

    def full(shape):
        return pl.BlockSpec(shape, lambda i, j: (0,) * len(shape))

    in_specs += [vec(0), vec(1), vec(2), vec(0, D_LORA_PAD)]
    args += [pw["mu_rkv"], pw["mu_rkv"], pw["mu_rkv"], pw["mu_lora"]]
    in_specs += [full((LANES, d))] * 4
    args += [pw["w2f"], pw["w2b"], pw["a2f"], pw["a2b"]]
    in_specs += [vec()] * 7
    args += [pw["w0f"], pw["w0b"], pw["a0f"], pw["a0b"], pw["k_k"], pw["k_a"], pw["r_k"]]

    pair_spec = pl.BlockSpec((1, N_PAIRS, tt, LANES), lambda i, j: (i, 0, j, 0))
    wl_spec = pl.BlockSpec((1, N_PAIRS, 8, LANES), lambda i, j: (i, 0, j, 0))

    def pair(dtype):
        return jax.ShapeDtypeStruct((b, N_PAIRS, t, LANES), dtype)

    wl = jax.ShapeDtypeStruct((b, N_PAIRS, t // tt * 8, LANES), F32)
    direction = [pair(BF16)] * 4 + [wl]
    out_shape = direction * 2 + [pair(BF16), pair(F32), jax.ShapeDtypeStruct((b, t, D_GATE_LORA), F32)]
    out_specs = ([pair_spec] * 4 + [wl_spec]) * 2 + [pair_spec, pair_spec,
                                                    pl.BlockSpec((1, tt, D_GATE_LORA), lambda i, j: (i, j, 0))]
    return pl.pallas_call(
        functools.partial(_prepare_kernel, grid_mode=grid_mode),
        grid=grid, in_specs=in_specs, out_specs=out_specs, out_shape=out_shape,
        compiler_params=_cparams(("parallel", "parallel")),
        name="prepare_grid" if grid_mode else "prepare_seq",
    )(*args)


def _scan_kernel(*refs, reverse, pairs, tb, emit):
    al_ref, be_ref, ka_ref, rh_ref, wl_ref, v_ref, s0_ref = refs[:7]
    if emit:
        y_ref, st_ref, s_scr = refs[7:]
    else:
        st_ref, s_scr = refs[7:]
    L = CHUNK
    n_chunks = tb // L

    @pl.when(pl.program_id(1) == 0)
    def _():
        s_scr[...] = s0_ref[...]

    t2 = _iota((2 * L, 2 * L), 0)
    s2 = _iota((2 * L, 2 * L), 1)
    same = (t2 >> 6) == (s2 >> 6)
    tl = t2 & (L - 1)
    sl = s2 & (L - 1)
    strict = same & ((sl > tl) if reverse else (sl < tl))
    incl = same & ((sl >= tl) if reverse else (sl <= tl))
    eye = (t2 == s2).astype(F32)
    head_a = _iota((pairs, L, LANES), 2) < HEAD
    zero = jnp.zeros((), BF16)

    def stack(ref, rows):
        x = ref[:, rows, :]
        return jnp.concatenate([jnp.where(head_a, x, zero), jnp.where(head_a, zero, x)], axis=1)

    def chunk(ci, carry):
        cc = (n_chunks - 1 - ci) if reverse else ci
        rows = pl.ds(pl.multiple_of(cc * L, L), L)
        a_s = stack(al_ref, rows)
        b_s = stack(be_ref, rows)
        k_s = stack(ka_ref, rows)
        v_s = stack(v_ref, rows)
        lhs = jnp.concatenate([a_s, stack(rh_ref, rows)], axis=1) if emit else a_s
        xb = _bmm_nt(lhs, b_s)
        xk = _bmm_nt(lhs, k_s)
        m1 = jnp.where(strict, xb[:, :2 * L], 0.0)
        m2 = jnp.where(strict, xk[:, :2 * L], 0.0)
        xp = -m1
        tinv = eye + xp
        for _ in range(5):
            xq = xp.astype(BF16)
            xp = _bmm(xq, xq)
            tinv = tinv + _bmm(tinv.astype(BF16), xp.astype(BF16))
        s = s_scr[...]
        xs = _bmm_nt(lhs, s.astype(BF16))
        if emit:
            n1 = jnp.where(incl, xb[:, 2 * L:], 0.0)
            n2 = jnp.where(incl, xk[:, 2 * L:], 0.0)
            xv = _bmm(jnp.concatenate([m2, n2], axis=1).astype(BF16), v_s)
        else:
            xv = _bmm(m2.astype(BF16), v_s)
        g = xs[:, :2 * L] + xv[:, :2 * L]
        u_s = (-_bmm(tinv.astype(BF16), g.astype(BF16))).astype(BF16)
        if emit:
            y2 = xs[:, 2 * L:] + xv[:, 2 * L:] + _bmm(n1.astype(BF16), u_s)
            y_ref[:, rows, :] = y2[:, :L] + y2[:, L:]
        w_last = wl_ref[:, pl.ds((cc >> 1) * 8 + (cc & 1), 1), :]
        upd = _bmm_tn(jnp.concatenate([u_s, v_s], axis=1), jnp.concatenate([b_s, k_s], axis=1))
        s_scr[...] = (s + upd) * w_last
        return carry

    lax.fori_loop(0, n_chunks, chunk, 0)

    @pl.when(pl.program_id(1) == pl.num_programs(1) - 1)
    def _():
        st_ref[...] = s_scr[...]


def _scan(al, be, ka, rh, wl, v, s0, reverse, emit, pairs=32, tb=256):
    bp, t, _ = al.shape
    pairs = min(pairs, bp)
    tb = min(t, tb)
    nb = t // tb

    def tmap(g, c):
        return (g, (nb - 1 - c) if reverse else c, 0)

    data = pl.BlockSpec((pairs, tb, LANES), tmap)
    wl_spec = pl.BlockSpec((pairs, tb // (2 * CHUNK) * 8, LANES), tmap)
    state = pl.BlockSpec((pairs, LANES, LANES), lambda g, c: (g, 0, 0))
    out_shape = [jax.ShapeDtypeStruct((bp, LANES, LANES), F32)]
    out_specs = [state]
    if emit:
        out_shape = [jax.ShapeDtypeStruct((bp, t, LANES), F32)] + out_shape
        out_specs = [data] + out_specs
    res = pl.pallas_call(
        functools.partial(_scan_kernel, reverse=reverse, pairs=pairs, tb=tb, emit=emit),
        grid=(bp // pairs, nb),
        in_specs=[data] * 4 + [wl_spec, data, state],
        out_specs=out_specs, out_shape=out_shape,
        scratch_shapes=[pltpu.VMEM((pairs, LANES, LANES), F32)],
        compiler_params=_cparams(("parallel", "arbitrary")),
        name="scan_" + ("bwd" if reverse else "fwd") + ("_emit" if emit else "_state"),
    )(al, be, ka, rh, wl, v, s0)
    return (res[0], res[1]) if emit else (None, res[0])


def _readout_kernel(yf_ref, yb_ref, bon_ref, gd_ref, lnw_ref, lnb_ref, g2_ref, w_ref, o_ref):
    y = jnp.concatenate([yf_ref[0, p] + yb_ref[0, p] for p in range(N_PAIRS)], axis=1)
    bonus = jnp.concatenate([bon_ref[0, p] for p in range(N_PAIRS)], axis=1)
    mean = _head_sum(y) * (1.0 / HEAD)
    dlt = y - mean
    var = _head_sum(dlt * dlt) * (1.0 / HEAD)
    yn = dlt * lax.rsqrt(var + GN_EPS) * lnw_ref[...] + lnb_ref[...]
    gate = _dot(_sigmoid(gd_ref[0]).astype(BF16), g2_ref[...])
    out = ((yn + bonus) * gate).astype(BF16)
    o_ref[0] = _dot(out, w_ref[...]).astype(o_ref.dtype)


def _readout(yf, yb, bonus, gd, lnw, lnb, g2, w_out):
    b, _, t, _ = yf.shape
    tt = 256
    d = D_ATT
    pair_spec = pl.BlockSpec((1, N_PAIRS, tt, LANES), lambda i, j: (i, 0, j, 0))
    return pl.pallas_call(
        _readout_kernel,
        grid=(b, t // tt),
        in_specs=[pair_spec, pair_spec, pair_spec,
                  pl.BlockSpec((1, tt, D_GATE_LORA), lambda i, j: (i, j, 0)),
                  pl.BlockSpec((1, d), lambda i, j: (0, 0)),
                  pl.BlockSpec((1, d), lambda i, j: (0, 0)),
                  pl.BlockSpec((D_GATE_LORA, d), lambda i, j: (0, 0)),
                  pl.BlockSpec((d, D_MODEL), lambda i, j: (0, 0))],
        out_specs=pl.BlockSpec((1, tt, D_MODEL), lambda i, j: (i, j, 0)),
        out_shape=jax.ShapeDtypeStruct((b, t, D_MODEL), BF16),
        compiler_params=_cparams(("parallel", "parallel")),
        name="readout",
    )(yf, yb, bonus, gd, lnw, lnb, g2, w_out)


def _pool_kernel(u_ref, pw_ref, ps_ref, wo_ref, o_ref):
    u = u_ref[0]
    tt = u.shape[0]
    t2 = _iota((tt, tt), 0)
    s2 = _iota((tt, tt), 1)
    same = (t2 >> 6) == (s2 >> 6)
    tc = t2 & (GRID_W - 1)
    sc = s2 & (GRID_W - 1)
    col = _iota((tt, POOL_GROUP), 0) & (GRID_W - 1)
    ys = []
    for gi, w in enumerate(POOL_WINDOWS):
        ug = u[:, gi * POOL_GROUP:(gi + 1) * POOL_GROUP]
        win = (same & (sc >= tc - w // 2) & (sc < tc + (w - w // 2))).astype(BF16)
        hi, lo = _split2(ug)
        wsum = _dot(win, hi) + _dot(win, lo)
        cnt = (jnp.minimum(col + (w - w // 2), GRID_W) - jnp.maximum(col - w // 2, 0)).astype(F32)
        dlt = wsum / cnt - ug
        ys.append(_dot(dlt.astype(BF16), pw_ref[gi]))
    y1 = jnp.concatenate(ys, axis=1) * ps_ref[...]
    o_ref[0] = _dot(y1.astype(BF16), wo_ref[...]).astype(o_ref.dtype)


def _pool_branch(u, pool_w, pool_scale, w_pool_out):
    b, t, _ = u.shape
    tt = 256
    return pl.pallas_call(
        _pool_kernel,
        grid=(b, t // tt),
        in_specs=[pl.BlockSpec((1, tt, D_POOL), lambda i, j: (i, j, 0)),
                  pl.BlockSpec((4, POOL_GROUP, POOL_GROUP), lambda i, j: (0, 0, 0)),
                  pl.BlockSpec((1, D_POOL), lambda i, j: (0, 0)),
                  pl.BlockSpec((D_POOL, D_MODEL), lambda i, j: (0, 0))],
        out_specs=pl.BlockSpec((1, tt, D_MODEL), lambda i, j: (i, j, 0)),
        out_shape=jax.ShapeDtypeStruct((b, t, D_MODEL), BF16),
        compiler_params=_cparams(("parallel", "parallel")),
        name="pool_branch",
    )(u, pool_w, pool_scale, w_pool_out)


ROW_TILE = (D_MODEL // LANES, LANES)


def _to_row_tiles(x):
    return x.astype(BF16).reshape((x.shape[0],) + ROW_TILE)


def _from_row_tiles(x):
    return x.reshape(x.shape[0], D_MODEL)


def _merge_kernel(yp_ref, yr_ref, gp_ref, gr_ref, x_ref, gt_ref, sc_ref, sh_ref, g_ref, w_ref, rw_ref,
                  x1_ref, h_ref, lg_ref):
    m = (_sigmoid(gp_ref[0].astype(F32)) * yp_ref[0].astype(F32)
         + _sigmoid(gr_ref[0].astype(F32)) * yr_ref[0].astype(F32))
    x1 = x_ref[0] + gt_ref[0] * _dot(m.astype(BF16), w_ref[...])
    x1_ref[0] = x1
    ms = jnp.mean(x1 * x1, axis=-1, keepdims=True)
    h = x1 * lax.rsqrt(ms + NORM_EPS) * g_ref[...]
    h = h * (1.0 + sc_ref[0]) + sh_ref[0]
    h_ref[0] = _to_row_tiles(h)
    hh, hl = _split2(h)
    rh, rl = _split2(rw_ref[...])
    lg_ref[0] = _dot(hh, rh) + _dot(hl, rh) + _dot(hh, rl)


def _merge(y_pool, y_rwkv, gates, x, gt1, sc2, sh2, g2, w_out, router_w_pad):
    b, t, d = x.shape
    tt = 256
    tile = pl.BlockSpec((1, tt, d), lambda i, j: (i, j, 0))
    mod = pl.BlockSpec((1, 1, d), lambda i, j: (i, 0, 0))
    return pl.pallas_call(
        _merge_kernel,
        grid=(b, t // tt),
        in_specs=[tile, tile,
                  pl.BlockSpec((1, tt, d), lambda i, j: (i, j, 0)),
                  pl.BlockSpec((1, tt, d), lambda i, j: (i, j, 1)),
                  tile, mod, mod, mod,
                  pl.BlockSpec((1, d), lambda i, j: (0, 0)),
                  pl.BlockSpec((d, d), lambda i, j: (0, 0)),
                  pl.BlockSpec((d, LANES), lambda i, j: (0, 0))],
        out_specs=[tile,
                   pl.BlockSpec((1, tt) + ROW_TILE, lambda i, j: (i, j, 0, 0)),
                   pl.BlockSpec((1, tt, LANES), lambda i, j: (i, j, 0))],
        out_shape=[jax.ShapeDtypeStruct((b, t, d), F32),
                   jax.ShapeDtypeStruct((b, t) + ROW_TILE, BF16),
                   jax.ShapeDtypeStruct((b, t, LANES), F32)],
        compiler_params=_cparams(("parallel", "parallel")),
        name="merge",
    )(y_pool, y_rwkv, gates, gates, x, gt1, sc2, sh2, g2.reshape(1, d), w_out, router_w_pad)


def _router_kernel(lg_ref, bias_ref, e_ref, rk_ref, w_ref, cnt_ref, carry):
    tt = lg_ref.shape[0]
    shape = (tt, LANES)
    lane = _iota(shape, 1)
    valid = lane < N_EXPERTS
    grp = (lane & (N_EXPERTS - 1)) >> 3
    neg = jnp.float32(-jnp.inf)

    @pl.when(pl.program_id(0) == 0)
    def _():
        carry[...] = jnp.zeros_like(carry)

    scores = _sigmoid(lg_ref[...])
    sel = scores + bias_ref[...]
    sel = jnp.where(valid, sel, pltpu.roll(sel, N_EXPERTS, 1))

    def group_reduce(x, op):
        for sh in (1, 2, 4):
            up = pltpu.roll(x, sh, 1)
            dn = pltpu.roll(x, LANES - sh, 1)
            x = op(x, jnp.where((lane & sh) != 0, up, dn))
        return x

    m1 = group_reduce(sel, jnp.maximum)
    first = group_reduce(jnp.where(sel == m1, lane, LANES), jnp.minimum)
    m2 = group_reduce(jnp.where(lane == first, neg, sel), jnp.maximum)
    gs = m1 + m2
    beaten = jnp.zeros(shape, I32)
    for k in range(1, N_GROUPS):
        other = pltpu.roll(gs, 8 * k, 1)
        og = (grp - k) & (N_GROUPS - 1)
        beaten = beaten + ((other > gs) | ((other == gs) & (og < grp))).astype(I32)
    cur = jnp.where((beaten < TOPK_GROUPS) & valid, sel, neg)

    picked = jnp.zeros(shape, jnp.bool_)
    e_acc = jnp.zeros(shape, I32)
    w_acc = jnp.zeros(shape, F32)
    idxs = []
    for k in range(TOP_K):
        m = jnp.max(cur, axis=1, keepdims=True)
        idx = jnp.min(jnp.where(cur == m, lane, LANES), axis=1, keepdims=True)
        oh = lane == idx
        sc = jnp.sum(jnp.where(oh, scores, 0.0), axis=1, keepdims=True)
        e_acc = jnp.where(lane == k, idx, e_acc)
        w_acc = jnp.where(lane == k, sc, w_acc)
        picked = picked | oh
        cur = jnp.where(oh, neg, cur)
        idxs.append(idx)
    wsum = jnp.sum(w_acc, axis=1, keepdims=True)
    w_ref[...] = w_acc / wsum * ROUTED_SCALE
    e_ref[...] = e_acc

    lower = (_iota((tt, tt), 1) < _iota((tt, tt), 0)).astype(BF16)
    pk = picked.astype(BF16)
    before = _dot(lower, pk) + carry[...]
    r_acc = jnp.zeros(shape, F32)
    for k in range(TOP_K):
        rk = jnp.sum(jnp.where(lane == idxs[k], before, 0.0), axis=1, keepdims=True)
        r_acc = jnp.where(lane == k, rk, r_acc)
    rk_ref[...] = r_acc.astype(I32)
    carry[...] = carry[...] + jnp.sum(picked.astype(F32), axis=0, keepdims=True)
    cnt_ref[...] = carry[...]


def _router(logits, bias_pad):
    n = logits.shape[0]
    tt = min(1024, n)
    tile = pl.BlockSpec((tt, LANES), lambda i: (i, 0))
    row = pl.BlockSpec((1, LANES), lambda i: (0, 0))
    return pl.pallas_call(
        _router_kernel,
        grid=(n // tt,),
        in_specs=[tile, row],
        out_specs=[tile, tile, tile, row],
        out_shape=[jax.ShapeDtypeStruct((n, LANES), I32), jax.ShapeDtypeStruct((n, LANES), I32),
                   jax.ShapeDtypeStruct((n, LANES), F32), jax.ShapeDtypeStruct((1, LANES), F32)],
        scratch_shapes=[pltpu.VMEM((1, LANES), F32)],
        compiler_params=_cparams(("arbitrary",)),
        name="router",
    )(logits, bias_pad)


def _swiglu(xb, wg, wu, wd):
    g = _dot(xb, wg.astype(BF16))
    u = _dot(xb, wu.astype(BF16))
    act = (g * _sigmoid(g) * u).astype(BF16)
    return _dot(act, wd.astype(BF16))


def _dispatch_kernel(e_ref, rk_ref, ps_ref, cnt_ref, h_ref, wg_ref, wu_ref, wd_ref, xs_ref, sh_ref, zbuf, sem, zsem,
                     *, tt):
    bm = EXPERT_BLOCK

    @pl.when(pl.program_id(0) == 0)
    def _():
        zbuf[...] = jnp.zeros_like(zbuf)

        def tail_copy(e):
            last = ps_ref[e] + ((cnt_ref[e] + bm - 1) & -bm) - bm
            return pltpu.make_async_copy(zbuf, xs_ref.at[pl.ds(pl.multiple_of(last, bm), bm)], zsem)

        def tail_start(e, c):
            @pl.when((cnt_ref[e] & (bm - 1)) != 0)
            def _():
                tail_copy(e).start()
            return c

        def tail_wait(e, c):
            @pl.when((cnt_ref[e] & (bm - 1)) != 0)
            def _():
                tail_copy(e).wait()
            return c

        lax.fori_loop(0, N_EXPERTS, tail_start, 0)
        lax.fori_loop(0, N_EXPERTS, tail_wait, 0)

    def row_copy(src_row, dst_row):
        return pltpu.make_async_copy(h_ref.at[pl.ds(src_row, 1)], xs_ref.at[pl.ds(dst_row, 1)], sem)

    def start(t, c):
        for k in range(TOP_K):
            j = t * TOP_K + k
            row_copy(t, ps_ref[e_ref[j]] + rk_ref[j]).start(priority=k % 2)
        return c

    lax.fori_loop(0, tt, start, 0)

    sh_ref[...] = _to_row_tiles(_swiglu(_from_row_tiles(h_ref[...]), wg_ref[...], wu_ref[...], wd_ref[...]))

    for _ in range(TOP_K):
        pltpu.make_async_copy(h_ref, xs_ref.at[pl.ds(0, tt)], sem).wait()


def _dispatch(e_flat, rk_flat, pstarts, counts, h_rows, cap, wg, wu, wd):
    n = h_rows.shape[0]
    tt = min(512, n)
    smem_blk = pl.BlockSpec((tt * TOP_K,), lambda i: (i,), memory_space=pltpu.SMEM)
    rows = pl.BlockSpec((tt,) + ROW_TILE, lambda i: (i, 0, 0))

    def const(w):
        return pl.BlockSpec(w.shape, lambda i: (0, 0), pipeline_mode=pl.Buffered(1))

    return pl.pallas_call(
        functools.partial(_dispatch_kernel, tt=tt),
        grid=(n // tt,),
        in_specs=[smem_blk, smem_blk,
                  pl.BlockSpec(memory_space=pltpu.SMEM),
                  pl.BlockSpec(memory_space=pltpu.SMEM),
                  rows, const(wg), const(wu), const(wd)],
        out_specs=[pl.BlockSpec(memory_space=pl.ANY), rows],
        out_shape=[jax.ShapeDtypeStruct((cap,) + ROW_TILE, h_rows.dtype),
                   jax.ShapeDtypeStruct((n,) + ROW_TILE, h_rows.dtype)],
        scratch_shapes=[pltpu.VMEM((EXPERT_BLOCK,) + ROW_TILE, h_rows.dtype),
                        pltpu.SemaphoreType.DMA(()), pltpu.SemaphoreType.DMA(())],
        compiler_params=_cparams(("arbitrary",)),
        name="dispatch",
    )(e_flat, rk_flat, pstarts, counts, h_rows, wg, wu, wd)


def _expert_kernel(be_ref, nu_ref, nx_ref, sl_ref, x_ref, wg_hbm, wu_hbm, wd_hbm, o_ref, wg_buf, wu_buf, wd_buf, sem):
    i = pl.program_id(0)
    used = i < nu_ref[0]
    e = be_ref[i]
    first = (i == 0) | (e != be_ref[jnp.maximum(i - 1, 0)])
    slot = sl_ref[e]
    nxt = nx_ref[e]

    def fetch(expert, s):
        return (pltpu.make_async_copy(wg_hbm.at[expert], wg_buf.at[s], sem.at[s, 0]),
                pltpu.make_async_copy(wu_hbm.at[expert], wu_buf.at[s], sem.at[s, 1]),
                pltpu.make_async_copy(wd_hbm.at[expert], wd_buf.at[s], sem.at[s, 2]))

    @pl.when(used & (i == 0))
    def _():
        for c in fetch(e, slot):
            c.start()

    @pl.when(used & first)
    def _():
        for c in fetch(e, slot):
            c.wait()

        @pl.when(nxt < N_EXPERTS)
        def _():
            for c in fetch(nxt, 1 - slot):
                c.start()

    @pl.when(used)
    def _():
        y = _swiglu(_from_row_tiles(x_ref[...]), wg_buf[slot], wu_buf[slot], wd_buf[slot])
        o_ref[...] = _to_row_tiles(y)

    @pl.when(jnp.logical_not(used))
    def _():
        o_ref[...] = jnp.zeros_like(o_ref)


def _experts(block_expert, n_used, next_expert, slot, xs, wg, wu, wd):
    cap = xs.shape[0]
    bm = EXPERT_BLOCK
    de = wg.shape[2]
    grid_spec = pltpu.PrefetchScalarGridSpec(
        num_scalar_prefetch=4,
        grid=(cap // bm,),
        in_specs=[pl.BlockSpec((bm,) + ROW_TILE, lambda i, be, nu, nx, sl: (jnp.minimum(i, nu[0] - 1), 0, 0)),
                  pl.BlockSpec(memory_space=pl.ANY),
                  pl.BlockSpec(memory_space=pl.ANY),
                  pl.BlockSpec(memory_space=pl.ANY)],
        out_specs=pl.BlockSpec((bm,) + ROW_TILE, lambda i, be, nu, nx, sl: (i, 0, 0)),
        scratch_shapes=[pltpu.VMEM((2, D_MODEL, de), F32), pltpu.VMEM((2, D_MODEL, de), F32),
                        pltpu.VMEM((2, de, D_MODEL), F32), pltpu.SemaphoreType.DMA((2, 3))],
    )
    return pl.pallas_call(
        _expert_kernel,
        grid_spec=grid_spec,
        out_shape=jax.ShapeDtypeStruct((cap,) + ROW_TILE, BF16),
        compiler_params=_cparams(("arbitrary",)),
        name="experts",
    )(block_expert, n_used, next_expert, slot, xs, wg, wu, wd)


def _combine_kernel(e_ref, rk_ref, en_ref, rkn_ref, ps_ref, ys_ref, w_ref, sh_ref, x_ref, gt_ref, g_ref, o_ref,
                    buf, sem, *, tt):
    step = pl.program_id(0) * pl.num_programs(1) + pl.program_id(1)
    n_steps = pl.num_programs(0) * pl.num_programs(1)
    slot = step & 1

    def gather(idx_ref, rank_ref, dst_slot):
        def start(t, c):
            for k in range(TOP_K):
                j = t * TOP_K + k
                pltpu.make_async_copy(ys_ref.at[pl.ds(ps_ref[idx_ref[j]] + rank_ref[j], 1)],
                                      buf.at[dst_slot, k, pl.ds(t, 1)], sem.at[dst_slot]).start(priority=k % 2)
            return c

        lax.fori_loop(0, tt, start, 0)

    @pl.when(step == 0)
    def _():
        gather(e_ref, rk_ref, 0)

    @pl.when(step + 1 < n_steps)
    def _():
        gather(en_ref, rkn_ref, 1 - slot)

    for k in range(TOP_K):
        pltpu.make_async_copy(ys_ref.at[pl.ds(0, tt)], buf.at[slot, k], sem.at[slot]).wait()

    w = w_ref[0]
    moe = _from_row_tiles(sh_ref[0]).astype(F32)
    for k in range(TOP_K):
        moe = moe + w[:, k:k + 1] * _from_row_tiles(buf[slot, k]).astype(F32)
    x2 = x_ref[0] + gt_ref[0] * moe
    ms = jnp.mean(x2 * x2, axis=-1, keepdims=True)
    o_ref[0] = x2 * lax.rsqrt(ms + NORM_EPS) * g_ref[...]


def _combine(e_flat, rk_flat, pstarts, ys, wsel, shared, x1, gt2, final_g):
    b, t, d = x1.shape
    tt = 128
    nt = t // tt
    last = b * nt - 1
    smem_blk = pl.BlockSpec((tt * TOP_K,), lambda i, j: (i * nt + j,), memory_space=pltpu.SMEM)
    smem_nxt = pl.BlockSpec((tt * TOP_K,), lambda i, j: (jnp.minimum(i * nt + j + 1, last),),
                            memory_space=pltpu.SMEM)
    return pl.pallas_call(
        functools.partial(_combine_kernel, tt=tt),
        grid=(b, nt),
        in_specs=[smem_blk, smem_blk, smem_nxt, smem_nxt,
                  pl.BlockSpec(memory_space=pltpu.SMEM),
                  pl.BlockSpec(memory_space=pl.ANY),
                  pl.BlockSpec((1, tt, LANES), lambda i, j: (i, j, 0)),
                  pl.BlockSpec((1, tt) + ROW_TILE, lambda i, j: (i, j, 0, 0)),
                  pl.BlockSpec((1, tt, d), lambda i, j: (i, j, 0)),
                  pl.BlockSpec((1, 1, d), lambda i, j: (i, 0, 0)),
                  pl.BlockSpec((1, d), lambda i, j: (0, 0))],
        out_specs=pl.BlockSpec((1, tt, d), lambda i, j: (i, j, 0)),
        out_shape=jax.ShapeDtypeStruct((b, t, d), F32),
        scratch_shapes=[pltpu.VMEM((2, TOP_K, tt) + ROW_TILE, BF16), pltpu.SemaphoreType.DMA((2,))],
        compiler_params=_cparams(("arbitrary", "arbitrary")),
        name="combine",
    )(e_flat, rk_flat, e_flat, rk_flat, pstarts, ys, wsel, shared, x1, gt2, final_g.reshape(1, d))


def _pad_rows(w, rows):
    return jnp.zeros((rows, w.shape[1]), w.dtype).at[:w.shape[0]].set(w)


def _lora_pad_cols(w):
    out = jnp.zeros(w.shape[:-1] + (D_LORA_PAD,), w.dtype)
    for i in range(4):
        out = out.at[..., i * LANES:i * LANES + D_LORA].set(w[..., i * D_LORA:(i + 1) * D_LORA])
    return out.at[..., 4 * LANES:].set(w[..., 4 * D_LORA:])


def kernel(x, c, ctx, c_ctx, norm1_g, norm2_g, ada_w, ada_b, w_in, shift_mu, pool_w, pool_scale, w_pool_out, decay_w0, decay_w2, iclr_a0, iclr_a2, gate_g2, k_k, k_a, r_k, lnx_w, lnx_b, w_rwkv_out, w_out, router_w, router_bias, exp_w_gate, exp_w_up, exp_w_down, shared_w_gate, shared_w_up, shared_w_down, final_g):
    B, T, D = x.shape
    TC = ctx.shape[1]
    n = B * T
    l = 0

    w_in_l = w_in[l]
    w_u = w_in_l[:, :D_POOL].astype(BF16)
    w_rkv = w_in_l[:, D_POOL:D_POOL + 3 * D_ATT].astype(BF16)
    w_lora = _lora_pad_cols(w_in_l[:, D_POOL + 3 * D_ATT:D_POOL + 3 * D_ATT + 4 * D_LORA + D_GATE_LORA]).astype(BF16)
    w_gates = w_in_l[:, D_POOL + 3 * D_ATT + 4 * D_LORA + D_GATE_LORA:].astype(BF16)
    mu = shift_mu[l]
    row = lambda a: a.reshape(1, -1)
    pw = {
        "mu_rkv": row(mu[:3 * D_ATT]),
        "mu_lora": row(_lora_pad_cols(mu[3 * D_ATT:])),
        "w2f": _pad_rows(0.5 * decay_w2[l, 0], LANES).astype(BF16),
        "w2b": _pad_rows(0.5 * decay_w2[l, 1], LANES).astype(BF16),
        "a2f": _pad_rows(0.5 * iclr_a2[l, 0], LANES).astype(BF16),
        "a2b": _pad_rows(0.5 * iclr_a2[l, 1], LANES).astype(BF16),
        "w0f": row(0.5 * decay_w0[l, 0]), "w0b": row(0.5 * decay_w0[l, 1]),
        "a0f": row(0.5 * iclr_a0[l, 0]), "a0b": row(0.5 * iclr_a0[l, 1]),
        "k_k": row(k_k[l]), "k_a": row(k_a[l]), "r_k": row(r_k[l]),
    }

    cstack = jnp.zeros((8, D), F32).at[:B].set(c).at[B].set(c_ctx)
    mod = _ada_mod(cstack, ada_w[l], ada_b[l])
    sh1, sc1, gt1, sh2, sc2, gt2 = [mod[:B, i * D:(i + 1) * D].reshape(B, 1, D) for i in range(6)]
    sh1c = jnp.broadcast_to(mod[B, 0:D].reshape(1, 1, D), (B, 1, D))
    sc1c = jnp.broadcast_to(mod[B, D:2 * D].reshape(1, 1, D), (B, 1, D))

    hc = _norm_mod(ctx, norm1_g[l], sc1c, sh1c, TC).reshape(B * TC, D)
    rkv_c = _matmul(hc, w_rkv, F32, 512, 512).reshape(B, TC, 3 * D_ATT)
    lora_c = _matmul(hc, w_lora, F32, 512, D_LORA_PAD).reshape(B, TC, D_LORA_PAD)
    pc = _prepare(rkv_c, lora_c, pw, grid_mode=False)
    flat = lambda a: a.reshape(B * N_PAIRS, a.shape[2], LANES)
    s0 = jnp.zeros((B * N_PAIRS, LANES, LANES), F32)
    vc = flat(pc[10])
    _, st_f = _scan(*[flat(a) for a in pc[0:5]], vc, s0, reverse=False, emit=False)
    _, st_b = _scan(*[flat(a) for a in pc[5:10]], vc, s0, reverse=True, emit=False)

    h = _norm_mod(x, norm1_g[l], sc1, sh1, 512).reshape(n, D)
    u = _matmul(h, w_u, F32, 2048, 512).reshape(B, T, D_POOL)
    rkv = _matmul(h, w_rkv, F32, 2048, 1024).reshape(B, T, 3 * D_ATT)
    lora = _matmul(h, w_lora, F32, 2048, D_LORA_PAD).reshape(B, T, D_LORA_PAD)
    gates = _matmul(h, w_gates, BF16, 2048, 1024).reshape(B, T, 2 * D)
    pp = _prepare(rkv, lora, pw, grid_mode=True)
    v_, bonus, gd = flat(pp[10]), pp[11], pp[12]
    y_f, _ = _scan(*[flat(a) for a in pp[0:5]], v_, st_f, reverse=False, emit=True)
    y_b, _ = _scan(*[flat(a) for a in pp[5:10]], v_, st_b, reverse=True, emit=True)
    y_rwkv = _readout(y_f.reshape(B, N_PAIRS, T, LANES), y_b.reshape(B, N_PAIRS, T, LANES), bonus, gd,
                      row(lnx_w[l]), row(lnx_b[l]), gate_g2[l].astype(BF16), w_rwkv_out[l].astype(BF16))
    y_pool = _pool_branch(u, pool_w[l].astype(BF16), row(pool_scale[l]), w_pool_out[l].astype(BF16))

    router_w_pad = jnp.zeros((D, LANES), F32).at[:, :N_EXPERTS].set(router_w[l])
    x1, h2, logits = _merge(y_pool, y_rwkv, gates, x, gt1, sc2, sh2, norm2_g[l],
                                w_out[l].astype(BF16), router_w_pad)

    bias_pad = jnp.zeros((1, LANES), F32).at[0, :N_EXPERTS].set(router_bias[l])
    e_idx, e_rank, wsel, counts = _router(logits.reshape(n, LANES), bias_pad)
    e_flat = e_idx[:, :TOP_K].reshape(-1)
    rk_flat = e_rank[:, :TOP_K].reshape(-1)
    bm = EXPERT_BLOCK
    cnt = counts[0, :N_EXPERTS].astype(I32)
    padded = (cnt + bm - 1) // bm * bm
    pend = jnp.cumsum(padded)
    pstarts = (pend - padded).astype(I32)
    cap = n * TOP_K + N_EXPERTS * bm
    n_blocks = cap // bm
    block_start = jnp.arange(n_blocks, dtype=I32) * bm
    block_expert = jnp.minimum(jnp.sum(block_start[:, None] >= pend[None, :], axis=-1), N_EXPERTS - 1).astype(I32)
    n_used = (pend[-1] // bm).astype(I32).reshape(1)
    has = cnt > 0
    ids = jnp.where(has, jnp.arange(N_EXPERTS, dtype=I32), N_EXPERTS)
    next_e = jnp.concatenate([lax.cummin(ids, axis=0, reverse=True)[1:], jnp.full((1,), N_EXPERTS, I32)])
    slot_e = ((jnp.cumsum(has.astype(I32)) - 1) & 1).astype(I32)

    h2 = h2.reshape((n,) + ROW_TILE)
    xs, shared = _dispatch(e_flat, rk_flat, pstarts, cnt, h2, cap,
                           shared_w_gate[l], shared_w_up[l], shared_w_down[l])
    ys = _experts(block_expert, n_used, next_e, slot_e, xs, exp_w_gate[l], exp_w_up[l], exp_w_down[l])
    return _combine(e_flat, rk_flat, pstarts, ys, wsel.reshape(B, T, LANES),
                    shared.reshape((B, T) + ROW_TILE), x1, gt2, final_g)
```

```python
import functools

import jax
import jax.numpy as jnp
from jax import lax
from jax.experimental import pallas as pl
from jax.experimental.pallas import tpu as pltpu

F32 = jnp.float32
BF16 = jnp.bfloat16
I32 = jnp.int32

D_MODEL = 2048
GRID_W = 64
POOL_WINDOWS = (2, 4, 8, 16)
POOL_GROUP = 256
D_POOL = 1024
HEAD = 64
N_HEADS = 32
N_PAIRS = N_HEADS // 2
D_ATT = 2048
D_LORA = 96
D_GATE_LORA = 256
D_LORA_PAD = 768
GN_EPS = 64e-5
NORM_EPS = 1e-6
N_EXPERTS = 64
TOP_K = 8
N_GROUPS = 8
TOPK_GROUPS = 4
D_EXPERT = 512
ROUTED_SCALE = 2.5
EXP_M05 = 0.6065306597126334

LANES = 128
CHUNK = 64
EXPERT_BLOCK = 512
VMEM_LIMIT = 56 * 1024 * 1024


def _cparams(sem):
    return pltpu.CompilerParams(dimension_semantics=sem, vmem_limit_bytes=VMEM_LIMIT)


def _dot(a, b):
    return jnp.dot(a, b, preferred_element_type=F32)


def _bmm(a, b):
    return lax.dot_general(a, b, (((2,), (1,)), ((0,), (0,))), preferred_element_type=F32)


def _bmm_nt(a, b):
    return lax.dot_general(a, b, (((2,), (2,)), ((0,), (0,))), preferred_element_type=F32)


def _bmm_tn(a, b):
    return lax.dot_general(a, b, (((1,), (1,)), ((0,), (0,))), preferred_element_type=F32)


def _sigmoid(x):
    return 0.5 * jnp.tanh(0.5 * x) + 0.5


def _split2(x):
    hi = x.astype(BF16)
    lo = (x - hi.astype(F32)).astype(BF16)
    return hi, lo


def _iota(shape, axis):
    return lax.broadcasted_iota(I32, shape, axis)


def _ada_kernel(c_ref, w_ref, b_ref, o_ref):
    c = c_ref[...]
    a = c * _sigmoid(c)
    o_ref[...] = _dot(a.astype(BF16), w_ref[...].astype(BF16)) + b_ref[...]


def _ada_mod(cstack, ada_w, ada_b):
    m, d = cstack.shape
    n = ada_w.shape[1]
    tn = 1024
    return pl.pallas_call(
        _ada_kernel,
        grid=(n // tn,),
        in_specs=[pl.BlockSpec((m, d), lambda j: (0, 0)),
                  pl.BlockSpec((d, tn), lambda j: (0, j)),
                  pl.BlockSpec((1, tn), lambda j: (0, j))],
        out_specs=pl.BlockSpec((m, tn), lambda j: (0, j)),
        out_shape=jax.ShapeDtypeStruct((m, n), F32),
        compiler_params=_cparams(("parallel",)),
        name="ada_mod",
    )(cstack, ada_w, ada_b.reshape(1, n))


def _norm_mod_kernel(x_ref, g_ref, sc_ref, sh_ref, o_ref):
    x = x_ref[0]
    ms = jnp.mean(x * x, axis=-1, keepdims=True)
    y = x * lax.rsqrt(ms + NORM_EPS) * g_ref[...]
    o_ref[0] = (y * (1.0 + sc_ref[0]) + sh_ref[0]).astype(o_ref.dtype)


def _norm_mod(x, g, sc, sh, tt):
    b, t, d = x.shape
    return pl.pallas_call(
        _norm_mod_kernel,
        grid=(b, t // tt),
        in_specs=[pl.BlockSpec((1, tt, d), lambda i, j: (i, j, 0)),
                  pl.BlockSpec((1, d), lambda i, j: (0, 0)),
                  pl.BlockSpec((1, 1, d), lambda i, j: (i, 0, 0)),
                  pl.BlockSpec((1, 1, d), lambda i, j: (i, 0, 0))],
        out_specs=pl.BlockSpec((1, tt, d), lambda i, j: (i, j, 0)),
        out_shape=jax.ShapeDtypeStruct((b, t, d), BF16),
        compiler_params=_cparams(("parallel", "parallel")),
        name="norm_mod",
    )(x, g.reshape(1, d), sc, sh)


def _mm_kernel(a_ref, b_ref, o_ref):
    o_ref[...] = _dot(a_ref[...], b_ref[...]).astype(o_ref.dtype)


def _matmul(a, b, out_dtype, tm, tn):
    m, k = a.shape
    n = b.shape[1]
    tm = min(tm, m)
    return pl.pallas_call(
        _mm_kernel,
        grid=(m // tm, n // tn),
        in_specs=[pl.BlockSpec((tm, k), lambda i, j: (i, 0)),
                  pl.BlockSpec((k, tn), lambda i, j: (0, j))],
        out_specs=pl.BlockSpec((tm, tn), lambda i, j: (i, j)),
        out_shape=jax.ShapeDtypeStruct((m, n), out_dtype),
        compiler_params=_cparams(("parallel", "parallel")),
        name="matmul",
    )(a, b)


def _shift_grid(x, prev, nxt, first, last):
    tt, c = x.shape
    col = _iota((tt, c), 0) & (GRID_W - 1)
    m = _iota((tt, c), 1) & 3
    horiz = jnp.where(m == 0, pltpu.roll(x, 1, 0), pltpu.roll(x, tt - 1, 0))
    at_border = ((m == 0) & (col == 0)) | ((m == 1) & (col == GRID_W - 1))
    horiz = jnp.where(at_border, 0.0, horiz)
    prev = jnp.where(first, 0.0, prev)
    nxt = jnp.where(last, 0.0, nxt)
    if tt > GRID_W:
        up = jnp.concatenate([prev, x[:tt - GRID_W]], axis=0)
        down = jnp.concatenate([x[GRID_W:], nxt], axis=0)
    else:
        up, down = prev, nxt
    return jnp.where(m < 2, horiz, jnp.where(m == 2, up, down))


def _shift_seq(x, prev8, next8, first, last):
    t, c = x.shape
    row = _iota((t, c), 0)
    odd = (_iota((t, c), 1) & 1) == 1
    before = jnp.where(first, 0.0, prev8[7:8])
    after = jnp.where(last, 0.0, next8[0:1])
    prev = jnp.where(row == 0, before, pltpu.roll(x, 1, 0))
    nxt = jnp.where(row == t - 1, after, pltpu.roll(x, t - 1, 0))
    return jnp.where(odd, nxt, prev)


def _head_sum(x):
    w = 2 * LANES
    ones = (_iota((w, w), 0) >> 6 == _iota((w, w), 1) >> 6).astype(BF16)
    outs = []
    for c in range(x.shape[1] // w):
        hi, lo = _split2(x[:, c * w:(c + 1) * w])
        outs.append(_dot(hi, ones) + _dot(lo, ones))
    return jnp.concatenate(outs, axis=1)


def _prepare_kernel(*refs, grid_mode):
    (r_ref, rp_ref, rn_ref, k_ref, kp_ref, kn_ref, v_ref, vp_ref, vn_ref,
     l_ref, lp_ref, ln_ref) = refs[:12]
    rest = refs[12:]
    (mur_ref, muk_ref, muv_ref, mul_ref, w2f_ref, w2b_ref, a2f_ref, a2b_ref,
     w0f_ref, w0b_ref, a0f_ref, a0b_ref, kk_ref, ka_ref, rk_ref,
     oaf_ref, obf_ref, okf_ref, orf_ref, owf_ref, oab_ref, obb_ref, okb_ref, orb_ref, owb_ref,
     ov_ref, obon_ref, ogd_ref) = rest

    first = pl.program_id(1) == 0
    last = pl.program_id(1) == pl.num_programs(1) - 1
    shift = _shift_grid if grid_mode else _shift_seq

    def mix(x_ref, p_ref, n_ref, mu_ref):
        x = x_ref[0]
        return x + (shift(x, p_ref[0], n_ref[0], first, last) - x) * mu_ref[...]

    r = mix(r_ref, rp_ref, rn_ref, mur_ref)
    k = mix(k_ref, kp_ref, kn_ref, muk_ref)
    v = mix(v_ref, vp_ref, vn_ref, muv_ref)
    lo = mix(l_ref, lp_ref, ln_ref, mul_ref)

    th = jnp.tanh(lo[:, :2 * LANES]).astype(BF16)
    tzf = jnp.tanh(w0f_ref[...] + _dot(th[:, :LANES], w2f_ref[...]))
    tzb = jnp.tanh(w0b_ref[...] + _dot(th[:, LANES:], w2b_ref[...]))
    c = -0.5 * EXP_M05
    lwf = tzf * c + c
    lwb = tzb * c + c
    ad = lo[:, 2 * LANES:4 * LANES].astype(BF16)
    af = 0.5 * jnp.tanh(a0f_ref[...] + _dot(ad[:, :LANES], a2f_ref[...])) + 0.5
    ab = 0.5 * jnp.tanh(a0b_ref[...] + _dot(ad[:, LANES:], a2b_ref[...])) + 0.5
    ogd_ref[0] = lo[:, 4 * LANES:]

    kk = k * kk_ref[...]
    kk = kk * lax.rsqrt(_head_sum(kk * kk) + 1e-12)
    ka = ka_ref[...]
    one_minus_ka = 1.0 - ka
    kf = k * (af * ka + one_minus_ka)
    kb = k * (ab * ka + one_minus_ka)
    bonus = _head_sum(r * (kf + kb) * rk_ref[...]) * v

    tt = r.shape[0]
    t2 = _iota((tt, tt), 0)
    s2 = _iota((tt, tt), 1)
    same = (t2 >> 6) == (s2 >> 6)

    def scan_operands(lw, b, kd, reverse):
        tri = (same & ((s2 >= t2) if reverse else (s2 <= t2))).astype(BF16)
        hi = lw.astype(BF16)
        r1 = lw - hi.astype(F32)
        mid = r1.astype(BF16)
        lo3 = (r1 - mid.astype(F32)).astype(BF16)
        cum = _dot(tri, hi) + _dot(tri, mid) + _dot(tri, lo3)
        w_inc = jnp.exp(cum)
        w_inv = jnp.exp(-cum)
        ends = [w_inc[c * CHUNK:c * CHUNK + 1] if reverse else w_inc[(c + 1) * CHUNK - 1:(c + 1) * CHUNK]
                for c in range(tt // CHUNK)]
        w_last = jnp.concatenate(ends + [jnp.zeros((8 - len(ends), lw.shape[1]), F32)], axis=0)
        return jnp.exp(cum - lw) * kk, b * w_inv, kd * w_inv, w_inc * r, w_last

    al_f, be_f, ka_f, rh_f, wl_f = scan_operands(lwf, kk * af, kf, False)
    al_b, be_b, ka_b, rh_b, wl_b = scan_operands(lwb, kk * ab, kb, True)
    outs = ((oaf_ref, al_f), (obf_ref, be_f), (okf_ref, ka_f), (orf_ref, rh_f), (owf_ref, wl_f),
            (oab_ref, al_b), (obb_ref, be_b), (okb_ref, ka_b), (orb_ref, rh_b), (owb_ref, wl_b),
            (ov_ref, v), (obon_ref, bonus))
    for o_ref, val in outs:
        for p in range(N_PAIRS):
            o_ref[0, p] = val[:, p * LANES:(p + 1) * LANES].astype(o_ref.dtype)


def _prepare(rkv, lora, pw, grid_mode):
    b, t, _ = rkv.shape
    d = D_ATT
    tt = 2 * GRID_W
    halo = GRID_W if grid_mode else 8
    hpt = tt // halo
    nhalo = t // halo
    grid = (b, t // tt)

    def tile_specs(w, c):
        return [pl.BlockSpec((1, tt, w), lambda i, j: (i, j, c)),
                pl.BlockSpec((1, halo, w), lambda i, j: (i, jnp.maximum(j * hpt - 1, 0), c)),
                pl.BlockSpec((1, halo, w), lambda i, j: (i, jnp.minimum((j + 1) * hpt, nhalo - 1), c))]

    in_specs = []
    args = []
    for c in range(3):
        in_specs += tile_specs(d, c)
        args += [rkv, rkv, rkv]
    in_specs += tile_specs(D_LORA_PAD, 0)
    args += [lora, lora, lora]

    def vec(c=0, w=d):
        return pl.BlockSpec((1, w), lambda i, j, c=c: (0, c))

    def full(shape):
        return pl.BlockSpec(shape, lambda i, j: (0,) * len(shape))

    in_specs += [vec(0), vec(1), vec(2), vec(0, D_LORA_PAD)]
    args += [pw["mu_rkv"], pw["mu_rkv"], pw["mu_rkv"], pw["mu_lora"]]
    in_specs += [full((LANES, d))] * 4
    args += [pw["w2f"], pw["w2b"], pw["a2f"], pw["a2b"]]
    in_specs += [vec()] * 7
    args += [pw["w0f"], pw["w0b"], pw["a0f"], pw["a0b"], pw["k_k"], pw["k_a"], pw["r_k"]]

    pair_spec = pl.BlockSpec((1, N_PAIRS, tt, LANES), lambda i, j: (i, 0, j, 0))
    wl_spec = pl.BlockSpec((1, N_PAIRS, 8, LANES), lambda i, j: (i, 0, j, 0))

    def pair(dtype):
        return jax.ShapeDtypeStruct((b, N_PAIRS, t, LANES), dtype)

    wl = jax.ShapeDtypeStruct((b, N_PAIRS, t // tt * 8, LANES), F32)
    direction = [pair(BF16)] * 4 + [wl]
    out_shape = direction * 2 + [pair(BF16), pair(F32), jax.ShapeDtypeStruct((b, t, D_GATE_LORA), F32)]
    out_specs = ([pair_spec] * 4 + [wl_spec]) * 2 + [pair_spec, pair_spec,
                                                    pl.BlockSpec((1, tt, D_GATE_LORA), lambda i, j: (i, j, 0))]
    return pl.pallas_call(
        functools.partial(_prepare_kernel, grid_mode=grid_mode),
        grid=grid, in_specs=in_specs, out_specs=out_specs, out_shape=out_shape,
        compiler_params=_cparams(("parallel", "parallel")),
        name="prepare_grid" if grid_mode else "prepare_seq",
    )(*args)


def _scan_kernel(*refs, reverse, pairs, tb, emit):
    al_ref, be_ref, ka_ref, rh_ref, wl_ref, v_ref, s0_ref = refs[:7]
    if emit:
        y_ref, st_ref, s_scr = refs[7:]
    else:
        st_ref, s_scr = refs[7:]
    L = CHUNK
    n_chunks = tb // L

    @pl.when(pl.program_id(1) == 0)
    def _():
        s_scr[...] = s0_ref[...]

    t2 = _iota((2 * L, 2 * L), 0)
    s2 = _iota((2 * L, 2 * L), 1)
    same = (t2 >> 6) == (s2 >> 6)
    tl = t2 & (L - 1)
    sl = s2 & (L - 1)
    strict = same & ((sl > tl) if reverse else (sl < tl))
    incl = same & ((sl >= tl) if reverse else (sl <= tl))
    eye = (t2 == s2).astype(F32)
    head_a = _iota((pairs, L, LANES), 2) < HEAD
    zero = jnp.zeros((), BF16)

    def stack(ref, rows):
        x = ref[:, rows, :]
        return jnp.concatenate([jnp.where(head_a, x, zero), jnp.where(head_a, zero, x)], axis=1)

    def chunk(ci, carry):
        cc = (n_chunks - 1 - ci) if reverse else ci
        rows = pl.ds(pl.multiple_of(cc * L, L), L)
        a_s = stack(al_ref, rows)
        b_s = stack(be_ref, rows)
        k_s = stack(ka_ref, rows)
        v_s = stack(v_ref, rows)
        lhs = jnp.concatenate([a_s, stack(rh_ref, rows)], axis=1) if emit else a_s
        xb = _bmm_nt(lhs, b_s)
        xk = _bmm_nt(lhs, k_s)
        m1 = jnp.where(strict, xb[:, :2 * L], 0.0)
        m2 = jnp.where(strict, xk[:, :2 * L], 0.0)
        xp = -m1
        tinv = eye + xp
        for _ in range(5):
            xq = xp.astype(BF16)
            xp = _bmm(xq, xq)
            tinv = tinv + _bmm(tinv.astype(BF16), xp.astype(BF16))
        s = s_scr[...]
        xs = _bmm_nt(lhs, s.astype(BF16))
        if emit:
            n1 = jnp.where(incl, xb[:, 2 * L:], 0.0)
            n2 = jnp.where(incl, xk[:, 2 * L:], 0.0)
            xv = _bmm(jnp.concatenate([m2, n2], axis=1).astype(BF16), v_s)
        else:
            xv = _bmm(m2.astype(BF16), v_s)
        g = xs[:, :2 * L] + xv[:, :2 * L]
        u_s = (-_bmm(tinv.astype(BF16), g.astype(BF16))).astype(BF16)
        if emit:
            y2 = xs[:, 2 * L:] + xv[:, 2 * L:] + _bmm(n1.astype(BF16), u_s)
            y_ref[:, rows, :] = y2[:, :L] + y2[:, L:]
        w_last = wl_ref[:, pl.ds((cc >> 1) * 8 + (cc & 1), 1), :]
        upd = _bmm_tn(jnp.concatenate([u_s, v_s], axis=1), jnp.concatenate([b_s, k_s], axis=1))
        s_scr[...] = (s + upd) * w_last
        return carry

    lax.fori_loop(0, n_chunks, chunk, 0)

    @pl.when(pl.program_id(1) == pl.num_programs(1) - 1)
    def _():
        st_ref[...] = s_scr[...]


def _scan(al, be, ka, rh, wl, v, s0, reverse, emit, pairs=32, tb=256):
    bp, t, _ = al.shape
    pairs = min(pairs, bp)
    tb = min(t, tb)
    nb = t // tb

    def tmap(g, c):
        return (g, (nb - 1 - c) if reverse else c, 0)

    data = pl.BlockSpec((pairs, tb, LANES), tmap)
    wl_spec = pl.BlockSpec((pairs, tb // (2 * CHUNK) * 8, LANES), tmap)
    state = pl.BlockSpec((pairs, LANES, LANES), lambda g, c: (g, 0, 0))
    out_shape = [jax.ShapeDtypeStruct((bp, LANES, LANES), F32)]
    out_specs = [state]
    if emit:
        out_shape = [jax.ShapeDtypeStruct((bp, t, LANES), F32)] + out_shape
        out_specs = [data] + out_specs
    res = pl.pallas_call(
        functools.partial(_scan_kernel, reverse=reverse, pairs=pairs, tb=tb, emit=emit),
        grid=(bp // pairs, nb),
        in_specs=[data] * 4 + [wl_spec, data, state],
        out_specs=out_specs, out_shape=out_shape,
        scratch_shapes=[pltpu.VMEM((pairs, LANES, LANES), F32)],
        compiler_params=_cparams(("parallel", "arbitrary")),
        name="scan_" + ("bwd" if reverse else "fwd") + ("_emit" if emit else "_state"),
    )(al, be, ka, rh, wl, v, s0)
    return (res[0], res[1]) if emit else (None, res[0])


def _readout_kernel(yf_ref, yb_ref, bon_ref, gd_ref, lnw_ref, lnb_ref, g2_ref, w_ref, o_ref):
    y = jnp.concatenate([yf_ref[0, p] + yb_ref[0, p] for p in range(N_PAIRS)], axis=1)
    bonus = jnp.concatenate([bon_ref[0, p] for p in range(N_PAIRS)], axis=1)
    mean = _head_sum(y) * (1.0 / HEAD)
    dlt = y - mean
    var = _head_sum(dlt * dlt) * (1.0 / HEAD)
    yn = dlt * lax.rsqrt(var + GN_EPS) * lnw_ref[...] + lnb_ref[...]
    gate = _dot(_sigmoid(gd_ref[0]).astype(BF16), g2_ref[...])
    out = ((yn + bonus) * gate).astype(BF16)
    o_ref[0] = _dot(out, w_ref[...]).astype(o_ref.dtype)


def _readout(yf, yb, bonus, gd, lnw, lnb, g2, w_out):
    b, _, t, _ = yf.shape
    tt = 256
    d = D_ATT
    pair_spec = pl.BlockSpec((1, N_PAIRS, tt, LANES), lambda i, j: (i, 0, j, 0))
    return pl.pallas_call(
        _readout_kernel,
        grid=(b, t // tt),
        in_specs=[pair_spec, pair_spec, pair_spec,
                  pl.BlockSpec((1, tt, D_GATE_LORA), lambda i, j: (i, j, 0)),
                  pl.BlockSpec((1, d), lambda i, j: (0, 0)),
                  pl.BlockSpec((1, d), lambda i, j: (0, 0)),
                  pl.BlockSpec((D_GATE_LORA, d), lambda i, j: (0, 0)),
                  pl.BlockSpec((d, D_MODEL), lambda i, j: (0, 0))],
        out_specs=pl.BlockSpec((1, tt, D_MODEL), lambda i, j: (i, j, 0)),
        out_shape=jax.ShapeDtypeStruct((b, t, D_MODEL), BF16),
        compiler_params=_cparams(("parallel", "parallel")),
        name="readout",
    )(yf, yb, bonus, gd, lnw, lnb, g2, w_out)


def _pool_kernel(u_ref, pw_ref, ps_ref, wo_ref, o_ref):
    u = u_ref[0]
    tt = u.shape[0]
    t2 = _iota((tt, tt), 0)
    s2 = _iota((tt, tt), 1)
    same = (t2 >> 6) == (s2 >> 6)
    tc = t2 & (GRID_W - 1)
    sc = s2 & (GRID_W - 1)
    col = _iota((tt, POOL_GROUP), 0) & (GRID_W - 1)
    ys = []
    for gi, w in enumerate(POOL_WINDOWS):
        ug = u[:, gi * POOL_GROUP:(gi + 1) * POOL_GROUP]
        win = (same & (sc >= tc - w // 2) & (sc < tc + (w - w // 2))).astype(BF16)
        hi, lo = _split2(ug)
        wsum = _dot(win, hi) + _dot(win, lo)
        cnt = (jnp.minimum(col + (w - w // 2), GRID_W) - jnp.maximum(col - w // 2, 0)).astype(F32)
        dlt = wsum / cnt - ug
        ys.append(_dot(dlt.astype(BF16), pw_ref[gi]))
    y1 = jnp.concatenate(ys, axis=1) * ps_ref[...]
    o_ref[0] = _dot(y1.astype(BF16), wo_ref[...]).astype(o_ref.dtype)


def _pool_branch(u, pool_w, pool_scale, w_pool_out):
    b, t, _ = u.shape
    tt = 256
    return pl.pallas_call(
        _pool_kernel,
        grid=(b, t // tt),
        in_specs=[pl.BlockSpec((1, tt, D_POOL), lambda i, j: (i, j, 0)),
                  pl.BlockSpec((4, POOL_GROUP, POOL_GROUP), lambda i, j: (0, 0, 0)),
                  pl.BlockSpec((1, D_POOL), lambda i, j: (0, 0)),
                  pl.BlockSpec((D_POOL, D_MODEL), lambda i, j: (0, 0))],
        out_specs=pl.BlockSpec((1, tt, D_MODEL), lambda i, j: (i, j, 0)),
        out_shape=jax.ShapeDtypeStruct((b, t, D_MODEL), BF16),
        compiler_params=_cparams(("parallel", "parallel")),
        name="pool_branch",
    )(u, pool_w, pool_scale, w_pool_out)


ROW_TILE = (D_MODEL // LANES, LANES)


def _to_row_tiles(x):
    return x.astype(BF16).reshape((x.shape[0],) + ROW_TILE)


def _from_row_tiles(x):
    return x.reshape(x.shape[0], D_MODEL)


def _merge_kernel(yp_ref, yr_ref, gp_ref, gr_ref, x_ref, gt_ref, sc_ref, sh_ref, g_ref, w_ref, rw_ref,
                  x1_ref, h_ref, lg_ref):
    m = (_sigmoid(gp_ref[0].astype(F32)) * yp_ref[0].astype(F32)
         + _sigmoid(gr_ref[0].astype(F32)) * yr_ref[0].astype(F32))
    x1 = x_ref[0] + gt_ref[0] * _dot(m.astype(BF16), w_ref[...])
    x1_ref[0] = x1
    ms = jnp.mean(x1 * x1, axis=-1, keepdims=True)
    h = x1 * lax.rsqrt(ms + NORM_EPS) * g_ref[...]
    h = h * (1.0 + sc_ref[0]) + sh_ref[0]
    h_ref[0] = _to_row_tiles(h)
    hh, hl = _split2(h)
    rh, rl = _split2(rw_ref[...])
    lg_ref[0] = _dot(hh, rh) + _dot(hl, rh) + _dot(hh, rl)


def _merge(y_pool, y_rwkv, gates, x, gt1, sc2, sh2, g2, w_out, router_w_pad):
    b, t, d = x.shape
    tt = 256
    tile = pl.BlockSpec((1, tt, d), lambda i, j: (i, j, 0))
    mod = pl.BlockSpec((1, 1, d), lambda i, j: (i, 0, 0))
    return pl.pallas_call(
        _merge_kernel,
        grid=(b, t // tt),
        in_specs=[tile, tile,
                  pl.BlockSpec((1, tt, d), lambda i, j: (i, j, 0)),
                  pl.BlockSpec((1, tt, d), lambda i, j: (i, j, 1)),
                  tile, mod, mod, mod,
                  pl.BlockSpec((1, d), lambda i, j: (0, 0)),
                  pl.BlockSpec((d, d), lambda i, j: (0, 0)),
                  pl.BlockSpec((d, LANES), lambda i, j: (0, 0))],
        out_specs=[tile,
                   pl.BlockSpec((1, tt) + ROW_TILE, lambda i, j: (i, j, 0, 0)),
                   pl.BlockSpec((1, tt, LANES), lambda i, j: (i, j, 0))],
        out_shape=[jax.ShapeDtypeStruct((b, t, d), F32),
                   jax.ShapeDtypeStruct((b, t) + ROW_TILE, BF16),
                   jax.ShapeDtypeStruct((b, t, LANES), F32)],
        compiler_params=_cparams(("parallel", "parallel")),
        name="merge",
    )(y_pool, y_rwkv, gates, gates, x, gt1, sc2, sh2, g2.reshape(1, d), w_out, router_w_pad)


def _router_kernel(lg_ref, bias_ref, e_ref, rk_ref, w_ref, cnt_ref, carry):
    tt = lg_ref.shape[0]
    shape = (tt, LANES)
    lane = _iota(shape, 1)
    valid = lane < N_EXPERTS
    grp = (lane & (N_EXPERTS - 1)) >> 3
    neg = jnp.float32(-jnp.inf)

    @pl.when(pl.program_id(0) == 0)
    def _():
        carry[...] = jnp.zeros_like(carry)

    scores = _sigmoid(lg_ref[...])
    sel = scores + bias_ref[...]
    sel = jnp.where(valid, sel, pltpu.roll(sel, N_EXPERTS, 1))

    def group_reduce(x, op):
        for sh in (1, 2, 4):
            up = pltpu.roll(x, sh, 1)
            dn = pltpu.roll(x, LANES - sh, 1)
            x = op(x, jnp.where((lane & sh) != 0, up, dn))
        return x

    m1 = group_reduce(sel, jnp.maximum)
    first = group_reduce(jnp.where(sel == m1, lane, LANES), jnp.minimum)
    m2 = group_reduce(jnp.where(lane == first, neg, sel), jnp.maximum)
    gs = m1 + m2
    beaten = jnp.zeros(shape, I32)
    for k in range(1, N_GROUPS):
        other = pltpu.roll(gs, 8 * k, 1)
        og = (grp - k) & (N_GROUPS - 1)
        beaten = beaten + ((other > gs) | ((other == gs) & (og < grp))).astype(I32)
    cur = jnp.where((beaten < TOPK_GROUPS) & valid, sel, neg)

    picked = jnp.zeros(shape, jnp.bool_)
    e_acc = jnp.zeros(shape, I32)
    w_acc = jnp.zeros(shape, F32)
    idxs = []
    for k in range(TOP_K):
        m = jnp.max(cur, axis=1, keepdims=True)
        idx = jnp.min(jnp.where(cur == m, lane, LANES), axis=1, keepdims=True)
        oh = lane == idx
        sc = jnp.sum(jnp.where(oh, scores, 0.0), axis=1, keepdims=True)
        e_acc = jnp.where(lane == k, idx, e_acc)
        w_acc = jnp.where(lane == k, sc, w_acc)
        picked = picked | oh
        cur = jnp.where(oh, neg, cur)
        idxs.append(idx)
    wsum = jnp.sum(w_acc, axis=1, keepdims=True)
    w_ref[...] = w_acc / wsum * ROUTED_SCALE
    e_ref[...] = e_acc

    lower = (_iota((tt, tt), 1) < _iota((tt, tt), 0)).astype(BF16)
    pk = picked.astype(BF16)
    before = _dot(lower, pk) + carry[...]
    r_acc = jnp.zeros(shape, F32)
    for k in range(TOP_K):
        rk = jnp.sum(jnp.where(lane == idxs[k], before, 0.0), axis=1, keepdims=True)
        r_acc = jnp.where(lane == k, rk, r_acc)
    rk_ref[...] = r_acc.astype(I32)
    carry[...] = carry[...] + jnp.sum(picked.astype(F32), axis=0, keepdims=True)
    cnt_ref[...] = carry[...]


def _router(logits, bias_pad):
    n = logits.shape[0]
    tt = min(1024, n)
    tile = pl.BlockSpec((tt, LANES), lambda i: (i, 0))
    row = pl.BlockSpec((1, LANES), lambda i: (0, 0))
    return pl.pallas_call(
        _router_kernel,
        grid=(n // tt,),
        in_specs=[tile, row],
        out_specs=[tile, tile, tile, row],
        out_shape=[jax.ShapeDtypeStruct((n, LANES), I32), jax.ShapeDtypeStruct((n, LANES), I32),
                   jax.ShapeDtypeStruct((n, LANES), F32), jax.ShapeDtypeStruct((1, LANES), F32)],
        scratch_shapes=[pltpu.VMEM((1, LANES), F32)],
        compiler_params=_cparams(("arbitrary",)),
        name="router",
    )(logits, bias_pad)


def _swiglu(xb, wg, wu, wd):
    g = _dot(xb, wg)
    u = _dot(xb, wu)
    act = (g * _sigmoid(g) * u).astype(BF16)
    return _dot(act, wd)


def _dispatch_kernel(e_ref, rk_ref, ps_ref, cnt_ref, h_ref, wg_ref, wu_ref, wd_ref, xs_ref, sh_ref, zbuf, sem, zsem,
                     *, tt):
    bm = EXPERT_BLOCK

    @pl.when(pl.program_id(0) == 0)
    def _():
        zbuf[...] = jnp.zeros_like(zbuf)

        def tail_copy(e):
            last = ps_ref[e] + ((cnt_ref[e] + bm - 1) & -bm) - bm
            return pltpu.make_async_copy(zbuf, xs_ref.at[pl.ds(pl.multiple_of(last, bm), bm)], zsem)

        def tail_start(e, c):
            @pl.when((cnt_ref[e] & (bm - 1)) != 0)
            def _():
                tail_copy(e).start()
            return c

        def tail_wait(e, c):
            @pl.when((cnt_ref[e] & (bm - 1)) != 0)
            def _():
                tail_copy(e).wait()
            return c

        lax.fori_loop(0, N_EXPERTS, tail_start, 0)
        lax.fori_loop(0, N_EXPERTS, tail_wait, 0)

    def row_copy(src_row, dst_row):
        return pltpu.make_async_copy(h_ref.at[pl.ds(src_row, 1)], xs_ref.at[pl.ds(dst_row, 1)], sem)

    def start(t, c):
        for k in range(TOP_K):
            j = t * TOP_K + k
            row_copy(t, ps_ref[e_ref[j]] + rk_ref[j]).start(priority=k % 2)
        return c

    lax.fori_loop(0, tt, start, 0)

    sh_ref[...] = _to_row_tiles(_swiglu(_from_row_tiles(h_ref[...]), wg_ref[...], wu_ref[...], wd_ref[...]))

    for _ in range(TOP_K):
        pltpu.make_async_copy(h_ref, xs_ref.at[pl.ds(0, tt)], sem).wait()


def _dispatch(e_flat, rk_flat, pstarts, counts, h_rows, cap, wg, wu, wd):
    n = h_rows.shape[0]
    tt = min(512, n)
    smem_blk = pl.BlockSpec((tt * TOP_K,), lambda i: (i,), memory_space=pltpu.SMEM)
    rows = pl.BlockSpec((tt,) + ROW_TILE, lambda i: (i, 0, 0))

    def const(w):
        return pl.BlockSpec(w.shape, lambda i: (0, 0), pipeline_mode=pl.Buffered(1))

    return pl.pallas_call(
        functools.partial(_dispatch_kernel, tt=tt),
        grid=(n // tt,),
        in_specs=[smem_blk, smem_blk,
                  pl.BlockSpec(memory_space=pltpu.SMEM),
                  pl.BlockSpec(memory_space=pltpu.SMEM),
                  rows, const(wg), const(wu), const(wd)],
        out_specs=[pl.BlockSpec(memory_space=pl.ANY), rows],
        out_shape=[jax.ShapeDtypeStruct((cap,) + ROW_TILE, h_rows.dtype),
                   jax.ShapeDtypeStruct((n,) + ROW_TILE, h_rows.dtype)],
        scratch_shapes=[pltpu.VMEM((EXPERT_BLOCK,) + ROW_TILE, h_rows.dtype),
                        pltpu.SemaphoreType.DMA(()), pltpu.SemaphoreType.DMA(())],
        compiler_params=_cparams(("arbitrary",)),
        name="dispatch",
    )(e_flat, rk_flat, pstarts, counts, h_rows, wg, wu, wd)


def _expert_kernel(be_ref, nu_ref, nx_ref, sl_ref, x_ref, wg_hbm, wu_hbm, wd_hbm, o_ref,
                   wg_buf, wu_buf, wd_buf, wg_bf, wu_bf, wd_bf, sem):
    i = pl.program_id(0)
    used = i < nu_ref[0]
    e = be_ref[i]
    first = (i == 0) | (e != be_ref[jnp.maximum(i - 1, 0)])
    slot = sl_ref[e]
    nxt = nx_ref[e]

    def fetch(expert, s):
        return (pltpu.make_async_copy(wg_hbm.at[expert], wg_buf.at[s], sem.at[s, 0]),
                pltpu.make_async_copy(wu_hbm.at[expert], wu_buf.at[s], sem.at[s, 1]),
                pltpu.make_async_copy(wd_hbm.at[expert], wd_buf.at[s], sem.at[s, 2]))

    @pl.when(used & (i == 0))
    def _():
        for c in fetch(e, slot):
            c.start()

    @pl.when(used & first)
    def _():
        for c in fetch(e, slot):
            c.wait()

        @pl.when(nxt < N_EXPERTS)
        def _():
            for c in fetch(nxt, 1 - slot):
                c.start()

        wg_bf[...] = wg_buf[slot].astype(BF16)
        wu_bf[...] = wu_buf[slot].astype(BF16)
        wd_bf[...] = wd_buf[slot].astype(BF16)

    @pl.when(used)
    def _():
        y = _swiglu(_from_row_tiles(x_ref[...]), wg_bf[...], wu_bf[...], wd_bf[...])
        o_ref[...] = _to_row_tiles(y)

    @pl.when(jnp.logical_not(used))
    def _():
        o_ref[...] = jnp.zeros_like(o_ref)


def _experts(block_expert, n_used, next_expert, slot, xs, wg, wu, wd):
    cap = xs.shape[0]
    bm = EXPERT_BLOCK
    de = wg.shape[2]
    grid_spec = pltpu.PrefetchScalarGridSpec(
        num_scalar_prefetch=4,
        grid=(cap // bm,),
        in_specs=[pl.BlockSpec((bm,) + ROW_TILE, lambda i, be, nu, nx, sl: (jnp.minimum(i, nu[0] - 1), 0, 0)),
                  pl.BlockSpec(memory_space=pl.ANY),
                  pl.BlockSpec(memory_space=pl.ANY),
                  pl.BlockSpec(memory_space=pl.ANY)],
        out_specs=pl.BlockSpec((bm,) + ROW_TILE, lambda i, be, nu, nx, sl: (i, 0, 0)),
        scratch_shapes=[pltpu.VMEM((2, D_MODEL, de), F32), pltpu.VMEM((2, D_MODEL, de), F32),
                        pltpu.VMEM((2, de, D_MODEL), F32),
                        pltpu.VMEM((D_MODEL, de), BF16), pltpu.VMEM((D_MODEL, de), BF16),
                        pltpu.VMEM((de, D_MODEL), BF16), pltpu.SemaphoreType.DMA((2, 3))],
    )
    return pl.pallas_call(
        _expert_kernel,
        grid_spec=grid_spec,
        out_shape=jax.ShapeDtypeStruct((cap,) + ROW_TILE, BF16),
        compiler_params=_cparams(("arbitrary",)),
        name="experts",
    )(block_expert, n_used, next_expert, slot, xs, wg, wu, wd)


def _combine_kernel(e_ref, rk_ref, en_ref, rkn_ref, ps_ref, ys_ref, w_ref, sh_ref, x_ref, gt_ref, g_ref, o_ref,
                    buf, sem, *, tt):
    step = pl.program_id(0) * pl.num_programs(1) + pl.program_id(1)
    n_steps = pl.num_programs(0) * pl.num_programs(1)
    slot = step & 1

    def gather(idx_ref, rank_ref, dst_slot):
        def start(t, c):
            for k in range(TOP_K):
                j = t * TOP_K + k
                pltpu.make_async_copy(ys_ref.at[pl.ds(ps_ref[idx_ref[j]] + rank_ref[j], 1)],
                                      buf.at[dst_slot, k, pl.ds(t, 1)], sem.at[dst_slot]).start(priority=k % 2)
            return c

        lax.fori_loop(0, tt, start, 0)

    @pl.when(step == 0)
    def _():
        gather(e_ref, rk_ref, 0)

    @pl.when(step + 1 < n_steps)
    def _():
        gather(en_ref, rkn_ref, 1 - slot)

    for k in range(TOP_K):
        pltpu.make_async_copy(ys_ref.at[pl.ds(0, tt)], buf.at[slot, k], sem.at[slot]).wait()

    w = w_ref[0]
    moe = _from_row_tiles(sh_ref[0]).astype(F32)
    for k in range(TOP_K):
        moe = moe + w[:, k:k + 1] * _from_row_tiles(buf[slot, k]).astype(F32)
    x2 = x_ref[0] + gt_ref[0] * moe
    ms = jnp.mean(x2 * x2, axis=-1, keepdims=True)
    o_ref[0] = x2 * lax.rsqrt(ms + NORM_EPS) * g_ref[...]


def _combine(e_flat, rk_flat, pstarts, ys, wsel, shared, x1, gt2, final_g):
    b, t, d = x1.shape
    tt = 128
    nt = t // tt
    last = b * nt - 1
    smem_blk = pl.BlockSpec((tt * TOP_K,), lambda i, j: (i * nt + j,), memory_space=pltpu.SMEM)
    smem_nxt = pl.BlockSpec((tt * TOP_K,), lambda i, j: (jnp.minimum(i * nt + j + 1, last),),
                            memory_space=pltpu.SMEM)
    return pl.pallas_call(
        functools.partial(_combine_kernel, tt=tt),
        grid=(b, nt),
        in_specs=[smem_blk, smem_blk, smem_nxt, smem_nxt,
                  pl.BlockSpec(memory_space=pltpu.SMEM),
                  pl.BlockSpec(memory_space=pl.ANY),
                  pl.BlockSpec((1, tt, LANES), lambda i, j: (i, j, 0)),
                  pl.BlockSpec((1, tt) + ROW_TILE, lambda i, j: (i, j, 0, 0)),
                  pl.BlockSpec((1, tt, d), lambda i, j: (i, j, 0)),
                  pl.BlockSpec((1, 1, d), lambda i, j: (i, 0, 0)),
                  pl.BlockSpec((1, d), lambda i, j: (0, 0))],
        out_specs=pl.BlockSpec((1, tt, d), lambda i, j: (i, j, 0)),
        out_shape=jax.ShapeDtypeStruct((b, t, d), F32),
        scratch_shapes=[pltpu.VMEM((2, TOP_K, tt) + ROW_TILE, BF16), pltpu.SemaphoreType.DMA((2,))],
        compiler_params=_cparams(("arbitrary", "arbitrary")),
        name="combine",
    )(e_flat, rk_flat, e_flat, rk_flat, pstarts, ys, wsel, shared, x1, gt2, final_g.reshape(1, d))


def _pad_rows(w, rows):
    return jnp.zeros((rows, w.shape[1]), w.dtype).at[:w.shape[0]].set(w)


def _lora_pad_cols(w):
    out = jnp.zeros(w.shape[:-1] + (D_LORA_PAD,), w.dtype)
    for i in range(4):
        out = out.at[..., i * LANES:i * LANES + D_LORA].set(w[..., i * D_LORA:(i + 1) * D_LORA])
    return out.at[..., 4 * LANES:].set(w[..., 4 * D_LORA:])


def kernel(x, c, ctx, c_ctx, norm1_g, norm2_g, ada_w, ada_b, w_in, shift_mu, pool_w, pool_scale, w_pool_out, decay_w0, decay_w2, iclr_a0, iclr_a2, gate_g2, k_k, k_a, r_k, lnx_w, lnx_b, w_rwkv_out, w_out, router_w, router_bias, exp_w_gate, exp_w_up, exp_w_down, shared_w_gate, shared_w_up, shared_w_down, final_g):
    B, T, D = x.shape
    TC = ctx.shape[1]
    n = B * T
    l = 0

    w_in_l = w_in[l]
    w_u = w_in_l[:, :D_POOL].astype(BF16)
    w_rkv = w_in_l[:, D_POOL:D_POOL + 3 * D_ATT].astype(BF16)
    w_lora = _lora_pad_cols(w_in_l[:, D_POOL + 3 * D_ATT:D_POOL + 3 * D_ATT + 4 * D_LORA + D_GATE_LORA]).astype(BF16)
    w_gates = w_in_l[:, D_POOL + 3 * D_ATT + 4 * D_LORA + D_GATE_LORA:].astype(BF16)
    mu = shift_mu[l]
    row = lambda a: a.reshape(1, -1)
    pw = {
        "mu_rkv": row(mu[:3 * D_ATT]),
        "mu_lora": row(_lora_pad_cols(mu[3 * D_ATT:])),
        "w2f": _pad_rows(0.5 * decay_w2[l, 0], LANES).astype(BF16),
        "w2b": _pad_rows(0.5 * decay_w2[l, 1], LANES).astype(BF16),
        "a2f": _pad_rows(0.5 * iclr_a2[l, 0], LANES).astype(BF16),
        "a2b": _pad_rows(0.5 * iclr_a2[l, 1], LANES).astype(BF16),
        "w0f": row(0.5 * decay_w0[l, 0]), "w0b": row(0.5 * decay_w0[l, 1]),
        "a0f": row(0.5 * iclr_a0[l, 0]), "a0b": row(0.5 * iclr_a0[l, 1]),
        "k_k": row(k_k[l]), "k_a": row(k_a[l]), "r_k": row(r_k[l]),
    }

    cstack = jnp.zeros((8, D), F32).at[:B].set(c).at[B].set(c_ctx)
    mod = _ada_mod(cstack, ada_w[l], ada_b[l])
    sh1, sc1, gt1, sh2, sc2, gt2 = [mod[:B, i * D:(i + 1) * D].reshape(B, 1, D) for i in range(6)]
    sh1c = jnp.broadcast_to(mod[B, 0:D].reshape(1, 1, D), (B, 1, D))
    sc1c = jnp.broadcast_to(mod[B, D:2 * D].reshape(1, 1, D), (B, 1, D))

    hc = _norm_mod(ctx, norm1_g[l], sc1c, sh1c, TC).reshape(B * TC, D)
    rkv_c = _matmul(hc, w_rkv, F32, 512, 512).reshape(B, TC, 3 * D_ATT)
    lora_c = _matmul(hc, w_lora, F32, 512, D_LORA_PAD).reshape(B, TC, D_LORA_PAD)
    pc = _prepare(rkv_c, lora_c, pw, grid_mode=False)
    flat = lambda a: a.reshape(B * N_PAIRS, a.shape[2], LANES)
    s0 = jnp.zeros((B * N_PAIRS, LANES, LANES), F32)
    vc = flat(pc[10])
    _, st_f = _scan(*[flat(a) for a in pc[0:5]], vc, s0, reverse=False, emit=False)
    _, st_b = _scan(*[flat(a) for a in pc[5:10]], vc, s0, reverse=True, emit=False)

    h = _norm_mod(x, norm1_g[l], sc1, sh1, 512).reshape(n, D)
    u = _matmul(h, w_u, F32, 2048, 512).reshape(B, T, D_POOL)
    rkv = _matmul(h, w_rkv, F32, 2048, 1024).reshape(B, T, 3 * D_ATT)
    lora = _matmul(h, w_lora, F32, 2048, D_LORA_PAD).reshape(B, T, D_LORA_PAD)
    gates = _matmul(h, w_gates, BF16, 2048, 1024).reshape(B, T, 2 * D)
    pp = _prepare(rkv, lora, pw, grid_mode=True)
    v_, bonus, gd = flat(pp[10]), pp[11], pp[12]
    y_f, _ = _scan(*[flat(a) for a in pp[0:5]], v_, st_f, reverse=False, emit=True)
    y_b, _ = _scan(*[flat(a) for a in pp[5:10]], v_, st_b, reverse=True, emit=True)
    y_rwkv = _readout(y_f.reshape(B, N_PAIRS, T, LANES), y_b.reshape(B, N_PAIRS, T, LANES), bonus, gd,
                      row(lnx_w[l]), row(lnx_b[l]), gate_g2[l].astype(BF16), w_rwkv_out[l].astype(BF16))
    y_pool = _pool_branch(u, pool_w[l].astype(BF16), row(pool_scale[l]), w_pool_out[l].astype(BF16))

    router_w_pad = jnp.zeros((D, LANES), F32).at[:, :N_EXPERTS].set(router_w[l])
    x1, h2, logits = _merge(y_pool, y_rwkv, gates, x, gt1, sc2, sh2, norm2_g[l],
                                w_out[l].astype(BF16), router_w_pad)

    bias_pad = jnp.zeros((1, LANES), F32).at[0, :N_EXPERTS].set(router_bias[l])
    e_idx, e_rank, wsel, counts = _router(logits.reshape(n, LANES), bias_pad)
    e_flat = e_idx[:, :TOP_K].reshape(-1)
    rk_flat = e_rank[:, :TOP_K].reshape(-1)
    bm = EXPERT_BLOCK
    cnt = counts[0, :N_EXPERTS].astype(I32)
    padded = (cnt + bm - 1) // bm * bm
    pend = jnp.cumsum(padded)
    pstarts = (pend - padded).astype(I32)
    cap = n * TOP_K + N_EXPERTS * bm
    n_blocks = cap // bm
    block_start = jnp.arange(n_blocks, dtype=I32) * bm
    block_expert = jnp.minimum(jnp.sum(block_start[:, None] >= pend[None, :], axis=-1), N_EXPERTS - 1).astype(I32)
    n_used = (pend[-1] // bm).astype(I32).reshape(1)
    has = cnt > 0
    ids = jnp.where(has, jnp.arange(N_EXPERTS, dtype=I32), N_EXPERTS)
    next_e = jnp.concatenate([lax.cummin(ids, axis=0, reverse=True)[1:], jnp.full((1,), N_EXPERTS, I32)])
    slot_e = ((jnp.cumsum(has.astype(I32)) - 1) & 1).astype(I32)

    h2 = h2.reshape((n,) + ROW_TILE)
    xs, shared = _dispatch(e_flat, rk_flat, pstarts, cnt, h2, cap,
                           shared_w_gate[l].astype(BF16), shared_w_up[l].astype(BF16),
                           shared_w_down[l].astype(BF16))
    ys = _experts(block_expert, n_used, next_e, slot_e, xs, exp_w_gate[l], exp_w_up[l], exp_w_down[l])
    return _combine(e_flat, rk_flat, pstarts, ys, wsel.reshape(B, T, LANES),
                    shared.reshape((B, T) + ROW_TILE), x1, gt2, final_g)
```

```python
import functools

import jax
import jax.numpy as jnp
from jax import lax
from jax.experimental import pallas as pl
from jax.experimental.pallas import tpu as pltpu

F32 = jnp.float32
BF16 = jnp.bfloat16
I32 = jnp.int32

D_MODEL = 2048
GRID_W = 64
POOL_WINDOWS = (2, 4, 8, 16)
POOL_GROUP = 256
D_POOL = 1024
HEAD = 64
N_HEADS = 32
N_PAIRS = N_HEADS // 2
D_ATT = 2048
D_LORA = 96
D_GATE_LORA = 256
D_LORA_PAD = 768
GN_EPS = 64e-5
NORM_EPS = 1e-6
N_EXPERTS = 64
TOP_K = 8
N_GROUPS = 8
TOPK_GROUPS = 4
D_EXPERT = 512
ROUTED_SCALE = 2.5
EXP_M05 = 0.6065306597126334

LANES = 128
CHUNK = 64
EXPERT_BLOCK = 512
VMEM_LIMIT = 56 * 1024 * 1024


def _cparams(sem):
    return pltpu.CompilerParams(dimension_semantics=sem, vmem_limit_bytes=VMEM_LIMIT)


def _dot(a, b):
    return jnp.dot(a, b, preferred_element_type=F32)


def _bmm(a, b):
    return lax.dot_general(a, b, (((2,), (1,)), ((0,), (0,))), preferred_element_type=F32)


def _bmm_nt(a, b):
    return lax.dot_general(a, b, (((2,), (2,)), ((0,), (0,))), preferred_element_type=F32)


def _bmm_tn(a, b):
    return lax.dot_general(a, b, (((1,), (1,)), ((0,), (0,))), preferred_element_type=F32)


def _sigmoid(x):
    return 0.5 * jnp.tanh(0.5 * x) + 0.5


def _split2(x):
    hi = x.astype(BF16)
    lo = (x - hi.astype(F32)).astype(BF16)
    return hi, lo


def _iota(shape, axis):
    return lax.broadcasted_iota(I32, shape, axis)


def _ada_kernel(c_ref, w_ref, b_ref, o_ref):
    c = c_ref[...]
    a = c * _sigmoid(c)
    o_ref[...] = _dot(a.astype(BF16), w_ref[...].astype(BF16)) + b_ref[...]


def _ada_mod(cstack, ada_w, ada_b):
    m, d = cstack.shape
    n = ada_w.shape[1]
    tn = 1024
    return pl.pallas_call(
        _ada_kernel,
        grid=(n // tn,),
        in_specs=[pl.BlockSpec((m, d), lambda j: (0, 0)),
                  pl.BlockSpec((d, tn), lambda j: (0, j)),
                  pl.BlockSpec((1, tn), lambda j: (0, j))],
        out_specs=pl.BlockSpec((m, tn), lambda j: (0, j)),
        out_shape=jax.ShapeDtypeStruct((m, n), F32),
        compiler_params=_cparams(("parallel",)),
        name="ada_mod",
    )(cstack, ada_w, ada_b.reshape(1, n))


def _norm_mod_kernel(x_ref, g_ref, sc_ref, sh_ref, o_ref):
    x = x_ref[0]
    ms = jnp.mean(x * x, axis=-1, keepdims=True)
    y = x * lax.rsqrt(ms + NORM_EPS) * g_ref[...]
    o_ref[0] = (y * (1.0 + sc_ref[0]) + sh_ref[0]).astype(o_ref.dtype)


def _norm_mod(x, g, sc, sh, tt):
    b, t, d = x.shape
    return pl.pallas_call(
        _norm_mod_kernel,
        grid=(b, t // tt),
        in_specs=[pl.BlockSpec((1, tt, d), lambda i, j: (i, j, 0)),
                  pl.BlockSpec((1, d), lambda i, j: (0, 0)),
                  pl.BlockSpec((1, 1, d), lambda i, j: (i, 0, 0)),
                  pl.BlockSpec((1, 1, d), lambda i, j: (i, 0, 0))],
        out_specs=pl.BlockSpec((1, tt, d), lambda i, j: (i, j, 0)),
        out_shape=jax.ShapeDtypeStruct((b, t, d), BF16),
        compiler_params=_cparams(("parallel", "parallel")),
        name="norm_mod",
    )(x, g.reshape(1, d), sc, sh)


def _mm_kernel(a_ref, b_ref, o_ref):
    o_ref[...] = _dot(a_ref[...], b_ref[...]).astype(o_ref.dtype)


def _matmul(a, b, out_dtype, tm, tn):
    m, k = a.shape
    n = b.shape[1]
    tm = min(tm, m)
    return pl.pallas_call(
        _mm_kernel,
        grid=(m // tm, n // tn),
        in_specs=[pl.BlockSpec((tm, k), lambda i, j: (i, 0)),
                  pl.BlockSpec((k, tn), lambda i, j: (0, j))],
        out_specs=pl.BlockSpec((tm, tn), lambda i, j: (i, j)),
        out_shape=jax.ShapeDtypeStruct((m, n), out_dtype),
        compiler_params=_cparams(("parallel", "parallel")),
        name="matmul",
    )(a, b)


def _shift_grid(x, prev, nxt, first, last):
    tt, c = x.shape
    col = _iota((tt, c), 0) & (GRID_W - 1)
    m = _iota((tt, c), 1) & 3
    horiz = jnp.where(m == 0, pltpu.roll(x, 1, 0), pltpu.roll(x, tt - 1, 0))
    at_border = ((m == 0) & (col == 0)) | ((m == 1) & (col == GRID_W - 1))
    horiz = jnp.where(at_border, 0.0, horiz)
    prev = jnp.where(first, 0.0, prev)
    nxt = jnp.where(last, 0.0, nxt)
    if tt > GRID_W:
        up = jnp.concatenate([prev, x[:tt - GRID_W]], axis=0)
        down = jnp.concatenate([x[GRID_W:], nxt], axis=0)
    else:
        up, down = prev, nxt
    return jnp.where(m < 2, horiz, jnp.where(m == 2, up, down))


def _shift_seq(x, prev8, next8, first, last):
    t, c = x.shape
    row = _iota((t, c), 0)
    odd = (_iota((t, c), 1) & 1) == 1
    before = jnp.where(first, 0.0, prev8[7:8])
    after = jnp.where(last, 0.0, next8[0:1])
    prev = jnp.where(row == 0, before, pltpu.roll(x, 1, 0))
    nxt = jnp.where(row == t - 1, after, pltpu.roll(x, t - 1, 0))
    return jnp.where(odd, nxt, prev)


def _head_sum(x):
    w = 2 * LANES
    ones = (_iota((w, w), 0) >> 6 == _iota((w, w), 1) >> 6).astype(BF16)
    outs = []
    for c in range(x.shape[1] // w):
        hi, lo = _split2(x[:, c * w:(c + 1) * w])
        outs.append(_dot(hi, ones) + _dot(lo, ones))
    return jnp.concatenate(outs, axis=1)


def _prepare_kernel(*refs, grid_mode):
    (r_ref, rp_ref, rn_ref, k_ref, kp_ref, kn_ref, v_ref, vp_ref, vn_ref,
     l_ref, lp_ref, ln_ref) = refs[:12]
    rest = refs[12:]
    (mur_ref, muk_ref, muv_ref, mul_ref, w2f_ref, w2b_ref, a2f_ref, a2b_ref,
     w0f_ref, w0b_ref, a0f_ref, a0b_ref, kk_ref, ka_ref, rk_ref,
     oaf_ref, obf_ref, okf_ref, orf_ref, owf_ref, oab_ref, obb_ref, okb_ref, orb_ref, owb_ref,
     ov_ref, obon_ref, ogd_ref) = rest

    first = pl.program_id(1) == 0
    last = pl.program_id(1) == pl.num_programs(1) - 1
    shift = _shift_grid if grid_mode else _shift_seq

    def mix(x_ref, p_ref, n_ref, mu_ref):
        x = x_ref[0]
        return x + (shift(x, p_ref[0], n_ref[0], first, last) - x) * mu_ref[...]

    r = mix(r_ref, rp_ref, rn_ref, mur_ref)
    k = mix(k_ref, kp_ref, kn_ref, muk_ref)
    v = mix(v_ref, vp_ref, vn_ref, muv_ref)
    lo = mix(l_ref, lp_ref, ln_ref, mul_ref)

    th = jnp.tanh(lo[:, :2 * LANES]).astype(BF16)
    tzf = jnp.tanh(w0f_ref[...] + _dot(th[:, :LANES], w2f_ref[...]))
    tzb = jnp.tanh(w0b_ref[...] + _dot(th[:, LANES:], w2b_ref[...]))
    c = -0.5 * EXP_M05
    lwf = tzf * c + c
    lwb = tzb * c + c
    ad = lo[:, 2 * LANES:4 * LANES].astype(BF16)
    af = 0.5 * jnp.tanh(a0f_ref[...] + _dot(ad[:, :LANES], a2f_ref[...])) + 0.5
    ab = 0.5 * jnp.tanh(a0b_ref[...] + _dot(ad[:, LANES:], a2b_ref[...])) + 0.5
    ogd_ref[0] = lo[:, 4 * LANES:]

    kk = k * kk_ref[...]
    kk = kk * lax.rsqrt(_head_sum(kk * kk) + 1e-12)
    ka = ka_ref[...]
    one_minus_ka = 1.0 - ka
    kf = k * (af * ka + one_minus_ka)
    kb = k * (ab * ka + one_minus_ka)
    bonus = _head_sum(r * (kf + kb) * rk_ref[...]) * v

    tt = r.shape[0]
    t2 = _iota((tt, tt), 0)
    s2 = _iota((tt, tt), 1)
    same = (t2 >> 6) == (s2 >> 6)

    def scan_operands(lw, b, kd, reverse):
        tri = (same & ((s2 >= t2) if reverse else (s2 <= t2))).astype(BF16)
        hi = lw.astype(BF16)
        r1 = lw - hi.astype(F32)
        mid = r1.astype(BF16)
        lo3 = (r1 - mid.astype(F32)).astype(BF16)
        cum = _dot(tri, hi) + _dot(tri, mid) + _dot(tri, lo3)
        w_inc = jnp.exp(cum)
        w_inv = jnp.exp(-cum)
        ends = [w_inc[c * CHUNK:c * CHUNK + 1] if reverse else w_inc[(c + 1) * CHUNK - 1:(c + 1) * CHUNK]
                for c in range(tt // CHUNK)]
        w_last = jnp.concatenate(ends + [jnp.zeros((8 - len(ends), lw.shape[1]), F32)], axis=0)
        return jnp.exp(cum - lw) * kk, b * w_inv, kd * w_inv, w_inc * r, w_last

    al_f, be_f, ka_f, rh_f, wl_f = scan_operands(lwf, kk * af, kf, False)
    al_b, be_b, ka_b, rh_b, wl_b = scan_operands(lwb, kk * ab, kb, True)
    outs = ((oaf_ref, al_f), (obf_ref, be_f), (okf_ref, ka_f), (orf_ref, rh_f), (owf_ref, wl_f),
            (oab_ref, al_b), (obb_ref, be_b), (okb_ref, ka_b), (orb_ref, rh_b), (owb_ref, wl_b),
            (ov_ref, v), (obon_ref, bonus))
    for o_ref, val in outs:
        for p in range(N_PAIRS):
            o_ref[0, p] = val[:, p * LANES:(p + 1) * LANES].astype(o_ref.dtype)


def _prepare(rkv, lora, pw, grid_mode):
    b, t, _ = rkv.shape
    d = D_ATT
    tt = 2 * GRID_W
    halo = GRID_W if grid_mode else 8
    hpt = tt // halo
    nhalo = t // halo
    grid = (b, t // tt)

    def tile_specs(w, c):
        return [pl.BlockSpec((1, tt, w), lambda i, j: (i, j, c)),
                pl.BlockSpec((1, halo, w), lambda i, j: (i, jnp.maximum(j * hpt - 1, 0), c)),
                pl.BlockSpec((1, halo, w), lambda i, j: (i, jnp.minimum((j + 1) * hpt, nhalo - 1), c))]

    in_specs = []
    args = []
    for c in range(3):
        in_specs += tile_specs(d, c)
        args += [rkv, rkv, rkv]
    in_specs += tile_specs(D_LORA_PAD, 0)
    args += [lora, lora, lora]

    def vec(c=0, w=d):
        return pl.BlockSpec((1, w), lambda i, j, c=c: (0, c))

    def full(shape):
        return pl.BlockSpec(shape, lambda i, j: (0,) * len(shape))

    in_specs += [vec(0), vec(1), vec(2), vec(0, D_LORA_PAD)]
    args += [pw["mu_rkv"], pw["mu_rkv"], pw["mu_rkv"], pw["mu_lora"]]
    in_specs += [full((LANES, d))] * 4
    args += [pw["w2f"], pw["w2b"], pw["a2f"], pw["a2b"]]
    in_specs += [vec()] * 7
    args += [pw["w0f"], pw["w0b"], pw["a0f"], pw["a0b"], pw["k_k"], pw["k_a"], pw["r_k"]]

    pair_spec = pl.BlockSpec((1, N_PAIRS, tt, LANES), lambda i, j: (i, 0, j, 0))
    wl_spec = pl.BlockSpec((1, N_PAIRS, 8, LANES), lambda i, j: (i, 0, j, 0))

    def pair(dtype):
        return jax.ShapeDtypeStruct((b, N_PAIRS, t, LANES), dtype)

    wl = jax.ShapeDtypeStruct((b, N_PAIRS, t // tt * 8, LANES), F32)
    direction = [pair(BF16)] * 4 + [wl]
    out_shape = direction * 2 + [pair(BF16), pair(F32), jax.ShapeDtypeStruct((b, t, D_GATE_LORA), F32)]
    out_specs = ([pair_spec] * 4 + [wl_spec]) * 2 + [pair_spec, pair_spec,
                                                    pl.BlockSpec((1, tt, D_GATE_LORA), lambda i, j: (i, j, 0))]
    return pl.pallas_call(
        functools.partial(_prepare_kernel, grid_mode=grid_mode),
        grid=grid, in_specs=in_specs, out_specs=out_specs, out_shape=out_shape,
        compiler_params=_cparams(("parallel", "parallel")),
        name="prepare_grid" if grid_mode else "prepare_seq",
    )(*args)


def _scan_kernel(*refs, reverse, pairs, tb, emit):
    al_ref, be_ref, ka_ref, rh_ref, wl_ref, v_ref, s0_ref = refs[:7]
    if emit:
        y_ref, st_ref, s_scr = refs[7:]
    else:
        st_ref, s_scr = refs[7:]
    L = CHUNK
    n_chunks = tb // L

    @pl.when(pl.program_id(1) == 0)
    def _():
        s_scr[...] = s0_ref[...]

    t2 = _iota((2 * L, 2 * L), 0)
    s2 = _iota((2 * L, 2 * L), 1)
    same = (t2 >> 6) == (s2 >> 6)
    tl = t2 & (L - 1)
    sl = s2 & (L - 1)
    strict = same & ((sl > tl) if reverse else (sl < tl))
    incl = same & ((sl >= tl) if reverse else (sl <= tl))
    eye = (t2 == s2).astype(F32)
    head_a = _iota((pairs, L, LANES), 2) < HEAD
    zero = jnp.zeros((), BF16)

    def stack(ref, rows):
        x = ref[:, rows, :]
        return jnp.concatenate([jnp.where(head_a, x, zero), jnp.where(head_a, zero, x)], axis=1)

    def chunk(ci, carry):
        cc = (n_chunks - 1 - ci) if reverse else ci
        rows = pl.ds(pl.multiple_of(cc * L, L), L)
        a_s = stack(al_ref, rows)
        b_s = stack(be_ref, rows)
        k_s = stack(ka_ref, rows)
        v_s = stack(v_ref, rows)
        lhs = jnp.concatenate([a_s, stack(rh_ref, rows)], axis=1) if emit else a_s
        xb = _bmm_nt(lhs, b_s)
        xk = _bmm_nt(lhs, k_s)
        m1 = jnp.where(strict, xb[:, :2 * L], 0.0)
        m2 = jnp.where(strict, xk[:, :2 * L], 0.0)
        xp = -m1
        tinv = eye + xp
        for _ in range(5):
            xq = xp.astype(BF16)
            xp = _bmm(xq, xq)
            tinv = tinv + _bmm(tinv.astype(BF16), xp.astype(BF16))
        s = s_scr[...]
        xs = _bmm_nt(lhs, s.astype(BF16))
        if emit:
            n1 = jnp.where(incl, xb[:, 2 * L:], 0.0)
            n2 = jnp.where(incl, xk[:, 2 * L:], 0.0)
            xv = _bmm(jnp.concatenate([m2, n2], axis=1).astype(BF16), v_s)
        else:
            xv = _bmm(m2.astype(BF16), v_s)
        g = xs[:, :2 * L] + xv[:, :2 * L]
        u_s = (-_bmm(tinv.astype(BF16), g.astype(BF16))).astype(BF16)
        if emit:
            y2 = xs[:, 2 * L:] + xv[:, 2 * L:] + _bmm(n1.astype(BF16), u_s)
            y_ref[:, rows, :] = y2[:, :L] + y2[:, L:]
        w_last = wl_ref[:, pl.ds((cc >> 1) * 8 + (cc & 1), 1), :]
        upd = _bmm_tn(jnp.concatenate([u_s, v_s], axis=1), jnp.concatenate([b_s, k_s], axis=1))
        s_scr[...] = (s + upd) * w_last
        return carry

    lax.fori_loop(0, n_chunks, chunk, 0)

    @pl.when(pl.program_id(1) == pl.num_programs(1) - 1)
    def _():
        st_ref[...] = s_scr[...]


def _scan(al, be, ka, rh, wl, v, s0, reverse, emit, pairs=32, tb=256):
    bp, t, _ = al.shape
    pairs = min(pairs, bp)
    tb = min(t, tb)
    nb = t // tb

    def tmap(g, c):
        return (g, (nb - 1 - c) if reverse else c, 0)

    data = pl.BlockSpec((pairs, tb, LANES), tmap)
    wl_spec = pl.BlockSpec((pairs, tb // (2 * CHUNK) * 8, LANES), tmap)
    state = pl.BlockSpec((pairs, LANES, LANES), lambda g, c: (g, 0, 0))
    out_shape = [jax.ShapeDtypeStruct((bp, LANES, LANES), F32)]
    out_specs = [state]
    if emit:
        out_shape = [jax.ShapeDtypeStruct((bp, t, LANES), F32)] + out_shape
        out_specs = [data] + out_specs
    res = pl.pallas_call(
        functools.partial(_scan_kernel, reverse=reverse, pairs=pairs, tb=tb, emit=emit),
        grid=(bp // pairs, nb),
        in_specs=[data] * 4 + [wl_spec, data, state],
        out_specs=out_specs, out_shape=out_shape,
        scratch_shapes=[pltpu.VMEM((pairs, LANES, LANES), F32)],
        compiler_params=_cparams(("parallel", "arbitrary")),
        name="scan_" + ("bwd" if reverse else "fwd") + ("_emit" if emit else "_state"),
    )(al, be, ka, rh, wl, v, s0)
    return (res[0], res[1]) if emit else (None, res[0])


def _readout_kernel(yf_ref, yb_ref, bon_ref, gd_ref, lnw_ref, lnb_ref, g2_ref, w_ref, o_ref):
    y = jnp.concatenate([yf_ref[0, p] + yb_ref[0, p] for p in range(N_PAIRS)], axis=1)
    bonus = jnp.concatenate([bon_ref[0, p] for p in range(N_PAIRS)], axis=1)
    mean = _head_sum(y) * (1.0 / HEAD)
    dlt = y - mean
    var = _head_sum(dlt * dlt) * (1.0 / HEAD)
    yn = dlt * lax.rsqrt(var + GN_EPS) * lnw_ref[...] + lnb_ref[...]
    gate = _dot(_sigmoid(gd_ref[0]).astype(BF16), g2_ref[...])
    out = ((yn + bonus) * gate).astype(BF16)
    o_ref[0] = _dot(out, w_ref[...]).astype(o_ref.dtype)


def _readout(yf, yb, bonus, gd, lnw, lnb, g2, w_out):
    b, _, t, _ = yf.shape
    tt = 256
    d = D_ATT
    pair_spec = pl.BlockSpec((1, N_PAIRS, tt, LANES), lambda i, j: (i, 0, j, 0))
    return pl.pallas_call(
        _readout_kernel,
        grid=(b, t // tt),
        in_specs=[pair_spec, pair_spec, pair_spec,
                  pl.BlockSpec((1, tt, D_GATE_LORA), lambda i, j: (i, j, 0)),
                  pl.BlockSpec((1, d), lambda i, j: (0, 0)),
                  pl.BlockSpec((1, d), lambda i, j: (0, 0)),
                  pl.BlockSpec((D_GATE_LORA, d), lambda i, j: (0, 0)),
                  pl.BlockSpec((d, D_MODEL), lambda i, j: (0, 0))],
        out_specs=pl.BlockSpec((1, tt, D_MODEL), lambda i, j: (i, j, 0)),
        out_shape=jax.ShapeDtypeStruct((b, t, D_MODEL), BF16),
        compiler_params=_cparams(("parallel", "parallel")),
        name="readout",
    )(yf, yb, bonus, gd, lnw, lnb, g2, w_out)


def _pool_kernel(u_ref, pw_ref, ps_ref, wo_ref, o_ref):
    u = u_ref[0]
    tt = u.shape[0]
    t2 = _iota((tt, tt), 0)
    s2 = _iota((tt, tt), 1)
    same = (t2 >> 6) == (s2 >> 6)
    tc = t2 & (GRID_W - 1)
    sc = s2 & (GRID_W - 1)
    col = _iota((tt, POOL_GROUP), 0) & (GRID_W - 1)
    ys = []
    for gi, w in enumerate(POOL_WINDOWS):
        ug = u[:, gi * POOL_GROUP:(gi + 1) * POOL_GROUP]
        win = (same & (sc >= tc - w // 2) & (sc < tc + (w - w // 2))).astype(BF16)
        hi, lo = _split2(ug)
        wsum = _dot(win, hi) + _dot(win, lo)
        cnt = (jnp.minimum(col + (w - w // 2), GRID_W) - jnp.maximum(col - w // 2, 0)).astype(F32)
        dlt = wsum / cnt - ug
        ys.append(_dot(dlt.astype(BF16), pw_ref[gi]))
    y1 = jnp.concatenate(ys, axis=1) * ps_ref[...]
    o_ref[0] = _dot(y1.astype(BF16), wo_ref[...]).astype(o_ref.dtype)


def _pool_branch(u, pool_w, pool_scale, w_pool_out):
    b, t, _ = u.shape
    tt = 256
    return pl.pallas_call(
        _pool_kernel,
        grid=(b, t // tt),
        in_specs=[pl.BlockSpec((1, tt, D_POOL), lambda i, j: (i, j, 0)),
                  pl.BlockSpec((4, POOL_GROUP, POOL_GROUP), lambda i, j: (0, 0, 0)),
                  pl.BlockSpec((1, D_POOL), lambda i, j: (0, 0)),
                  pl.BlockSpec((D_POOL, D_MODEL), lambda i, j: (0, 0))],
        out_specs=pl.BlockSpec((1, tt, D_MODEL), lambda i, j: (i, j, 0)),
        out_shape=jax.ShapeDtypeStruct((b, t, D_MODEL), BF16),
        compiler_params=_cparams(("parallel", "parallel")),
        name="pool_branch",
    )(u, pool_w, pool_scale, w_pool_out)


ROW_TILE = (D_MODEL // LANES, LANES)


def _to_row_tiles(x):
    return x.astype(BF16).reshape((x.shape[0],) + ROW_TILE)


def _from_row_tiles(x):
    return x.reshape(x.shape[0], D_MODEL)


def _merge_kernel(yp_ref, yr_ref, gp_ref, gr_ref, x_ref, gt_ref, sc_ref, sh_ref, g_ref, w_ref, rw_ref,
                  x1_ref, h_ref, lg_ref):
    m = (_sigmoid(gp_ref[0].astype(F32)) * yp_ref[0].astype(F32)
         + _sigmoid(gr_ref[0].astype(F32)) * yr_ref[0].astype(F32))
    x1 = x_ref[0] + gt_ref[0] * _dot(m.astype(BF16), w_ref[...])
    x1_ref[0] = x1
    ms = jnp.mean(x1 * x1, axis=-1, keepdims=True)
    h = x1 * lax.rsqrt(ms + NORM_EPS) * g_ref[...]
    h = h * (1.0 + sc_ref[0]) + sh_ref[0]
    h_ref[0] = _to_row_tiles(h)
    hh, hl = _split2(h)
    rh, rl = _split2(rw_ref[...])
    lg_ref[0] = _dot(hh, rh) + _dot(hl, rh) + _dot(hh, rl)


def _merge(y_pool, y_rwkv, gates, x, gt1, sc2, sh2, g2, w_out, router_w_pad):
    b, t, d = x.shape
    tt = 256
    tile = pl.BlockSpec((1, tt, d), lambda i, j: (i, j, 0))
    mod = pl.BlockSpec((1, 1, d), lambda i, j: (i, 0, 0))
    return pl.pallas_call(
        _merge_kernel,
        grid=(b, t // tt),
        in_specs=[tile, tile,
                  pl.BlockSpec((1, tt, d), lambda i, j: (i, j, 0)),
                  pl.BlockSpec((1, tt, d), lambda i, j: (i, j, 1)),
                  tile, mod, mod, mod,
                  pl.BlockSpec((1, d), lambda i, j: (0, 0)),
                  pl.BlockSpec((d, d), lambda i, j: (0, 0)),
                  pl.BlockSpec((d, LANES), lambda i, j: (0, 0))],
        out_specs=[tile,
                   pl.BlockSpec((1, tt) + ROW_TILE, lambda i, j: (i, j, 0, 0)),
                   pl.BlockSpec((1, tt, LANES), lambda i, j: (i, j, 0))],
        out_shape=[jax.ShapeDtypeStruct((b, t, d), F32),
                   jax.ShapeDtypeStruct((b, t) + ROW_TILE, BF16),
                   jax.ShapeDtypeStruct((b, t, LANES), F32)],
        compiler_params=_cparams(("parallel", "parallel")),
        name="merge",
    )(y_pool, y_rwkv, gates, gates, x, gt1, sc2, sh2, g2.reshape(1, d), w_out, router_w_pad)


def _router_kernel(lg_ref, bias_ref, e_ref, rk_ref, w_ref, cnt_ref, carry):
    tt = lg_ref.shape[0]
    shape = (tt, LANES)
    lane = _iota(shape, 1)
    valid = lane < N_EXPERTS
    grp = (lane & (N_EXPERTS - 1)) >> 3
    neg = jnp.float32(-jnp.inf)

    @pl.when(pl.program_id(0) == 0)
    def _():
        carry[...] = jnp.zeros_like(carry)

    scores = _sigmoid(lg_ref[...])
    sel = scores + bias_ref[...]
    sel = jnp.where(valid, sel, pltpu.roll(sel, N_EXPERTS, 1))

    def group_reduce(x, op):
        for sh in (1, 2, 4):
            up = pltpu.roll(x, sh, 1)
            dn = pltpu.roll(x, LANES - sh, 1)
            x = op(x, jnp.where((lane & sh) != 0, up, dn))
        return x

    m1 = group_reduce(sel, jnp.maximum)
    first = group_reduce(jnp.where(sel == m1, lane, LANES), jnp.minimum)
    m2 = group_reduce(jnp.where(lane == first, neg, sel), jnp.maximum)
    gs = m1 + m2
    beaten = jnp.zeros(shape, I32)
    for k in range(1, N_GROUPS):
        other = pltpu.roll(gs, 8 * k, 1)
        og = (grp - k) & (N_GROUPS - 1)
        beaten = beaten + ((other > gs) | ((other == gs) & (og < grp))).astype(I32)
    cur = jnp.where((beaten < TOPK_GROUPS) & valid, sel, neg)

    picked = jnp.zeros(shape, jnp.bool_)
    e_acc = jnp.zeros(shape, I32)
    w_acc = jnp.zeros(shape, F32)
    idxs = []
    for k in range(TOP_K):
        m = jnp.max(cur, axis=1, keepdims=True)
        idx = jnp.min(jnp.where(cur == m, lane, LANES), axis=1, keepdims=True)
        oh = lane == idx
        sc = jnp.sum(jnp.where(oh, scores, 0.0), axis=1, keepdims=True)
        e_acc = jnp.where(lane == k, idx, e_acc)
        w_acc = jnp.where(lane == k, sc, w_acc)
        picked = picked | oh
        cur = jnp.where(oh, neg, cur)
        idxs.append(idx)
    wsum = jnp.sum(w_acc, axis=1, keepdims=True)
    w_ref[...] = w_acc / wsum * ROUTED_SCALE
    e_ref[...] = e_acc

    lower = (_iota((tt, tt), 1) < _iota((tt, tt), 0)).astype(BF16)
    pk = picked.astype(BF16)
    before = _dot(lower, pk) + carry[...]
    r_acc = jnp.zeros(shape, F32)
    for k in range(TOP_K):
        rk = jnp.sum(jnp.where(lane == idxs[k], before, 0.0), axis=1, keepdims=True)
        r_acc = jnp.where(lane == k, rk, r_acc)
    rk_ref[...] = r_acc.astype(I32)
    carry[...] = carry[...] + jnp.sum(picked.astype(F32), axis=0, keepdims=True)
    cnt_ref[...] = carry[...]


def _router(logits, bias_pad):
    n = logits.shape[0]
    tt = min(1024, n)
    tile = pl.BlockSpec((tt, LANES), lambda i: (i, 0))
    row = pl.BlockSpec((1, LANES), lambda i: (0, 0))
    return pl.pallas_call(
        _router_kernel,
        grid=(n // tt,),
        in_specs=[tile, row],
        out_specs=[tile, tile, tile, row],
        out_shape=[jax.ShapeDtypeStruct((n, LANES), I32), jax.ShapeDtypeStruct((n, LANES), I32),
                   jax.ShapeDtypeStruct((n, LANES), F32), jax.ShapeDtypeStruct((1, LANES), F32)],
        scratch_shapes=[pltpu.VMEM((1, LANES), F32)],
        compiler_params=_cparams(("arbitrary",)),
        name="router",
    )(logits, bias_pad)


def _swiglu(xb, wg, wu, wd):
    g = _dot(xb, wg.astype(BF16))
    u = _dot(xb, wu.astype(BF16))
    act = (g * _sigmoid(g) * u).astype(BF16)
    return _dot(act, wd.astype(BF16))


def _dispatch_kernel(e_ref, rk_ref, ps_ref, cnt_ref, h_ref, wg_ref, wu_ref, wd_ref, xs_ref, sh_ref, zbuf, sem, zsem,
                     *, tt):
    bm = EXPERT_BLOCK

    @pl.when(pl.program_id(0) == 0)
    def _():
        zbuf[...] = jnp.zeros_like(zbuf)

        def tail_copy(e):
            last = ps_ref[e] + ((cnt_ref[e] + bm - 1) & -bm) - bm
            return pltpu.make_async_copy(zbuf, xs_ref.at[pl.ds(pl.multiple_of(last, bm), bm)], zsem)

        def tail_start(e, c):
            @pl.when((cnt_ref[e] & (bm - 1)) != 0)
            def _():
                tail_copy(e).start()
            return c

        def tail_wait(e, c):
            @pl.when((cnt_ref[e] & (bm - 1)) != 0)
            def _():
                tail_copy(e).wait()
            return c

        lax.fori_loop(0, N_EXPERTS, tail_start, 0)
        lax.fori_loop(0, N_EXPERTS, tail_wait, 0)

        last_e = N_EXPERTS - 1
        n_used = (ps_ref[last_e] + ((cnt_ref[last_e] + bm - 1) & -bm)) // bm
        n_blocks = xs_ref.shape[0] // bm

        def spare_copy(b):
            return pltpu.make_async_copy(zbuf, xs_ref.at[pl.ds(pl.multiple_of(b * bm, bm), bm)], zsem)

        def spare_start(b, c):
            spare_copy(b).start()
            return c

        def spare_wait(b, c):
            spare_copy(b).wait()
            return c

        lax.fori_loop(n_used, n_blocks, spare_start, 0)
        lax.fori_loop(n_used, n_blocks, spare_wait, 0)

    def row_copy(src_row, dst_row):
        return pltpu.make_async_copy(h_ref.at[pl.ds(src_row, 1)], xs_ref.at[pl.ds(dst_row, 1)], sem)

    def start(t, c):
        for k in range(TOP_K):
            j = t * TOP_K + k
            row_copy(t, ps_ref[e_ref[j]] + rk_ref[j]).start(priority=k % 2)
        return c

    lax.fori_loop(0, tt, start, 0)

    sh_ref[...] = _to_row_tiles(_swiglu(_from_row_tiles(h_ref[...]), wg_ref[...], wu_ref[...], wd_ref[...]))

    for _ in range(TOP_K):
        pltpu.make_async_copy(h_ref, xs_ref.at[pl.ds(0, tt)], sem).wait()


def _dispatch(e_flat, rk_flat, pstarts, counts, h_rows, cap, wg, wu, wd):
    n = h_rows.shape[0]
    tt = min(512, n)
    smem_blk = pl.BlockSpec((tt * TOP_K,), lambda i: (i,), memory_space=pltpu.SMEM)
    rows = pl.BlockSpec((tt,) + ROW_TILE, lambda i: (i, 0, 0))

    def const(w):
        return pl.BlockSpec(w.shape, lambda i: (0, 0), pipeline_mode=pl.Buffered(1))

    return pl.pallas_call(
        functools.partial(_dispatch_kernel, tt=tt),
        grid=(n // tt,),
        in_specs=[smem_blk, smem_blk,
                  pl.BlockSpec(memory_space=pltpu.SMEM),
                  pl.BlockSpec(memory_space=pltpu.SMEM),
                  rows, const(wg), const(wu), const(wd)],
        out_specs=[pl.BlockSpec(memory_space=pl.ANY), rows],
        out_shape=[jax.ShapeDtypeStruct((cap,) + ROW_TILE, h_rows.dtype),
                   jax.ShapeDtypeStruct((n,) + ROW_TILE, h_rows.dtype)],
        scratch_shapes=[pltpu.VMEM((EXPERT_BLOCK,) + ROW_TILE, h_rows.dtype),
                        pltpu.SemaphoreType.DMA(()), pltpu.SemaphoreType.DMA(())],
        compiler_params=_cparams(("arbitrary",)),
        name="dispatch",
    )(e_flat, rk_flat, pstarts, counts, h_rows, wg, wu, wd)


def _expert_kernel(be_ref, nu_ref, nx_ref, sl_ref, x_ref, wg_hbm, wu_hbm, wd_hbm, o_ref, wg_buf, wu_buf, wd_buf, sem):
    i = pl.program_id(0)
    used = i < nu_ref[0]
    e = be_ref[i]
    first = (i == 0) | (e != be_ref[jnp.maximum(i - 1, 0)])
    slot = sl_ref[e]
    nxt = nx_ref[e]

    def fetch(expert, s):
        return (pltpu.make_async_copy(wg_hbm.at[expert], wg_buf.at[s], sem.at[s, 0]),
                pltpu.make_async_copy(wu_hbm.at[expert], wu_buf.at[s], sem.at[s, 1]),
                pltpu.make_async_copy(wd_hbm.at[expert], wd_buf.at[s], sem.at[s, 2]))

    @pl.when(used & (i == 0))
    def _():
        for c in fetch(e, slot):
            c.start()

    @pl.when(used & first)
    def _():
        for c in fetch(e, slot):
            c.wait()

        @pl.when(nxt < N_EXPERTS)
        def _():
            for c in fetch(nxt, 1 - slot):
                c.start()

    @pl.when(used)
    def _():
        y = _swiglu(_from_row_tiles(x_ref[...]), wg_buf[slot], wu_buf[slot], wd_buf[slot])
        o_ref[...] = _to_row_tiles(y)

    @pl.when(jnp.logical_not(used))
    def _():
        o_ref[...] = jnp.zeros_like(o_ref)


def _experts(block_expert, n_used, next_expert, slot, xs, wg, wu, wd):
    cap = xs.shape[0]
    bm = EXPERT_BLOCK
    de = wg.shape[2]
    grid_spec = pltpu.PrefetchScalarGridSpec(
        num_scalar_prefetch=4,
        grid=(cap // bm,),
        in_specs=[pl.BlockSpec((bm,) + ROW_TILE, lambda i, be, nu, nx, sl: (jnp.minimum(i, nu[0] - 1), 0, 0)),
                  pl.BlockSpec(memory_space=pl.ANY),
                  pl.BlockSpec(memory_space=pl.ANY),
                  pl.BlockSpec(memory_space=pl.ANY)],
        out_specs=pl.BlockSpec((bm,) + ROW_TILE, lambda i, be, nu, nx, sl: (i, 0, 0)),
        scratch_shapes=[pltpu.VMEM((2, D_MODEL, de), F32), pltpu.VMEM((2, D_MODEL, de), F32),
                        pltpu.VMEM((2, de, D_MODEL), F32), pltpu.SemaphoreType.DMA((2, 3))],
    )
    return pl.pallas_call(
        _expert_kernel,
        grid_spec=grid_spec,
        out_shape=jax.ShapeDtypeStruct((cap,) + ROW_TILE, BF16),
        compiler_params=_cparams(("arbitrary",)),
        name="experts",
    )(block_expert, n_used, next_expert, slot, xs, wg, wu, wd)


def _combine_kernel(e_ref, rk_ref, en_ref, rkn_ref, ps_ref, ys_ref, w_ref, sh_ref, x_ref, gt_ref, g_ref, o_ref,
                    buf, sem, *, tt):
    step = pl.program_id(0) * pl.num_programs(1) + pl.program_id(1)
    n_steps = pl.num_programs(0) * pl.num_programs(1)
    slot = step & 1

    def gather(idx_ref, rank_ref, dst_slot):
        def start(t, c):
            for k in range(TOP_K):
                j = t * TOP_K + k
                pltpu.make_async_copy(ys_ref.at[pl.ds(ps_ref[idx_ref[j]] + rank_ref[j], 1)],
                                      buf.at[dst_slot, k, pl.ds(t, 1)], sem.at[dst_slot]).start(priority=k % 2)
            return c

        lax.fori_loop(0, tt, start, 0)

    @pl.when(step == 0)
    def _():
        gather(e_ref, rk_ref, 0)

    @pl.when(step + 1 < n_steps)
    def _():
        gather(en_ref, rkn_ref, 1 - slot)

    for k in range(TOP_K):
        pltpu.make_async_copy(ys_ref.at[pl.ds(0, tt)], buf.at[slot, k], sem.at[slot]).wait()

    w = w_ref[0]
    moe = _from_row_tiles(sh_ref[0]).astype(F32)
    for k in range(TOP_K):
        moe = moe + w[:, k:k + 1] * _from_row_tiles(buf[slot, k]).astype(F32)
    x2 = x_ref[0] + gt_ref[0] * moe
    ms = jnp.mean(x2 * x2, axis=-1, keepdims=True)
    o_ref[0] = x2 * lax.rsqrt(ms + NORM_EPS) * g_ref[...]


def _combine(e_flat, rk_flat, pstarts, ys, wsel, shared, x1, gt2, final_g):
    b, t, d = x1.shape
    tt = 128
    nt = t // tt
    last = b * nt - 1
    smem_blk = pl.BlockSpec((tt * TOP_K,), lambda i, j: (i * nt + j,), memory_space=pltpu.SMEM)
    smem_nxt = pl.BlockSpec((tt * TOP_K,), lambda i, j: (jnp.minimum(i * nt + j + 1, last),),
                            memory_space=pltpu.SMEM)
    return pl.pallas_call(
        functools.partial(_combine_kernel, tt=tt),
        grid=(b, nt),
        in_specs=[smem_blk, smem_blk, smem_nxt, smem_nxt,
                  pl.BlockSpec(memory_space=pltpu.SMEM),
                  pl.BlockSpec(memory_space=pl.ANY),
                  pl.BlockSpec((1, tt, LANES), lambda i, j: (i, j, 0)),
                  pl.BlockSpec((1, tt) + ROW_TILE, lambda i, j: (i, j, 0, 0)),
                  pl.BlockSpec((1, tt, d), lambda i, j: (i, j, 0)),
                  pl.BlockSpec((1, 1, d), lambda i, j: (i, 0, 0)),
                  pl.BlockSpec((1, d), lambda i, j: (0, 0))],
        out_specs=pl.BlockSpec((1, tt, d), lambda i, j: (i, j, 0)),
        out_shape=jax.ShapeDtypeStruct((b, t, d), F32),
        scratch_shapes=[pltpu.VMEM((2, TOP_K, tt) + ROW_TILE, BF16), pltpu.SemaphoreType.DMA((2,))],
        compiler_params=_cparams(("arbitrary", "arbitrary")),
        name="combine",
    )(e_flat, rk_flat, e_flat, rk_flat, pstarts, ys, wsel, shared, x1, gt2, final_g.reshape(1, d))


def _pad_rows(w, rows):
    return jnp.zeros((rows, w.shape[1]), w.dtype).at[:w.shape[0]].set(w)


def _lora_pad_cols(w):
    out = jnp.zeros(w.shape[:-1] + (D_LORA_PAD,), w.dtype)
    for i in range(4):
        out = out.at[..., i * LANES:i * LANES + D_LORA].set(w[..., i * D_LORA:(i + 1) * D_LORA])
    return out.at[..., 4 * LANES:].set(w[..., 4 * D_LORA:])


def kernel(x, c, ctx, c_ctx, norm1_g, norm2_g, ada_w, ada_b, w_in, shift_mu, pool_w, pool_scale, w_pool_out, decay_w0, decay_w2, iclr_a0, iclr_a2, gate_g2, k_k, k_a, r_k, lnx_w, lnx_b, w_rwkv_out, w_out, router_w, router_bias, exp_w_gate, exp_w_up, exp_w_down, shared_w_gate, shared_w_up, shared_w_down, final_g):
    B, T, D = x.shape
    TC = ctx.shape[1]
    n = B * T
    l = 0

    w_in_l = w_in[l]
    w_u = w_in_l[:, :D_POOL].astype(BF16)
    w_rkv = w_in_l[:, D_POOL:D_POOL + 3 * D_ATT].astype(BF16)
    w_lora = _lora_pad_cols(w_in_l[:, D_POOL + 3 * D_ATT:D_POOL + 3 * D_ATT + 4 * D_LORA + D_GATE_LORA]).astype(BF16)
    w_gates = w_in_l[:, D_POOL + 3 * D_ATT + 4 * D_LORA + D_GATE_LORA:].astype(BF16)
    mu = shift_mu[l]
    row = lambda a: a.reshape(1, -1)
    pw = {
        "mu_rkv": row(mu[:3 * D_ATT]),
        "mu_lora": row(_lora_pad_cols(mu[3 * D_ATT:])),
        "w2f": _pad_rows(0.5 * decay_w2[l, 0], LANES).astype(BF16),
        "w2b": _pad_rows(0.5 * decay_w2[l, 1], LANES).astype(BF16),
        "a2f": _pad_rows(0.5 * iclr_a2[l, 0], LANES).astype(BF16),
        "a2b": _pad_rows(0.5 * iclr_a2[l, 1], LANES).astype(BF16),
        "w0f": row(0.5 * decay_w0[l, 0]), "w0b": row(0.5 * decay_w0[l, 1]),
        "a0f": row(0.5 * iclr_a0[l, 0]), "a0b": row(0.5 * iclr_a0[l, 1]),
        "k_k": row(k_k[l]), "k_a": row(k_a[l]), "r_k": row(r_k[l]),
    }

    cstack = jnp.zeros((8, D), F32).at[:B].set(c).at[B].set(c_ctx)
    mod = _ada_mod(cstack, ada_w[l], ada_b[l])
    sh1, sc1, gt1, sh2, sc2, gt2 = [mod[:B, i * D:(i + 1) * D].reshape(B, 1, D) for i in range(6)]
    sh1c = jnp.broadcast_to(mod[B, 0:D].reshape(1, 1, D), (B, 1, D))
    sc1c = jnp.broadcast_to(mod[B, D:2 * D].reshape(1, 1, D), (B, 1, D))

    hc = _norm_mod(ctx, norm1_g[l], sc1c, sh1c, TC).reshape(B * TC, D)
    rkv_c = _matmul(hc, w_rkv, F32, 512, 512).reshape(B, TC, 3 * D_ATT)
    lora_c = _matmul(hc, w_lora, F32, 512, D_LORA_PAD).reshape(B, TC, D_LORA_PAD)
    pc = _prepare(rkv_c, lora_c, pw, grid_mode=False)
    flat = lambda a: a.reshape(B * N_PAIRS, a.shape[2], LANES)
    s0 = jnp.zeros((B * N_PAIRS, LANES, LANES), F32)
    vc = flat(pc[10])
    _, st_f = _scan(*[flat(a) for a in pc[0:5]], vc, s0, reverse=False, emit=False)
    _, st_b = _scan(*[flat(a) for a in pc[5:10]], vc, s0, reverse=True, emit=False)

    h = _norm_mod(x, norm1_g[l], sc1, sh1, 512).reshape(n, D)
    u = _matmul(h, w_u, F32, 2048, 512).reshape(B, T, D_POOL)
    rkv = _matmul(h, w_rkv, F32, 2048, 1024).reshape(B, T, 3 * D_ATT)
    lora = _matmul(h, w_lora, F32, 2048, D_LORA_PAD).reshape(B, T, D_LORA_PAD)
    gates = _matmul(h, w_gates, BF16, 2048, 1024).reshape(B, T, 2 * D)
    pp = _prepare(rkv, lora, pw, grid_mode=True)
    v_, bonus, gd = flat(pp[10]), pp[11], pp[12]
    y_f, _ = _scan(*[flat(a) for a in pp[0:5]], v_, st_f, reverse=False, emit=True)
    y_b, _ = _scan(*[flat(a) for a in pp[5:10]], v_, st_b, reverse=True, emit=True)
    y_rwkv = _readout(y_f.reshape(B, N_PAIRS, T, LANES), y_b.reshape(B, N_PAIRS, T, LANES), bonus, gd,
                      row(lnx_w[l]), row(lnx_b[l]), gate_g2[l].astype(BF16), w_rwkv_out[l].astype(BF16))
    y_pool = _pool_branch(u, pool_w[l].astype(BF16), row(pool_scale[l]), w_pool_out[l].astype(BF16))

    router_w_pad = jnp.zeros((D, LANES), F32).at[:, :N_EXPERTS].set(router_w[l])
    x1, h2, logits = _merge(y_pool, y_rwkv, gates, x, gt1, sc2, sh2, norm2_g[l],
                                w_out[l].astype(BF16), router_w_pad)

    bias_pad = jnp.zeros((1, LANES), F32).at[0, :N_EXPERTS].set(router_bias[l])
    e_idx, e_rank, wsel, counts = _router(logits.reshape(n, LANES), bias_pad)
    e_flat = e_idx[:, :TOP_K].reshape(-1)
    rk_flat = e_rank[:, :TOP_K].reshape(-1)
    bm = EXPERT_BLOCK
    cnt = counts[0, :N_EXPERTS].astype(I32)
    padded = (cnt + bm - 1) // bm * bm
    pend = jnp.cumsum(padded)
    pstarts = (pend - padded).astype(I32)
    cap = n * TOP_K + N_EXPERTS * bm
    n_blocks = cap // bm
    block_start = jnp.arange(n_blocks, dtype=I32) * bm
    block_expert = jnp.minimum(jnp.sum(block_start[:, None] >= pend[None, :], axis=-1), N_EXPERTS - 1).astype(I32)
    n_used = (pend[-1] // bm).astype(I32).reshape(1)
    has = cnt > 0
    ids = jnp.where(has, jnp.arange(N_EXPERTS, dtype=I32), N_EXPERTS)
    next_e = jnp.concatenate([lax.cummin(ids, axis=0, reverse=True)[1:], jnp.full((1,), N_EXPERTS, I32)])
    slot_e = ((jnp.cumsum(has.astype(I32)) - 1) & 1).astype(I32)

    h2 = h2.reshape((n,) + ROW_TILE)
    xs, shared = _dispatch(e_flat, rk_flat, pstarts, cnt, h2, cap,
                           shared_w_gate[l], shared_w_up[l], shared_w_down[l])
    ys = _experts(block_expert, n_used, next_e, slot_e, xs, exp_w_gate[l], exp_w_up[l], exp_w_down[l])
    return _combine(e_flat, rk_flat, pstarts, ys, wsel.reshape(B, T, LANES),
                    shared.reshape((B, T) + ROW_TILE), x1, gt2, final_g)
```

```python
import functools

import jax
import jax.numpy as jnp
from jax import lax
from jax.experimental import pallas as pl
from jax.experimental.pallas import tpu as pltpu

F32 = jnp.float32
BF16 = jnp.bfloat16
I32 = jnp.int32

D_MODEL = 2048
GRID_W = 64
POOL_WINDOWS = (2, 4, 8, 16)
POOL_GROUP = 256
D_POOL = 1024
HEAD = 64
N_HEADS = 32
N_PAIRS = N_HEADS // 2
D_ATT = 2048
D_LORA = 96
D_GATE_LORA = 256
D_LORA_PAD = 768
GN_EPS = 64e-5
NORM_EPS = 1e-6
N_EXPERTS = 64
TOP_K = 8
N_GROUPS = 8
TOPK_GROUPS = 4
D_EXPERT = 512
ROUTED_SCALE = 2.5
EXP_M05 = 0.6065306597126334

LANES = 128
CHUNK = 64
EXPERT_BLOCK = 512
VMEM_LIMIT = 56 * 1024 * 1024


def _cparams(sem):
    return pltpu.CompilerParams(dimension_semantics=sem, vmem_limit_bytes=VMEM_LIMIT)


def _dot(a, b):
    return jnp.dot(a, b, preferred_element_type=F32)


def _bmm(a, b):
    return lax.dot_general(a, b, (((2,), (1,)), ((0,), (0,))), preferred_element_type=F32)


def _bmm_nt(a, b):
    return lax.dot_general(a, b, (((2,), (2,)), ((0,), (0,))), preferred_element_type=F32)


def _bmm_tn(a, b):
    return lax.dot_general(a, b, (((1,), (1,)), ((0,), (0,))), preferred_element_type=F32)


def _sigmoid(x):
    return 0.5 * jnp.tanh(0.5 * x) + 0.5


def _split2(x):
    hi = x.astype(BF16)
    lo = (x - hi.astype(F32)).astype(BF16)
    return hi, lo


def _iota(shape, axis):
    return lax.broadcasted_iota(I32, shape, axis)


def _ada_kernel(c_ref, w_ref, b_ref, o_ref):
    c = c_ref[...]
    a = c * _sigmoid(c)
    o_ref[...] = _dot(a.astype(BF16), w_ref[...].astype(BF16)) + b_ref[...]


def _ada_mod(cstack, ada_w, ada_b):
    m, d = cstack.shape
    n = ada_w.shape[1]
    tn = 1024
    return pl.pallas_call(
        _ada_kernel,
        grid=(n // tn,),
        in_specs=[pl.BlockSpec((m, d), lambda j: (0, 0)),
                  pl.BlockSpec((d, tn), lambda j: (0, j)),
                  pl.BlockSpec((1, tn), lambda j: (0, j))],
        out_specs=pl.BlockSpec((m, tn), lambda j: (0, j)),
        out_shape=jax.ShapeDtypeStruct((m, n), F32),
        compiler_params=_cparams(("parallel",)),
        name="ada_mod",
    )(cstack, ada_w, ada_b.reshape(1, n))


def _norm_mod_kernel(x_ref, g_ref, sc_ref, sh_ref, o_ref):
    x = x_ref[0]
    ms = jnp.mean(x * x, axis=-1, keepdims=True)
    y = x * lax.rsqrt(ms + NORM_EPS) * g_ref[...]
    o_ref[0] = (y * (1.0 + sc_ref[0]) + sh_ref[0]).astype(o_ref.dtype)


def _norm_mod(x, g, sc, sh, tt):
    b, t, d = x.shape
    return pl.pallas_call(
        _norm_mod_kernel,
        grid=(b, t // tt),
        in_specs=[pl.BlockSpec((1, tt, d), lambda i, j: (i, j, 0)),
                  pl.BlockSpec((1, d), lambda i, j: (0, 0)),
                  pl.BlockSpec((1, 1, d), lambda i, j: (i, 0, 0)),
                  pl.BlockSpec((1, 1, d), lambda i, j: (i, 0, 0))],
        out_specs=pl.BlockSpec((1, tt, d), lambda i, j: (i, j, 0)),
        out_shape=jax.ShapeDtypeStruct((b, t, d), BF16),
        compiler_params=_cparams(("parallel", "parallel")),
        name="norm_mod",
    )(x, g.reshape(1, d), sc, sh)


def _mm_kernel(a_ref, b_ref, o_ref):
    o_ref[...] = _dot(a_ref[...], b_ref[...]).astype(o_ref.dtype)


def _matmul(a, b, out_dtype, tm, tn):
    m, k = a.shape
    n = b.shape[1]
    tm = min(tm, m)
    return pl.pallas_call(
        _mm_kernel,
        grid=(m // tm, n // tn),
        in_specs=[pl.BlockSpec((tm, k), lambda i, j: (i, 0)),
                  pl.BlockSpec((k, tn), lambda i, j: (0, j))],
        out_specs=pl.BlockSpec((tm, tn), lambda i, j: (i, j)),
        out_shape=jax.ShapeDtypeStruct((m, n), out_dtype),
        compiler_params=_cparams(("parallel", "parallel")),
        name="matmul",
    )(a, b)


def _shift_grid(x, prev, nxt, first, last):
    tt, c = x.shape
    col = _iota((tt, c), 0) & (GRID_W - 1)
    m = _iota((tt, c), 1) & 3
    horiz = jnp.where(m == 0, pltpu.roll(x, 1, 0), pltpu.roll(x, tt - 1, 0))
    at_border = ((m == 0) & (col == 0)) | ((m == 1) & (col == GRID_W - 1))
    horiz = jnp.where(at_border, 0.0, horiz)
    prev = jnp.where(first, 0.0, prev)
    nxt = jnp.where(last, 0.0, nxt)
    if tt > GRID_W:
        up = jnp.concatenate([prev, x[:tt - GRID_W]], axis=0)
        down = jnp.concatenate([x[GRID_W:], nxt], axis=0)
    else:
        up, down = prev, nxt
    return jnp.where(m < 2, horiz, jnp.where(m == 2, up, down))


def _shift_seq(x, prev8, next8, first, last):
    t, c = x.shape
    row = _iota((t, c), 0)
    odd = (_iota((t, c), 1) & 1) == 1
    before = jnp.where(first, 0.0, prev8[7:8])
    after = jnp.where(last, 0.0, next8[0:1])
    prev = jnp.where(row == 0, before, pltpu.roll(x, 1, 0))
    nxt = jnp.where(row == t - 1, after, pltpu.roll(x, t - 1, 0))
    return jnp.where(odd, nxt, prev)


def _head_sum(x):
    w = 2 * LANES
    ones = (_iota((w, w), 0) >> 6 == _iota((w, w), 1) >> 6).astype(BF16)
    outs = []
    for c in range(x.shape[1] // w):
        hi, lo = _split2(x[:, c * w:(c + 1) * w])
        outs.append(_dot(hi, ones) + _dot(lo, ones))
    return jnp.concatenate(outs, axis=1)


def _prepare_kernel(*refs, grid_mode):
    (r_ref, rp_ref, rn_ref, k_ref, kp_ref, kn_ref, v_ref, vp_ref, vn_ref,
     l_ref, lp_ref, ln_ref) = refs[:12]
    rest = refs[12:]
    (mur_ref, muk_ref, muv_ref, mul_ref, w2f_ref, w2b_ref, a2f_ref, a2b_ref,
     w0f_ref, w0b_ref, a0f_ref, a0b_ref, kk_ref, ka_ref, rk_ref,
     oaf_ref, obf_ref, okf_ref, orf_ref, owf_ref, oab_ref, obb_ref, okb_ref, orb_ref, owb_ref,
     ov_ref, obon_ref, ogd_ref) = rest

    first = pl.program_id(1) == 0
    last = pl.program_id(1) == pl.num_programs(1) - 1
    shift = _shift_grid if grid_mode else _shift_seq

    def mix(x_ref, p_ref, n_ref, mu_ref):
        x = x_ref[0]
        return x + (shift(x, p_ref[0], n_ref[0], first, last) - x) * mu_ref[...]

    r = mix(r_ref, rp_ref, rn_ref, mur_ref)
    k = mix(k_ref, kp_ref, kn_ref, muk_ref)
    v = mix(v_ref, vp_ref, vn_ref, muv_ref)
    lo = mix(l_ref, lp_ref, ln_ref, mul_ref)

    th = jnp.tanh(lo[:, :2 * LANES]).astype(BF16)
    tzf = jnp.tanh(w0f_ref[...] + _dot(th[:, :LANES], w2f_ref[...]))
    tzb = jnp.tanh(w0b_ref[...] + _dot(th[:, LANES:], w2b_ref[...]))
    c = -0.5 * EXP_M05
    lwf = tzf * c + c
    lwb = tzb * c + c
    ad = lo[:, 2 * LANES:4 * LANES].astype(BF16)
    af = 0.5 * jnp.tanh(a0f_ref[...] + _dot(ad[:, :LANES], a2f_ref[...])) + 0.5
    ab = 0.5 * jnp.tanh(a0b_ref[...] + _dot(ad[:, LANES:], a2b_ref[...])) + 0.5
    ogd_ref[0] = lo[:, 4 * LANES:]

    kk = k * kk_ref[...]
    kk = kk * lax.rsqrt(_head_sum(kk * kk) + 1e-12)
    ka = ka_ref[...]
    one_minus_ka = 1.0 - ka
    kf = k * (af * ka + one_minus_ka)
    kb = k * (ab * ka + one_minus_ka)
    bonus = _head_sum(r * (kf + kb) * rk_ref[...]) * v

    tt = r.shape[0]
    t2 = _iota((tt, tt), 0)
    s2 = _iota((tt, tt), 1)
    same = (t2 >> 6) == (s2 >> 6)

    def scan_operands(lw, b, kd, reverse):
        tri = (same & ((s2 >= t2) if reverse else (s2 <= t2))).astype(BF16)
        hi = lw.astype(BF16)
        r1 = lw - hi.astype(F32)
        mid = r1.astype(BF16)
        lo3 = (r1 - mid.astype(F32)).astype(BF16)
        cum = _dot(tri, hi) + _dot(tri, mid) + _dot(tri, lo3)
        w_inc = jnp.exp(cum)
        w_inv = jnp.exp(-cum)
        ends = [w_inc[c * CHUNK:c * CHUNK + 1] if reverse else w_inc[(c + 1) * CHUNK - 1:(c + 1) * CHUNK]
                for c in range(tt // CHUNK)]
        w_last = jnp.concatenate(ends + [jnp.zeros((8 - len(ends), lw.shape[1]), F32)], axis=0)
        return jnp.exp(cum - lw) * kk, b * w_inv, kd * w_inv, w_inc * r, w_last

    al_f, be_f, ka_f, rh_f, wl_f = scan_operands(lwf, kk * af, kf, False)
    al_b, be_b, ka_b, rh_b, wl_b = scan_operands(lwb, kk * ab, kb, True)
    outs = ((oaf_ref, al_f), (obf_ref, be_f), (okf_ref, ka_f), (orf_ref, rh_f), (owf_ref, wl_f),
            (oab_ref, al_b), (obb_ref, be_b), (okb_ref, ka_b), (orb_ref, rh_b), (owb_ref, wl_b),
            (ov_ref, v), (obon_ref, bonus))
    for o_ref, val in outs:
        for p in range(N_PAIRS):
            o_ref[0, p] = val[:, p * LANES:(p + 1) * LANES].astype(o_ref.dtype)


def _prepare(rkv, lora, pw, grid_mode):
    b, t, _ = rkv.shape
    d = D_ATT
    tt = 2 * GRID_W
    halo = GRID_W if grid_mode else 8
    hpt = tt // halo
    nhalo = t // halo
    grid = (b, t // tt)

    def tile_specs(w, c):
        return [pl.BlockSpec((1, tt, w), lambda i, j: (i, j, c)),
                pl.BlockSpec((1, halo, w), lambda i, j: (i, jnp.maximum(j * hpt - 1, 0), c)),
                pl.BlockSpec((1, halo, w), lambda i, j: (i, jnp.minimum((j + 1) * hpt, nhalo - 1), c))]

    in_specs = []
    args = []
    for c in range(3):
        in_specs += tile_specs(d, c)
        args += [rkv, rkv, rkv]
    in_specs += tile_specs(D_LORA_PAD, 0)
    args += [lora, lora, lora]

    def vec(c=0, w=d):
        return pl.BlockSpec((1, w), lambda i, j, c=c: (0, c))

    def full(shape):
        return pl.BlockSpec(shape, lambda i, j: (0,) * len(shape))

    in_specs += [vec(0), vec(1), vec(2), vec(0, D_LORA_PAD)]
    args += [pw["mu_rkv"], pw["mu_rkv"], pw["mu_rkv"], pw["mu_lora"]]
    in_specs += [full((LANES, d))] * 4
    args += [pw["w2f"], pw["w2b"], pw["a2f"], pw["a2b"]]
    in_specs += [vec()] * 7
    args += [pw["w0f"], pw["w0b"], pw["a0f"], pw["a0b"], pw["k_k"], pw["k_a"], pw["r_k"]]

    pair_spec = pl.BlockSpec((1, N_PAIRS, tt, LANES), lambda i, j: (i, 0, j, 0))
    wl_spec = pl.BlockSpec((1, N_PAIRS, 8, LANES), lambda i, j: (i, 0, j, 0))

    def pair(dtype):
        return jax.ShapeDtypeStruct((b, N_PAIRS, t, LANES), dtype)

    wl = jax.ShapeDtypeStruct((b, N_PAIRS, t // tt * 8, LANES), F32)
    direction = [pair(BF16)] * 4 + [wl]
    out_shape = direction * 2 + [pair(BF16), pair(F32), jax.ShapeDtypeStruct((b, t, D_GATE_LORA), F32)]
    out_specs = ([pair_spec] * 4 + [wl_spec]) * 2 + [pair_spec, pair_spec,
                                                    pl.BlockSpec((1, tt, D_GATE_LORA), lambda i, j: (i, j, 0))]
    return pl.pallas_call(
        functools.partial(_prepare_kernel, grid_mode=grid_mode),
        grid=grid, in_specs=in_specs, out_specs=out_specs, out_shape=out_shape,
        compiler_params=_cparams(("parallel", "parallel")),
        name="prepare_grid" if grid_mode else "prepare_seq",
    )(*args)


def _scan_kernel(*refs, reverse, pairs, tb, emit):
    al_ref, be_ref, ka_ref, rh_ref, wl_ref, v_ref, s0_ref = refs[:7]
    if emit:
        y_ref, st_ref, s_scr = refs[7:]
    else:
        st_ref, s_scr = refs[7:]
    L = CHUNK
    n_chunks = tb // L

    @pl.when(pl.program_id(1) == 0)
    def _():
        s_scr[...] = s0_ref[...]

    t2 = _iota((2 * L, 2 * L), 0)
    s2 = _iota((2 * L, 2 * L), 1)
    same = (t2 >> 6) == (s2 >> 6)
    tl = t2 & (L - 1)
    sl = s2 & (L - 1)
    strict = same & ((sl > tl) if reverse else (sl < tl))
    incl = same & ((sl >= tl) if reverse else (sl <= tl))
    eye = (t2 == s2).astype(F32)
    head_a = _iota((pairs, L, LANES), 2) < HEAD
    zero = jnp.zeros((), BF16)

    def stack(ref, rows):
        x = ref[:, rows, :]
        return jnp.concatenate([jnp.where(head_a, x, zero), jnp.where(head_a, zero, x)], axis=1)

    def chunk(ci, carry):
        cc = (n_chunks - 1 - ci) if reverse else ci
        rows = pl.ds(pl.multiple_of(cc * L, L), L)
        a_s = stack(al_ref, rows)
        b_s = stack(be_ref, rows)
        k_s = stack(ka_ref, rows)
        v_s = stack(v_ref, rows)
        lhs = jnp.concatenate([a_s, stack(rh_ref, rows)], axis=1) if emit else a_s
        xb = _bmm_nt(lhs, b_s)
        xk = _bmm_nt(lhs, k_s)
        m1 = jnp.where(strict, xb[:, :2 * L], 0.0)
        m2 = jnp.where(strict, xk[:, :2 * L], 0.0)
        xp = -m1
        tinv = eye + xp
        for _ in range(5):
            xq = xp.astype(BF16)
            xp = _bmm(xq, xq)
            tinv = tinv + _bmm(tinv.astype(BF16), xp.astype(BF16))
        s = s_scr[...]
        xs = _bmm_nt(lhs, s.astype(BF16))
        if emit:
            n1 = jnp.where(incl, xb[:, 2 * L:], 0.0)
            n2 = jnp.where(incl, xk[:, 2 * L:], 0.0)
            xv = _bmm(jnp.concatenate([m2, n2], axis=1).astype(BF16), v_s)
        else:
            xv = _bmm(m2.astype(BF16), v_s)
        g = xs[:, :2 * L] + xv[:, :2 * L]
        u_s = (-_bmm(tinv.astype(BF16), g.astype(BF16))).astype(BF16)
        if emit:
            y2 = xs[:, 2 * L:] + xv[:, 2 * L:] + _bmm(n1.astype(BF16), u_s)
            y_ref[:, rows, :] = y2[:, :L] + y2[:, L:]
        w_last = wl_ref[:, pl.ds((cc >> 1) * 8 + (cc & 1), 1), :]
        upd = _bmm_tn(jnp.concatenate([u_s, v_s], axis=1), jnp.concatenate([b_s, k_s], axis=1))
        s_scr[...] = (s + upd) * w_last
        return carry

    lax.fori_loop(0, n_chunks, chunk, 0, unroll=2)

    @pl.when(pl.program_id(1) == pl.num_programs(1) - 1)
    def _():
        st_ref[...] = s_scr[...]


def _scan(al, be, ka, rh, wl, v, s0, reverse, emit, pairs=32, tb=256):
    bp, t, _ = al.shape
    pairs = min(pairs, bp)
    tb = min(t, tb)
    nb = t // tb

    def tmap(g, c):
        return (g, (nb - 1 - c) if reverse else c, 0)

    data = pl.BlockSpec((pairs, tb, LANES), tmap)
    wl_spec = pl.BlockSpec((pairs, tb // (2 * CHUNK) * 8, LANES), tmap)
    state = pl.BlockSpec((pairs, LANES, LANES), lambda g, c: (g, 0, 0))
    out_shape = [jax.ShapeDtypeStruct((bp, LANES, LANES), F32)]
    out_specs = [state]
    if emit:
        out_shape = [jax.ShapeDtypeStruct((bp, t, LANES), F32)] + out_shape
        out_specs = [data] + out_specs
    res = pl.pallas_call(
        functools.partial(_scan_kernel, reverse=reverse, pairs=pairs, tb=tb, emit=emit),
        grid=(bp // pairs, nb),
        in_specs=[data] * 4 + [wl_spec, data, state],
        out_specs=out_specs, out_shape=out_shape,
        scratch_shapes=[pltpu.VMEM((pairs, LANES, LANES), F32)],
        compiler_params=_cparams(("parallel", "arbitrary")),
        name="scan_" + ("bwd" if reverse else "fwd") + ("_emit" if emit else "_state"),
    )(al, be, ka, rh, wl, v, s0)
    return (res[0], res[1]) if emit else (None, res[0])


def _readout_kernel(yf_ref, yb_ref, bon_ref, gd_ref, lnw_ref, lnb_ref, g2_ref, w_ref, o_ref):
    y = jnp.concatenate([yf_ref[0, p] + yb_ref[0, p] for p in range(N_PAIRS)], axis=1)
    bonus = jnp.concatenate([bon_ref[0, p] for p in range(N_PAIRS)], axis=1)
    mean = _head_sum(y) * (1.0 / HEAD)
    dlt = y - mean
    var = _head_sum(dlt * dlt) * (1.0 / HEAD)
    yn = dlt * lax.rsqrt(var + GN_EPS) * lnw_ref[...] + lnb_ref[...]
    gate = _dot(_sigmoid(gd_ref[0]).astype(BF16), g2_ref[...])
    out = ((yn + bonus) * gate).astype(BF16)
    o_ref[0] = _dot(out, w_ref[...]).astype(o_ref.dtype)


def _readout(yf, yb, bonus, gd, lnw, lnb, g2, w_out):
    b, _, t, _ = yf.shape
    tt = 256
    d = D_ATT
    pair_spec = pl.BlockSpec((1, N_PAIRS, tt, LANES), lambda i, j: (i, 0, j, 0))
    return pl.pallas_call(
        _readout_kernel,
        grid=(b, t // tt),
        in_specs=[pair_spec, pair_spec, pair_spec,
                  pl.BlockSpec((1, tt, D_GATE_LORA), lambda i, j: (i, j, 0)),
                  pl.BlockSpec((1, d), lambda i, j: (0, 0)),
                  pl.BlockSpec((1, d), lambda i, j: (0, 0)),
                  pl.BlockSpec((D_GATE_LORA, d), lambda i, j: (0, 0)),
                  pl.BlockSpec((d, D_MODEL), lambda i, j: (0, 0))],
        out_specs=pl.BlockSpec((1, tt, D_MODEL), lambda i, j: (i, j, 0)),
        out_shape=jax.ShapeDtypeStruct((b, t, D_MODEL), BF16),
        compiler_params=_cparams(("parallel", "parallel")),
        name="readout",
    )(yf, yb, bonus, gd, lnw, lnb, g2, w_out)


def _pool_kernel(u_ref, pw_ref, ps_ref, wo_ref, o_ref):
    u = u_ref[0]
    tt = u.shape[0]
    t2 = _iota((tt, tt), 0)
    s2 = _iota((tt, tt), 1)
    same = (t2 >> 6) == (s2 >> 6)
    tc = t2 & (GRID_W - 1)
    sc = s2 & (GRID_W - 1)
    col = _iota((tt, POOL_GROUP), 0) & (GRID_W - 1)
    ys = []
    for gi, w in enumerate(POOL_WINDOWS):
        ug = u[:, gi * POOL_GROUP:(gi + 1) * POOL_GROUP]
        win = (same & (sc >= tc - w // 2) & (sc < tc + (w - w // 2))).astype(BF16)
        hi, lo = _split2(ug)
        wsum = _dot(win, hi) + _dot(win, lo)
        cnt = (jnp.minimum(col + (w - w // 2), GRID_W) - jnp.maximum(col - w // 2, 0)).astype(F32)
        dlt = wsum / cnt - ug
        ys.append(_dot(dlt.astype(BF16), pw_ref[gi]))
    y1 = jnp.concatenate(ys, axis=1) * ps_ref[...]
    o_ref[0] = _dot(y1.astype(BF16), wo_ref[...]).astype(o_ref.dtype)


def _pool_branch(u, pool_w, pool_scale, w_pool_out):
    b, t, _ = u.shape
    tt = 256
    return pl.pallas_call(
        _pool_kernel,
        grid=(b, t // tt),
        in_specs=[pl.BlockSpec((1, tt, D_POOL), lambda i, j: (i, j, 0)),
                  pl.BlockSpec((4, POOL_GROUP, POOL_GROUP), lambda i, j: (0, 0, 0)),
                  pl.BlockSpec((1, D_POOL), lambda i, j: (0, 0)),
                  pl.BlockSpec((D_POOL, D_MODEL), lambda i, j: (0, 0))],
        out_specs=pl.BlockSpec((1, tt, D_MODEL), lambda i, j: (i, j, 0)),
        out_shape=jax.ShapeDtypeStruct((b, t, D_MODEL), BF16),
        compiler_params=_cparams(("parallel", "parallel")),
        name="pool_branch",
    )(u, pool_w, pool_scale, w_pool_out)


ROW_TILE = (D_MODEL // LANES, LANES)


def _to_row_tiles(x):
    return x.astype(BF16).reshape((x.shape[0],) + ROW_TILE)


def _from_row_tiles(x):
    return x.reshape(x.shape[0], D_MODEL)


def _merge_kernel(yp_ref, yr_ref, gp_ref, gr_ref, x_ref, gt_ref, sc_ref, sh_ref, g_ref, w_ref, rw_ref,
                  x1_ref, h_ref, lg_ref):
    m = (_sigmoid(gp_ref[0].astype(F32)) * yp_ref[0].astype(F32)
         + _sigmoid(gr_ref[0].astype(F32)) * yr_ref[0].astype(F32))
    x1 = x_ref[0] + gt_ref[0] * _dot(m.astype(BF16), w_ref[...])
    x1_ref[0] = x1
    ms = jnp.mean(x1 * x1, axis=-1, keepdims=True)
    h = x1 * lax.rsqrt(ms + NORM_EPS) * g_ref[...]
    h = h * (1.0 + sc_ref[0]) + sh_ref[0]
    h_ref[0] = _to_row_tiles(h)
    hh, hl = _split2(h)
    rh, rl = _split2(rw_ref[...])
    lg_ref[0] = _dot(hh, rh) + _dot(hl, rh) + _dot(hh, rl)


def _merge(y_pool, y_rwkv, gates, x, gt1, sc2, sh2, g2, w_out, router_w_pad):
    b, t, d = x.shape
    tt = 256
    tile = pl.BlockSpec((1, tt, d), lambda i, j: (i, j, 0))
    mod = pl.BlockSpec((1, 1, d), lambda i, j: (i, 0, 0))
    return pl.pallas_call(
        _merge_kernel,
        grid=(b, t // tt),
        in_specs=[tile, tile,
                  pl.BlockSpec((1, tt, d), lambda i, j: (i, j, 0)),
                  pl.BlockSpec((1, tt, d), lambda i, j: (i, j, 1)),
                  tile, mod, mod, mod,
                  pl.BlockSpec((1, d), lambda i, j: (0, 0)),
                  pl.BlockSpec((d, d), lambda i, j: (0, 0)),
                  pl.BlockSpec((d, LANES), lambda i, j: (0, 0))],
        out_specs=[tile,
                   pl.BlockSpec((1, tt) + ROW_TILE, lambda i, j: (i, j, 0, 0)),
                   pl.BlockSpec((1, tt, LANES), lambda i, j: (i, j, 0))],
        out_shape=[jax.ShapeDtypeStruct((b, t, d), F32),
                   jax.ShapeDtypeStruct((b, t) + ROW_TILE, BF16),
                   jax.ShapeDtypeStruct((b, t, LANES), F32)],
        compiler_params=_cparams(("parallel", "parallel")),
        name="merge",
    )(y_pool, y_rwkv, gates, gates, x, gt1, sc2, sh2, g2.reshape(1, d), w_out, router_w_pad)


def _router_kernel(lg_ref, bias_ref, e_ref, rk_ref, w_ref, cnt_ref, carry):
    tt = lg_ref.shape[0]
    shape = (tt, LANES)
    lane = _iota(shape, 1)
    valid = lane < N_EXPERTS
    grp = (lane & (N_EXPERTS - 1)) >> 3
    neg = jnp.float32(-jnp.inf)

    @pl.when(pl.program_id(0) == 0)
    def _():
        carry[...] = jnp.zeros_like(carry)

    scores = _sigmoid(lg_ref[...])
    sel = scores + bias_ref[...]
    sel = jnp.where(valid, sel, pltpu.roll(sel, N_EXPERTS, 1))

    def group_reduce(x, op):
        for sh in (1, 2, 4):
            up = pltpu.roll(x, sh, 1)
            dn = pltpu.roll(x, LANES - sh, 1)
            x = op(x, jnp.where((lane & sh) != 0, up, dn))
        return x

    m1 = group_reduce(sel, jnp.maximum)
    first = group_reduce(jnp.where(sel == m1, lane, LANES), jnp.minimum)
    m2 = group_reduce(jnp.where(lane == first, neg, sel), jnp.maximum)
    gs = m1 + m2
    beaten = jnp.zeros(shape, I32)
    for k in range(1, N_GROUPS):
        other = pltpu.roll(gs, 8 * k, 1)
        og = (grp - k) & (N_GROUPS - 1)
        beaten = beaten + ((other > gs) | ((other == gs) & (og < grp))).astype(I32)
    cur = jnp.where((beaten < TOPK_GROUPS) & valid, sel, neg)

    picked = jnp.zeros(shape, jnp.bool_)
    e_acc = jnp.zeros(shape, I32)
    w_acc = jnp.zeros(shape, F32)
    idxs = []
    for k in range(TOP_K):
        m = jnp.max(cur, axis=1, keepdims=True)
        idx = jnp.min(jnp.where(cur == m, lane, LANES), axis=1, keepdims=True)
        oh = lane == idx
        sc = jnp.sum(jnp.where(oh, scores, 0.0), axis=1, keepdims=True)
        e_acc = jnp.where(lane == k, idx, e_acc)
        w_acc = jnp.where(lane == k, sc, w_acc)
        picked = picked | oh
        cur = jnp.where(oh, neg, cur)
        idxs.append(idx)
    wsum = jnp.sum(w_acc, axis=1, keepdims=True)
    w_ref[...] = w_acc / wsum * ROUTED_SCALE
    e_ref[...] = e_acc

    lower = (_iota((tt, tt), 1) < _iota((tt, tt), 0)).astype(BF16)
    pk = picked.astype(BF16)
    before = _dot(lower, pk) + carry[...]
    r_acc = jnp.zeros(shape, F32)
    for k in range(TOP_K):
        rk = jnp.sum(jnp.where(lane == idxs[k], before, 0.0), axis=1, keepdims=True)
        r_acc = jnp.where(lane == k, rk, r_acc)
    rk_ref[...] = r_acc.astype(I32)
    carry[...] = carry[...] + jnp.sum(picked.astype(F32), axis=0, keepdims=True)
    cnt_ref[...] = carry[...]


def _router(logits, bias_pad):
    n = logits.shape[0]
    tt = min(1024, n)
    tile = pl.BlockSpec((tt, LANES), lambda i: (i, 0))
    row = pl.BlockSpec((1, LANES), lambda i: (0, 0))
    return pl.pallas_call(
        _router_kernel,
        grid=(n // tt,),
        in_specs=[tile, row],
        out_specs=[tile, tile, tile, row],
        out_shape=[jax.ShapeDtypeStruct((n, LANES), I32), jax.ShapeDtypeStruct((n, LANES), I32),
                   jax.ShapeDtypeStruct((n, LANES), F32), jax.ShapeDtypeStruct((1, LANES), F32)],
        scratch_shapes=[pltpu.VMEM((1, LANES), F32)],
        compiler_params=_cparams(("arbitrary",)),
        name="router",
    )(logits, bias_pad)


def _swiglu(xb, wg, wu, wd):
    g = _dot(xb, wg.astype(BF16))
    u = _dot(xb, wu.astype(BF16))
    act = (g * _sigmoid(g) * u).astype(BF16)
    return _dot(act, wd.astype(BF16))


def _dispatch_kernel(e_ref, rk_ref, ps_ref, cnt_ref, h_ref, wg_ref, wu_ref, wd_ref, xs_ref, sh_ref, zbuf, sem, zsem,
                     *, tt):
    bm = EXPERT_BLOCK

    @pl.when(pl.program_id(0) == 0)
    def _():
        zbuf[...] = jnp.zeros_like(zbuf)

        def tail_copy(e):
            last = ps_ref[e] + ((cnt_ref[e] + bm - 1) & -bm) - bm
            return pltpu.make_async_copy(zbuf, xs_ref.at[pl.ds(pl.multiple_of(last, bm), bm)], zsem)

        def tail_start(e, c):
            @pl.when((cnt_ref[e] & (bm - 1)) != 0)
            def _():
                tail_copy(e).start()
            return c

        def tail_wait(e, c):
            @pl.when((cnt_ref[e] & (bm - 1)) != 0)
            def _():
                tail_copy(e).wait()
            return c

        lax.fori_loop(0, N_EXPERTS, tail_start, 0)
        lax.fori_loop(0, N_EXPERTS, tail_wait, 0)

        last_e = N_EXPERTS - 1
        n_used = (ps_ref[last_e] + ((cnt_ref[last_e] + bm - 1) & -bm)) // bm
        n_blocks = xs_ref.shape[0] // bm

        def spare_copy(b):
            return pltpu.make_async_copy(zbuf, xs_ref.at[pl.ds(pl.multiple_of(b * bm, bm), bm)], zsem)

        def spare_start(b, c):
            spare_copy(b).start()
            return c

        def spare_wait(b, c):
            spare_copy(b).wait()
            return c

        lax.fori_loop(n_used, n_blocks, spare_start, 0)
        lax.fori_loop(n_used, n_blocks, spare_wait, 0)

    def row_copy(src_row, dst_row):
        return pltpu.make_async_copy(h_ref.at[pl.ds(src_row, 1)], xs_ref.at[pl.ds(dst_row, 1)], sem)

    def start(t, c):
        for k in range(TOP_K):
            j = t * TOP_K + k
            row_copy(t, ps_ref[e_ref[j]] + rk_ref[j]).start(priority=k % 2)
        return c

    lax.fori_loop(0, tt, start, 0)

    sh_ref[...] = _to_row_tiles(_swiglu(_from_row_tiles(h_ref[...]), wg_ref[...], wu_ref[...], wd_ref[...]))

    for _ in range(TOP_K):
        pltpu.make_async_copy(h_ref, xs_ref.at[pl.ds(0, tt)], sem).wait()


def _dispatch(e_flat, rk_flat, pstarts, counts, h_rows, cap, wg, wu, wd):
    n = h_rows.shape[0]
    tt = min(512, n)
    smem_blk = pl.BlockSpec((tt * TOP_K,), lambda i: (i,), memory_space=pltpu.SMEM)
    rows = pl.BlockSpec((tt,) + ROW_TILE, lambda i: (i, 0, 0))

    def const(w):
        return pl.BlockSpec(w.shape, lambda i: (0, 0), pipeline_mode=pl.Buffered(1))

    return pl.pallas_call(
        functools.partial(_dispatch_kernel, tt=tt),
        grid=(n // tt,),
        in_specs=[smem_blk, smem_blk,
                  pl.BlockSpec(memory_space=pltpu.SMEM),
                  pl.BlockSpec(memory_space=pltpu.SMEM),
                  rows, const(wg), const(wu), const(wd)],
        out_specs=[pl.BlockSpec(memory_space=pl.ANY), rows],
        out_shape=[jax.ShapeDtypeStruct((cap,) + ROW_TILE, h_rows.dtype),
                   jax.ShapeDtypeStruct((n,) + ROW_TILE, h_rows.dtype)],
        scratch_shapes=[pltpu.VMEM((EXPERT_BLOCK,) + ROW_TILE, h_rows.dtype),
                        pltpu.SemaphoreType.DMA(()), pltpu.SemaphoreType.DMA(())],
        compiler_params=_cparams(("arbitrary",)),
        name="dispatch",
    )(e_flat, rk_flat, pstarts, counts, h_rows, wg, wu, wd)


def _expert_kernel(be_ref, nu_ref, nx_ref, sl_ref, x_ref, wg_hbm, wu_hbm, wd_hbm, o_ref, wg_buf, wu_buf, wd_buf, sem):
    i = pl.program_id(0)
    used = i < nu_ref[0]
    e = be_ref[i]
    first = (i == 0) | (e != be_ref[jnp.maximum(i - 1, 0)])
    slot = sl_ref[e]
    nxt = nx_ref[e]

    def fetch(expert, s):
        return (pltpu.make_async_copy(wg_hbm.at[expert], wg_buf.at[s], sem.at[s, 0]),
                pltpu.make_async_copy(wu_hbm.at[expert], wu_buf.at[s], sem.at[s, 1]),
                pltpu.make_async_copy(wd_hbm.at[expert], wd_buf.at[s], sem.at[s, 2]))

    @pl.when(used & (i == 0))
    def _():
        for c in fetch(e, slot):
            c.start()

    @pl.when(used & first)
    def _():
        for c in fetch(e, slot):
            c.wait()

        @pl.when(nxt < N_EXPERTS)
        def _():
            for c in fetch(nxt, 1 - slot):
                c.start()

    @pl.when(used)
    def _():
        y = _swiglu(_from_row_tiles(x_ref[...]), wg_buf[slot], wu_buf[slot], wd_buf[slot])
        o_ref[...] = _to_row_tiles(y)

    @pl.when(jnp.logical_not(used))
    def _():
        o_ref[...] = jnp.zeros_like(o_ref)


def _experts(block_expert, n_used, next_expert, slot, xs, wg, wu, wd):
    cap = xs.shape[0]
    bm = EXPERT_BLOCK
    de = wg.shape[2]
    grid_spec = pltpu.PrefetchScalarGridSpec(
        num_scalar_prefetch=4,
        grid=(cap // bm,),
        in_specs=[pl.BlockSpec((bm,) + ROW_TILE, lambda i, be, nu, nx, sl: (jnp.minimum(i, nu[0] - 1), 0, 0)),
                  pl.BlockSpec(memory_space=pl.ANY),
                  pl.BlockSpec(memory_space=pl.ANY),
                  pl.BlockSpec(memory_space=pl.ANY)],
        out_specs=pl.BlockSpec((bm,) + ROW_TILE, lambda i, be, nu, nx, sl: (i, 0, 0)),
        scratch_shapes=[pltpu.VMEM((2, D_MODEL, de), F32), pltpu.VMEM((2, D_MODEL, de), F32),
                        pltpu.VMEM((2, de, D_MODEL), F32), pltpu.SemaphoreType.DMA((2, 3))],
    )
    return pl.pallas_call(
        _expert_kernel,
        grid_spec=grid_spec,
        out_shape=jax.ShapeDtypeStruct((cap,) + ROW_TILE, BF16),
        compiler_params=_cparams(("arbitrary",)),
        name="experts",
    )(block_expert, n_used, next_expert, slot, xs, wg, wu, wd)


def _combine_kernel(e_ref, rk_ref, en_ref, rkn_ref, ps_ref, ys_ref, w_ref, sh_ref, x_ref, gt_ref, g_ref, o_ref,
                    buf, sem, *, tt):
    step = pl.program_id(0) * pl.num_programs(1) + pl.program_id(1)
    n_steps = pl.num_programs(0) * pl.num_programs(1)
    slot = step & 1

    def gather(idx_ref, rank_ref, dst_slot):
        def start(t, c):
            for k in range(TOP_K):
                j = t * TOP_K + k
                pltpu.make_async_copy(ys_ref.at[pl.ds(ps_ref[idx_ref[j]] + rank_ref[j], 1)],
                                      buf.at[dst_slot, k, pl.ds(t, 1)], sem.at[dst_slot]).start(priority=k % 2)
            return c

        lax.fori_loop(0, tt, start, 0)

    @pl.when(step == 0)
    def _():
        gather(e_ref, rk_ref, 0)

    @pl.when(step + 1 < n_steps)
    def _():
        gather(en_ref, rkn_ref, 1 - slot)

    for k in range(TOP_K):
        pltpu.make_async_copy(ys_ref.at[pl.ds(0, tt)], buf.at[slot, k], sem.at[slot]).wait()

    w = w_ref[0]
    moe = _from_row_tiles(sh_ref[0]).astype(F32)
    for k in range(TOP_K):
        moe = moe + w[:, k:k + 1] * _from_row_tiles(buf[slot, k]).astype(F32)
    x2 = x_ref[0] + gt_ref[0] * moe
    ms = jnp.mean(x2 * x2, axis=-1, keepdims=True)
    o_ref[0] = x2 * lax.rsqrt(ms + NORM_EPS) * g_ref[...]


def _combine(e_flat, rk_flat, pstarts, ys, wsel, shared, x1, gt2, final_g):
    b, t, d = x1.shape
    tt = 128
    nt = t // tt
    last = b * nt - 1
    smem_blk = pl.BlockSpec((tt * TOP_K,), lambda i, j: (i * nt + j,), memory_space=pltpu.SMEM)
    smem_nxt = pl.BlockSpec((tt * TOP_K,), lambda i, j: (jnp.minimum(i * nt + j + 1, last),),
                            memory_space=pltpu.SMEM)
    return pl.pallas_call(
        functools.partial(_combine_kernel, tt=tt),
        grid=(b, nt),
        in_specs=[smem_blk, smem_blk, smem_nxt, smem_nxt,
                  pl.BlockSpec(memory_space=pltpu.SMEM),
                  pl.BlockSpec(memory_space=pl.ANY),
                  pl.BlockSpec((1, tt, LANES), lambda i, j: (i, j, 0)),
                  pl.BlockSpec((1, tt) + ROW_TILE, lambda i, j: (i, j, 0, 0)),
                  pl.BlockSpec((1, tt, d), lambda i, j: (i, j, 0)),
                  pl.BlockSpec((1, 1, d), lambda i, j: (i, 0, 0)),
                  pl.BlockSpec((1, d), lambda i, j: (0, 0))],
        out_specs=pl.BlockSpec((1, tt, d), lambda i, j: (i, j, 0)),
        out_shape=jax.ShapeDtypeStruct((b, t, d), F32),
        scratch_shapes=[pltpu.VMEM((2, TOP_K, tt) + ROW_TILE, BF16), pltpu.SemaphoreType.DMA((2,))],
        compiler_params=_cparams(("arbitrary", "arbitrary")),
        name="combine",
    )(e_flat, rk_flat, e_flat, rk_flat, pstarts, ys, wsel, shared, x1, gt2, final_g.reshape(1, d))


def _pad_rows(w, rows):
    return jnp.zeros((rows, w.shape[1]), w.dtype).at[:w.shape[0]].set(w)


def _lora_pad_cols(w):
    out = jnp.zeros(w.shape[:-1] + (D_LORA_PAD,), w.dtype)
    for i in range(4):
        out = out.at[..., i * LANES:i * LANES + D_LORA].set(w[..., i * D_LORA:(i + 1) * D_LORA])
    return out.at[..., 4 * LANES:].set(w[..., 4 * D_LORA:])


def kernel(x, c, ctx, c_ctx, norm1_g, norm2_g, ada_w, ada_b, w_in, shift_mu, pool_w, pool_scale, w_pool_out, decay_w0, decay_w2, iclr_a0, iclr_a2, gate_g2, k_k, k_a, r_k, lnx_w, lnx_b, w_rwkv_out, w_out, router_w, router_bias, exp_w_gate, exp_w_up, exp_w_down, shared_w_gate, shared_w_up, shared_w_down, final_g):
    B, T, D = x.shape
    TC = ctx.shape[1]
    n = B * T
    l = 0

    w_in_l = w_in[l]
    w_u = w_in_l[:, :D_POOL].astype(BF16)
    w_rkv = w_in_l[:, D_POOL:D_POOL + 3 * D_ATT].astype(BF16)
    w_lora = _lora_pad_cols(w_in_l[:, D_POOL + 3 * D_ATT:D_POOL + 3 * D_ATT + 4 * D_LORA + D_GATE_LORA]).astype(BF16)
    w_gates = w_in_l[:, D_POOL + 3 * D_ATT + 4 * D_LORA + D_GATE_LORA:].astype(BF16)
    mu = shift_mu[l]
    row = lambda a: a.reshape(1, -1)
    pw = {
        "mu_rkv": row(mu[:3 * D_ATT]),
        "mu_lora": row(_lora_pad_cols(mu[3 * D_ATT:])),
        "w2f": _pad_rows(0.5 * decay_w2[l, 0], LANES).astype(BF16),
        "w2b": _pad_rows(0.5 * decay_w2[l, 1], LANES).astype(BF16),
        "a2f": _pad_rows(0.5 * iclr_a2[l, 0], LANES).astype(BF16),
        "a2b": _pad_rows(0.5 * iclr_a2[l, 1], LANES).astype(BF16),
        "w0f": row(0.5 * decay_w0[l, 0]), "w0b": row(0.5 * decay_w0[l, 1]),
        "a0f": row(0.5 * iclr_a0[l, 0]), "a0b": row(0.5 * iclr_a0[l, 1]),
        "k_k": row(k_k[l]), "k_a": row(k_a[l]), "r_k": row(r_k[l]),
    }

    cstack = jnp.zeros((8, D), F32).at[:B].set(c).at[B].set(c_ctx)
    mod = _ada_mod(cstack, ada_w[l], ada_b[l])
    sh1, sc1, gt1, sh2, sc2, gt2 = [mod[:B, i * D:(i + 1) * D].reshape(B, 1, D) for i in range(6)]
    sh1c = jnp.broadcast_to(mod[B, 0:D].reshape(1, 1, D), (B, 1, D))
    sc1c = jnp.broadcast_to(mod[B, D:2 * D].reshape(1, 1, D), (B, 1, D))

    hc = _norm_mod(ctx, norm1_g[l], sc1c, sh1c, TC).reshape(B * TC, D)
    rkv_c = _matmul(hc, w_rkv, F32, 512, 512).reshape(B, TC, 3 * D_ATT)
    lora_c = _matmul(hc, w_lora, F32, 512, D_LORA_PAD).reshape(B, TC, D_LORA_PAD)
    pc = _prepare(rkv_c, lora_c, pw, grid_mode=False)
    flat = lambda a: a.reshape(B * N_PAIRS, a.shape[2], LANES)
    s0 = jnp.zeros((B * N_PAIRS, LANES, LANES), F32)
    vc = flat(pc[10])
    _, st_f = _scan(*[flat(a) for a in pc[0:5]], vc, s0, reverse=False, emit=False)
    _, st_b = _scan(*[flat(a) for a in pc[5:10]], vc, s0, reverse=True, emit=False)

    h = _norm_mod(x, norm1_g[l], sc1, sh1, 512).reshape(n, D)
    u = _matmul(h, w_u, F32, 2048, 512).reshape(B, T, D_POOL)
    rkv = _matmul(h, w_rkv, F32, 2048, 1024).reshape(B, T, 3 * D_ATT)
    lora = _matmul(h, w_lora, F32, 2048, D_LORA_PAD).reshape(B, T, D_LORA_PAD)
    gates = _matmul(h, w_gates, BF16, 2048, 1024).reshape(B, T, 2 * D)
    pp = _prepare(rkv, lora, pw, grid_mode=True)
    v_, bonus, gd = flat(pp[10]), pp[11], pp[12]
    y_f, _ = _scan(*[flat(a) for a in pp[0:5]], v_, st_f, reverse=False, emit=True)
    y_b, _ = _scan(*[flat(a) for a in pp[5:10]], v_, st_b, reverse=True, emit=True)
    y_rwkv = _readout(y_f.reshape(B, N_PAIRS, T, LANES), y_b.reshape(B, N_PAIRS, T, LANES), bonus, gd,
                      row(lnx_w[l]), row(lnx_b[l]), gate_g2[l].astype(BF16), w_rwkv_out[l].astype(BF16))
    y_pool = _pool_branch(u, pool_w[l].astype(BF16), row(pool_scale[l]), w_pool_out[l].astype(BF16))

    router_w_pad = jnp.zeros((D, LANES), F32).at[:, :N_EXPERTS].set(router_w[l])
    x1, h2, logits = _merge(y_pool, y_rwkv, gates, x, gt1, sc2, sh2, norm2_g[l],
                                w_out[l].astype(BF16), router_w_pad)

    bias_pad = jnp.zeros((1, LANES), F32).at[0, :N_EXPERTS].set(router_bias[l])
    e_idx, e_rank, wsel, counts = _router(logits.reshape(n, LANES), bias_pad)
    e_flat = e_idx[:, :TOP_K].reshape(-1)
    rk_flat = e_rank[:, :TOP_K].reshape(-1)
    bm = EXPERT_BLOCK
    cnt = counts[0, :N_EXPERTS].astype(I32)
    padded = (cnt + bm - 1) // bm * bm
    pend = jnp.cumsum(padded)
    pstarts = (pend - padded).astype(I32)
    cap = n * TOP_K + N_EXPERTS * bm
    n_blocks = cap // bm
    block_start = jnp.arange(n_blocks, dtype=I32) * bm
    block_expert = jnp.minimum(jnp.sum(block_start[:, None] >= pend[None, :], axis=-1), N_EXPERTS - 1).astype(I32)
    n_used = (pend[-1] // bm).astype(I32).reshape(1)
    has = cnt > 0
    ids = jnp.where(has, jnp.arange(N_EXPERTS, dtype=I32), N_EXPERTS)
    next_e = jnp.concatenate([lax.cummin(ids, axis=0, reverse=True)[1:], jnp.full((1,), N_EXPERTS, I32)])
    slot_e = ((jnp.cumsum(has.astype(I32)) - 1) & 1).astype(I32)

    h2 = h2.reshape((n,) + ROW_TILE)
    xs, shared = _dispatch(e_flat, rk_flat, pstarts, cnt, h2, cap,
                           shared_w_gate[l], shared_w_up[l], shared_w_down[l])
    ys = _experts(block_expert, n_used, next_e, slot_e, xs, exp_w_gate[l], exp_w_up[l], exp_w_down[l])
    return _combine(e_flat, rk_flat, pstarts, ys, wsel.reshape(B, T, LANES),
                    shared.reshape((B, T) + ROW_TILE), x1, gt2, final_g)
```
